```python
import math
import jax, jax.numpy as jnp
from jax import lax
import numpy as np

D_MODEL = 2048
BATCH = 1
SEQ = 16384
DEPTH = 2

Q_BLOCK = 128
A_HEADS = 8
A_QK_DIM = 64
A_V_DIM = 2 * A_QK_DIM
B_HEADS = 8
B_Q_LORA = 512
B_KV_LORA = 512
B_NOPE_DIM = 128
B_ROPE_DIM = 64
B_V_DIM = 128
ROPE_THETA = 10000.0
C_HEADS = 16
C_HEAD_DIM = D_MODEL // C_HEADS
FFN_DIM = 5632
N_EXPERTS = 8
TOP_K = 2
EXPERT_DIM = 5632
EVEN_IN_DIM = 2 * A_HEADS * 2 * A_QK_DIM + A_HEADS * A_V_DIM + B_Q_LORA + B_KV_LORA + B_ROPE_DIM
EVEN_MIX_DIM = A_HEADS * A_V_DIM + B_HEADS * B_V_DIM
ODD_IN_DIM = 3 * C_HEADS * C_HEAD_DIM + C_HEADS
N_EVEN = (DEPTH + 1) // 2
N_ODD = DEPTH // 2

kernel_name = "hybrid_diffattn_mla_fox_moe"


def rms_norm(x, g, eps=1e-6):
    xf = x.astype(jnp.float32)
    y = xf * lax.rsqrt(jnp.mean(xf * xf, axis=-1, keepdims=True) + eps)
    return (y * g.astype(jnp.float32)).astype(x.dtype)


def split_sizes(x, sizes):
    idx, acc = [], 0
    for s in sizes[:-1]:
        acc += s
        idx.append(acc)
    return jnp.split(x, idx, axis=-1)


def rope_tables(seq, dim):
    inv = ROPE_THETA ** (-jnp.arange(0, dim, 2, dtype=jnp.float32) / dim)
    ang = jnp.arange(seq, dtype=jnp.float32)[:, None] * inv[None, :]
    return jnp.cos(ang), jnp.sin(ang)


def apply_rope(x, cos, sin):
    xf = x.astype(jnp.float32)
    x1, x2 = jnp.split(xf, 2, axis=-1)
    return jnp.concatenate([x1 * cos - x2 * sin, x2 * cos + x1 * sin], axis=-1).astype(x.dtype)


def block_offsets(i, seq):
    qpos = i * Q_BLOCK + jnp.arange(Q_BLOCK)
    return qpos[:, None] - jnp.arange(seq)[None, :]


def sweep_query_blocks(block_fn, seq):
    out = lax.map(block_fn, jnp.arange(seq // Q_BLOCK))
    nb, b, qb, h, d = out.shape
    return jnp.moveaxis(out, 0, 1).reshape(b, nb * qb, h, d)


def diff_attention(q, k, v, lam, slopes):
    seq = q.shape[1]
    scale = A_QK_DIM ** -0.5

    def block(i):
        qb = lax.dynamic_slice_in_dim(q, i * Q_BLOCK, Q_BLOCK, axis=1)
        dist = block_offsets(i, seq)
        logits = jnp.einsum('bqhmd,bkhmd->bhmqk', qb, k).astype(jnp.float32) * scale
        logits = logits - slopes[:, None, None, None] * dist.astype(jnp.float32)
        logits = jnp.where(dist >= 0, logits, -jnp.inf)
        p = jax.nn.softmax(logits, axis=-1)
        w = (p[:, :, 0] - lam * p[:, :, 1]).astype(v.dtype)
        return jnp.einsum('bhqk,bkhd->bqhd', w, v)

    return sweep_query_blocks(block, seq)


def mla_attention(q_nope, q_pe, k_nope, k_pe, v):
    seq = q_nope.shape[1]
    scale = (B_NOPE_DIM + B_ROPE_DIM) ** -0.5

    def block(i):
        qn = lax.dynamic_slice_in_dim(q_nope, i * Q_BLOCK, Q_BLOCK, axis=1)
        qp = lax.dynamic_slice_in_dim(q_pe, i * Q_BLOCK, Q_BLOCK, axis=1)
        dist = block_offsets(i, seq)
        logits = (jnp.einsum('bqhd,bkhd->bhqk', qn, k_nope)
                  + jnp.einsum('bqhr,bkr->bhqk', qp, k_pe)).astype(jnp.float32) * scale
        logits = jnp.where(dist >= 0, logits, -jnp.inf)
        p = jax.nn.softmax(logits, axis=-1).astype(v.dtype)
        return jnp.einsum('bhqk,bkhd->bqhd', p, v)

    return sweep_query_blocks(block, seq)


def forgetting_attention(q, k, v, log_f):
    seq = q.shape[1]
    scale = C_HEAD_DIM ** -0.5
    cum = jnp.cumsum(log_f, axis=1).transpose(0, 2, 1)

    def block(i):
        qb = lax.dynamic_slice_in_dim(q, i * Q_BLOCK, Q_BLOCK, axis=1)
        cq = lax.dynamic_slice_in_dim(cum, i * Q_BLOCK, Q_BLOCK, axis=2)
        dist = block_offsets(i, seq)
        logits = jnp.einsum('bqhd,bkhd->bhqk', qb, k).astype(jnp.float32) * scale
        logits = logits + (cq[..., :, None] - cum[..., None, :])
        logits = jnp.where(dist >= 0, logits, -jnp.inf)
        p = jax.nn.softmax(logits, axis=-1).astype(v.dtype)
        return jnp.einsum('bhqk,bkhd->bqhd', p, v)

    return sweep_query_blocks(block, seq)


def even_mixer(h, w_in, q_norm, w_uq, kv_norm, w_ukv, lq1, lk1, lq2, lk2, subln, w_out, lambda_init):
    b, s, _ = h.shape
    proj = h @ w_in
    aq, ak, av, cq, ckv, kpe = split_sizes(
        proj, [A_HEADS * 2 * A_QK_DIM, A_HEADS * 2 * A_QK_DIM, A_HEADS * A_V_DIM,
               B_Q_LORA, B_KV_LORA, B_ROPE_DIM])
    aq = aq.reshape(b, s, A_HEADS, 2, A_QK_DIM)
    ak = ak.reshape(b, s, A_HEADS, 2, A_QK_DIM)
    av = av.reshape(b, s, A_HEADS, A_V_DIM)
    f32 = jnp.float32
    lam = (jnp.exp(jnp.sum(lq1.astype(f32) * lk1.astype(f32)))
           - jnp.exp(jnp.sum(lq2.astype(f32) * lk2.astype(f32))) + lambda_init)
    slopes = jnp.exp2(-8.0 * jnp.arange(1, A_HEADS + 1, dtype=f32) / A_HEADS)
    oa = diff_attention(aq, ak, av, lam, slopes)
    oa = (rms_norm(oa, subln, eps=1e-5) * (1.0 - lambda_init)).reshape(b, s, A_HEADS * A_V_DIM)
    cos, sin = rope_tables(s, B_ROPE_DIM)
    qb = (rms_norm(cq, q_norm) @ w_uq).reshape(b, s, B_HEADS, B_NOPE_DIM + B_ROPE_DIM)
    q_nope, q_pe = jnp.split(qb, [B_NOPE_DIM], axis=-1)
    q_pe = apply_rope(q_pe, cos[None, :, None, :], sin[None, :, None, :])
    kv = (rms_norm(ckv, kv_norm) @ w_ukv).reshape(b, s, B_HEADS, B_NOPE_DIM + B_V_DIM)
    k_nope, vb = jnp.split(kv, [B_NOPE_DIM], axis=-1)
    k_pe = apply_rope(kpe, cos[None], sin[None])
    ob = mla_attention(q_nope, q_pe, k_nope, k_pe, vb).reshape(b, s, B_HEADS * B_V_DIM)
    return jnp.concatenate([oa, ob], axis=-1) @ w_out


def odd_mixer(h, w_in, forget_bias, w_out):
    b, s, _ = h.shape
    proj = h @ w_in
    width = C_HEADS * C_HEAD_DIM
    q, k, v, f = split_sizes(proj, [width, width, width, C_HEADS])
    log_f = jax.nn.log_sigmoid((f + forget_bias).astype(jnp.float32))
    shp = (b, s, C_HEADS, C_HEAD_DIM)
    o = forgetting_attention(q.reshape(shp), k.reshape(shp), v.reshape(shp), log_f)
    return o.reshape(b, s, width) @ w_out


def swiglu(h, w_gate, w_up, w_down):
    return (jax.nn.silu(h @ w_gate) * (h @ w_up)) @ w_down


def moe_swiglu(h, router, w_gate, w_up, w_down):
    logits = (h @ router).astype(jnp.float32)
    probs = jax.nn.softmax(logits, axis=-1)
    top_p, top_i = lax.top_k(probs, TOP_K)
    top_p = top_p / jnp.sum(top_p, axis=-1, keepdims=True)
    gates = jnp.sum(jax.nn.one_hot(top_i, N_EXPERTS, dtype=jnp.float32) * top_p[..., None], axis=-2)
    out = jnp.zeros_like(h)
    for e in range(N_EXPERTS):
        out = out + gates[..., e:e + 1].astype(h.dtype) * swiglu(h, w_gate[e], w_up[e], w_down[e])
    return out


def setup_inputs(seed: int = 0) -> dict:
    key = jax.random.key(seed)
    ks = iter(jax.random.split(key, 32))
    f32 = jnp.float32

    def w(shape, fan_in):
        return jax.random.normal(next(ks), shape, f32) * fan_in ** -0.5

    def gain(shape):
        return 1.0 + 0.02 * jax.random.normal(next(ks), shape, f32)

    E, O, D = N_EVEN, N_ODD, D_MODEL
    return {
        "x": jax.random.normal(next(ks), (BATCH, SEQ, D), f32),
        "ev_attn_norm": gain((E, D)),
        "ev_w_in": w((E, D, EVEN_IN_DIM), D),
        "ev_q_norm": gain((E, B_Q_LORA)),
        "ev_w_uq": w((E, B_Q_LORA, B_HEADS * (B_NOPE_DIM + B_ROPE_DIM)), B_Q_LORA),
        "ev_kv_norm": gain((E, B_KV_LORA)),
        "ev_w_ukv": w((E, B_KV_LORA, B_HEADS * (B_NOPE_DIM + B_V_DIM)), B_KV_LORA),
        "ev_lambda_q1": 0.1 * jax.random.normal(next(ks), (E, A_QK_DIM), f32),
        "ev_lambda_k1": 0.1 * jax.random.normal(next(ks), (E, A_QK_DIM), f32),
        "ev_lambda_q2": 0.1 * jax.random.normal(next(ks), (E, A_QK_DIM), f32),
        "ev_lambda_k2": 0.1 * jax.random.normal(next(ks), (E, A_QK_DIM), f32),
        "ev_subln": gain((E, A_V_DIM)),
        "ev_w_out": w((E, EVEN_MIX_DIM, D), EVEN_MIX_DIM),
        "ev_ffn_norm": gain((E, D)),
        "ev_ffn_w_gate": w((E, D, FFN_DIM), D),
        "ev_ffn_w_up": w((E, D, FFN_DIM), D),
        "ev_ffn_w_down": w((E, FFN_DIM, D), FFN_DIM),
        "od_attn_norm": gain((O, D)),
        "od_w_in": w((O, D, ODD_IN_DIM), D),
        "od_forget_bias": jax.random.uniform(next(ks), (O, C_HEADS), f32, 1.0, 4.0),
        "od_w_out": w((O, C_HEADS * C_HEAD_DIM, D), C_HEADS * C_HEAD_DIM),
        "od_ffn_norm": gain((O, D)),
        "od_router": w((O, D, N_EXPERTS), D),
        "od_moe_w_gate": w((O, N_EXPERTS, D, EXPERT_DIM), D),
        "od_moe_w_up": w((O, N_EXPERTS, D, EXPERT_DIM), D),
        "od_moe_w_down": w((O, N_EXPERTS, EXPERT_DIM, D), EXPERT_DIM),
        "final_norm": gain((D,)),
    }


def reference(x, ev_attn_norm, ev_w_in, ev_q_norm, ev_w_uq, ev_kv_norm, ev_w_ukv,
              ev_lambda_q1, ev_lambda_k1, ev_lambda_q2, ev_lambda_k2, ev_subln, ev_w_out,
              ev_ffn_norm, ev_ffn_w_gate, ev_ffn_w_up, ev_ffn_w_down,
              od_attn_norm, od_w_in, od_forget_bias, od_w_out,
              od_ffn_norm, od_router, od_moe_w_gate, od_moe_w_up, od_moe_w_down,
              final_norm):
    h = x
    for layer in range(DEPTH):
        j = layer // 2
        if layer % 2 == 0:
            lambda_init = 0.8 - 0.6 * math.exp(-0.3 * layer)
            h = h + even_mixer(rms_norm(h, ev_attn_norm[j]), ev_w_in[j], ev_q_norm[j], ev_w_uq[j],
                               ev_kv_norm[j], ev_w_ukv[j], ev_lambda_q1[j], ev_lambda_k1[j],
                               ev_lambda_q2[j], ev_lambda_k2[j], ev_subln[j], ev_w_out[j], lambda_init)
            h = h + swiglu(rms_norm(h, ev_ffn_norm[j]), ev_ffn_w_gate[j], ev_ffn_w_up[j], ev_ffn_w_down[j])
        else:
            h = h + odd_mixer(rms_norm(h, od_attn_norm[j]), od_w_in[j], od_forget_bias[j], od_w_out[j])
            h = h + moe_swiglu(rms_norm(h, od_ffn_norm[j]), od_router[j], od_moe_w_gate[j],
                               od_moe_w_up[j], od_moe_w_down[j])
    return rms_norm(h, final_norm)
```

```python
import functools
import math

import jax
import jax.numpy as jnp
from jax import lax
from jax.experimental import pallas as pl
from jax.experimental.pallas import tpu as pltpu

F32 = jnp.float32
BF16 = jnp.bfloat16

V7X_LANES = 128
V7X_VMEM_REQUEST_CAP = 60000 * 1024

A_HEADS, A_QK, A_V = 8, 64, 128
B_HEADS, B_LORA, B_NOPE, B_ROPE, B_V = 8, 512, 128, 64, 128
C_HEADS, C_DIM = 16, 128
N_EXPERTS, TOP_K = 8, 2
ROPE_THETA = 10000.0
MASK_VALUE = -1e30

ATTN_BLOCK = 512
PROJ_TM, PROJ_TN = 1024, 512
FFN_TM, FFN_TF = 512, 512
MOE_TM = 512
GATHER_ROWS = 256
PREP_TM = 512
GATE_TM = 1024
ROUTER_TM = 512


def _cparams(semantics, vmem_bytes):
    return pltpu.CompilerParams(
        dimension_semantics=semantics,
        vmem_limit_bytes=int(min(V7X_VMEM_REQUEST_CAP, vmem_bytes)))


def _rms(x, g, eps):
    return x * lax.rsqrt(jnp.mean(x * x, axis=-1, keepdims=True) + eps) * g


def _rms_matmul_kernel(x_ref, g_ref, w_ref, cs_ref, o_ref, xn_ref, *, eps):
    @pl.when(pl.program_id(1) == 0)
    def _():
        xn_ref[...] = _rms(x_ref[...], g_ref[...], eps).astype(BF16)

    acc = jnp.dot(xn_ref[...], w_ref[...], preferred_element_type=F32)
    o_ref[...] = (acc * cs_ref[...]).astype(o_ref.dtype)


def rms_matmul(x, g, w, col_scale, out_dtype, *, eps=1e-6, tm=PROJ_TM, tn=PROJ_TN):
    m, k = x.shape
    n = w.shape[1]
    assert m % tm == 0 and n % tn == 0, (m, n, tm, tn)
    vmem = 2 * tm * k * 4 + tm * k * 2 + 2 * k * tn * 2 + 4 * tm * tn * 4 + (4 << 20)
    return pl.pallas_call(
        functools.partial(_rms_matmul_kernel, eps=eps),
        grid=(m // tm, n // tn),
        in_specs=[
            pl.BlockSpec((tm, k), lambda i, j: (i, 0)),
            pl.BlockSpec((1, k), lambda i, j: (0, 0)),
            pl.BlockSpec((k, tn), lambda i, j: (0, j)),
            pl.BlockSpec((1, tn), lambda i, j: (0, j)),
        ],
        out_specs=pl.BlockSpec((tm, tn), lambda i, j: (i, j)),
        out_shape=jax.ShapeDtypeStruct((m, n), out_dtype),
        scratch_shapes=[pltpu.VMEM((tm, k), BF16)],
        compiler_params=_cparams(("parallel", "arbitrary"), vmem),
        name="rms_matmul",
    )(x, g.reshape(1, k), w, col_scale.reshape(1, n))


def _proj_residual_kernel(*refs, n_in):
    a_refs, w_refs = refs[:n_in], refs[n_in:2 * n_in]
    h_ref, o_ref = refs[2 * n_in], refs[2 * n_in + 1]
    acc = h_ref[...]
    for a_ref, w_ref in zip(a_refs, w_refs):
        acc = acc + jnp.dot(a_ref[...], w_ref[...], preferred_element_type=F32)
    o_ref[...] = acc


def proj_residual(a_list, w_list, h, *, tm=PROJ_TM, tn=PROJ_TN):
    m, n = h.shape
    n_in = len(a_list)
    assert m % tm == 0 and n % tn == 0
    ks = [a.shape[1] for a in a_list]
    vmem = sum(2 * tm * k * 2 + 2 * k * tn * 2 for k in ks) + 6 * tm * tn * 4 + (4 << 20)
    in_specs = [pl.BlockSpec((tm, k), lambda i, j: (i, 0)) for k in ks]
    in_specs += [pl.BlockSpec((k, tn), lambda i, j: (0, j)) for k in ks]
    in_specs += [pl.BlockSpec((tm, tn), lambda i, j: (i, j))]
    return pl.pallas_call(
        functools.partial(_proj_residual_kernel, n_in=n_in),
        grid=(m // tm, n // tn),
        in_specs=in_specs,
        out_specs=pl.BlockSpec((tm, tn), lambda i, j: (i, j)),
        out_shape=jax.ShapeDtypeStruct((m, n), F32),
        compiler_params=_cparams(("parallel", "arbitrary"), vmem),
        name="proj_residual",
    )(*a_list, *w_list, h)


def _flash_attend(qs, k_ref, v_ref, qi, bias_fn, m_scr, l_scr, acc_scr, *, bq):
    rows = qs.shape[0]
    bk = bq
    n_chunks = bk // V7X_LANES
    m_scr[...] = jnp.full(m_scr.shape, MASK_VALUE, F32)
    l_scr[...] = jnp.zeros(l_scr.shape, F32)
    acc_scr[...] = jnp.zeros(acc_scr.shape, F32)

    def step(j, masked):
        start = pl.multiple_of(j * bk, bk)
        k = k_ref[pl.ds(start, bk), :]
        v = v_ref[pl.ds(start, bk), :]
        s = lax.dot_general(qs, k, (((1,), (1,)), ((), ())), preferred_element_type=F32)
        if bias_fn is not None:
            s = s + bias_fn(start)
        if masked:
            row = lax.broadcasted_iota(jnp.int32, (rows, bk), 0) & (bq - 1)
            col = lax.broadcasted_iota(jnp.int32, (rows, bk), 1)
            s = jnp.where(col <= row, s, MASK_VALUE)
        chunks = [s[:, c * V7X_LANES:(c + 1) * V7X_LANES] for c in range(n_chunks)]
        m_lane = chunks[0]
        for c in chunks[1:]:
            m_lane = jnp.maximum(m_lane, c)
        m_prev = m_scr[...]
        m_next = jnp.maximum(m_prev, jnp.max(m_lane, axis=1, keepdims=True))
        alpha = jnp.exp(m_prev - m_next)
        p_chunks = [jnp.exp(c - m_next) for c in chunks]
        l_new = alpha * l_scr[...]
        for p in p_chunks:
            l_new = l_new + p
        l_scr[...] = l_new
        m_scr[...] = m_next
        p = jnp.concatenate([p.astype(BF16) for p in p_chunks], axis=1)
        acc_scr[...] = acc_scr[...] * alpha + jnp.dot(p, v, preferred_element_type=F32)

    def full_step(j, carry):
        step(j, False)
        return carry

    lax.fori_loop(0, qi, full_step, 0)
    step(qi, True)
    l = jnp.sum(l_scr[...], axis=1, keepdims=True)
    return acc_scr[...] / l


def _attn_scratch(rows):
    return [pltpu.VMEM((rows, V7X_LANES), F32) for _ in range(3)]


def _attn_vmem(seq, rows, bq, kv_width):
    resident = 2 * seq * kv_width * 2
    temps = 6 * rows * bq * 4 + 8 * rows * V7X_LANES * 4
    return resident + temps + (6 << 20)


def _diff_attn_kernel(slope_ref, q_ref, k_ref, v_ref, lq1_ref, lk1_ref, lq2_ref, lk2_ref,
                      subln_ref, o_ref, m_scr, l_scr, acc_scr, *, bq, lambda_init):
    h, qi = pl.program_id(0), pl.program_id(1)
    q = q_ref[...]
    lane = lax.broadcasted_iota(jnp.int32, q.shape, 1)
    zero = jnp.zeros_like(q)
    qs = jnp.concatenate([jnp.where(lane < A_QK, q, zero), jnp.where(lane >= A_QK, q, zero)], axis=0)
    slope = slope_ref[h]
    t0 = qi * bq

    def bias_fn(start):
        kpos = lax.broadcasted_iota(jnp.int32, (1, bq), 1) + (start - t0)
        return slope * kpos.astype(F32)

    o = _flash_attend(qs, k_ref, v_ref, qi, bias_fn, m_scr, l_scr, acc_scr, bq=bq)
    lam = (jnp.exp(jnp.sum(lq1_ref[...] * lk1_ref[...], axis=1, keepdims=True))
           - jnp.exp(jnp.sum(lq2_ref[...] * lk2_ref[...], axis=1, keepdims=True)) + lambda_init)
    d = o[:bq] - lam * o[bq:]
    o_ref[...] = (_rms(d, subln_ref[...], 1e-5) * (1.0 - lambda_init)).astype(o_ref.dtype)


def diff_attention(qkv, lq1, lk1, lq2, lk2, subln, lambda_init, *, bq=ATTN_BLOCK):
    seq = qkv.shape[0]
    nq = seq // bq
    slopes = jnp.exp2(-8.0 * jnp.arange(1, A_HEADS + 1, dtype=F32) / A_HEADS)
    vec = lambda: pl.BlockSpec((1, A_QK), lambda h, i, s: (0, 0))
    grid_spec = pltpu.PrefetchScalarGridSpec(
        num_scalar_prefetch=1,
        grid=(A_HEADS, nq),
        in_specs=[
            pl.BlockSpec((bq, 128), lambda h, i, s: (i, h)),
            pl.BlockSpec((seq, 128), lambda h, i, s: (0, A_HEADS + h)),
            pl.BlockSpec((seq, 128), lambda h, i, s: (0, 2 * A_HEADS + h)),
            vec(), vec(), vec(), vec(),
            pl.BlockSpec((1, A_V), lambda h, i, s: (0, 0)),
        ],
        out_specs=pl.BlockSpec((bq, A_V), lambda h, i, s: (i, h)),
        scratch_shapes=_attn_scratch(2 * bq),
    )
    return pl.pallas_call(
        functools.partial(_diff_attn_kernel, bq=bq, lambda_init=lambda_init),
        grid_spec=grid_spec,
        out_shape=jax.ShapeDtypeStruct((seq, A_HEADS * A_V), BF16),
        compiler_params=_cparams(("parallel", "arbitrary"), _attn_vmem(seq, 2 * bq, bq, 256)),
        name="diff_attention",
    )(slopes, qkv, qkv, qkv, lq1.reshape(1, -1), lk1.reshape(1, -1), lq2.reshape(1, -1),
      lk2.reshape(1, -1), subln.reshape(1, -1))


def _mla_prep_kernel(c_ref, qn_ref, kvn_ref, wq1_ref, wq2_ref, wkn_ref, wv_ref, cos_ref, sin_ref,
                     q_ref, k_ref, v_ref, *, scale):
    c = c_ref[...]
    cqn = _rms(c[:, :B_LORA], qn_ref[...], 1e-6).astype(BF16)
    ckvn = _rms(c[:, B_LORA:2 * B_LORA], kvn_ref[...], 1e-6).astype(BF16)
    kp = c[:, 2 * B_LORA:]
    cos, sin = cos_ref[...], sin_ref[...]
    qa = jnp.dot(cqn, wq1_ref[...], preferred_element_type=F32)
    qb = jnp.dot(cqn, wq2_ref[...], preferred_element_type=F32)
    kn = jnp.dot(ckvn, wkn_ref[...], preferred_element_type=F32)
    v_ref[...] = jnp.dot(ckvn, wv_ref[...], preferred_element_type=F32).astype(v_ref.dtype)
    kr = (kp * cos + pltpu.roll(kp, 64, axis=1) * sin).astype(k_ref.dtype)
    for h in range(B_HEADS):
        lo, hi = 256 * h, 256 * h + 128
        q_ref[:, lo:hi] = (qa[:, lo:hi] * scale).astype(q_ref.dtype)
        pe = qa[:, hi:hi + 128] * cos + qb[:, 128 * h:128 * (h + 1)] * sin
        q_ref[:, hi:hi + 128] = (pe * scale).astype(q_ref.dtype)
        k_ref[:, lo:hi] = kn[:, 128 * h:128 * (h + 1)].astype(k_ref.dtype)
        k_ref[:, hi:hi + 128] = kr


def mla_prep(c, q_norm, kv_norm, wq1, wq2, wkn, wv, cos_t, sin_t, *, tm=PREP_TM):
    seq = c.shape[0]
    scale = (B_NOPE + B_ROPE) ** -0.5
    full = lambda a: pl.BlockSpec(a.shape, lambda i: (0, 0))
    row = lambda w: pl.BlockSpec((tm, w), lambda i: (i, 0))
    qn, kvn = q_norm.reshape(1, -1), kv_norm.reshape(1, -1)
    weights = 2 * 2 * (wq1.size + wq2.size + wkn.size + wv.size)
    vmem = weights + 2 * tm * (1152 * 4 + 256 * 4 + 5120 * 2) + tm * 5120 * 4 * 2 + (6 << 20)
    return pl.pallas_call(
        functools.partial(_mla_prep_kernel, scale=scale),
        grid=(seq // tm,),
        in_specs=[row(c.shape[1]), full(qn), full(kvn), full(wq1), full(wq2), full(wkn), full(wv),
                  row(128), row(128)],
        out_specs=[row(2048), row(2048), row(1024)],
        out_shape=[jax.ShapeDtypeStruct((seq, 2048), BF16), jax.ShapeDtypeStruct((seq, 2048), BF16),
                   jax.ShapeDtypeStruct((seq, 1024), BF16)],
        compiler_params=_cparams(("parallel",), vmem),
        name="mla_prep",
    )(c, qn, kvn, wq1, wq2, wkn, wv, cos_t, sin_t)


def _mla_attn_kernel(q_ref, k_ref, v_ref, o_ref, m_scr, l_scr, acc_scr, *, bq):
    o = _flash_attend(q_ref[...], k_ref, v_ref, pl.program_id(1), None, m_scr, l_scr, acc_scr, bq=bq)
    o_ref[...] = o.astype(o_ref.dtype)


def mla_attention(q, k, v, *, bq=ATTN_BLOCK):
    seq = q.shape[0]
    return pl.pallas_call(
        functools.partial(_mla_attn_kernel, bq=bq),
        grid=(B_HEADS, seq // bq),
        in_specs=[
            pl.BlockSpec((bq, 256), lambda h, i: (i, h)),
            pl.BlockSpec((seq, 256), lambda h, i: (0, h)),
            pl.BlockSpec((seq, 128), lambda h, i: (0, h)),
        ],
        out_specs=pl.BlockSpec((bq, B_V), lambda h, i: (i, h)),
        out_shape=jax.ShapeDtypeStruct((seq, B_HEADS * B_V), BF16),
        scratch_shapes=_attn_scratch(bq),
        compiler_params=_cparams(("parallel", "arbitrary"), _attn_vmem(seq, bq, bq, 384)),
        name="mla_attention",
    )(q, k, v)


def _forget_gate_kernel(h_ref, g_ref, wf_ref, b_ref, cum_ref, carry_scr):
    @pl.when(pl.program_id(0) == 0)
    def _():
        carry_scr[...] = jnp.zeros(carry_scr.shape, F32)

    xn = _rms(h_ref[...], g_ref[...], 1e-6).astype(BF16)
    f = lax.dot_general(wf_ref[...], xn, (((1,), (1,)), ((), ())), preferred_element_type=F32)
    z = f + b_ref[...]
    log_f = jnp.minimum(z, 0.0) - jnp.log(1.0 + jnp.exp(-jnp.abs(z)))
    r = lax.broadcasted_iota(jnp.int32, (V7X_LANES, V7X_LANES), 0)
    c = lax.broadcasted_iota(jnp.int32, (V7X_LANES, V7X_LANES), 1)
    tri = (r <= c).astype(F32)
    carry = carry_scr[...]
    for j in range(log_f.shape[1] // V7X_LANES):
        sl = slice(j * V7X_LANES, (j + 1) * V7X_LANES)
        loc = jnp.dot(log_f[:, sl], tri, preferred_element_type=F32,
                      precision=lax.Precision.HIGHEST) + carry
        cum_ref[:, sl] = loc
        carry = jnp.broadcast_to(loc[:, V7X_LANES - 1:], carry.shape)
    carry_scr[...] = carry


def forget_gate_cumsum(h, g, wf_t, bias, *, tm=GATE_TM):
    seq, d = h.shape
    nh = wf_t.shape[0]
    return pl.pallas_call(
        _forget_gate_kernel,
        grid=(seq // tm,),
        in_specs=[
            pl.BlockSpec((tm, d), lambda i: (i, 0)),
            pl.BlockSpec((1, d), lambda i: (0, 0)),
            pl.BlockSpec((nh, d), lambda i: (0, 0)),
            pl.BlockSpec((nh, 1), lambda i: (0, 0)),
        ],
        out_specs=pl.BlockSpec((nh, tm), lambda i: (0, i)),
        out_shape=jax.ShapeDtypeStruct((nh, seq), F32),
        scratch_shapes=[pltpu.VMEM((nh, V7X_LANES), F32)],
        compiler_params=_cparams(("arbitrary",), 2 * tm * d * 4 + 3 * tm * d * 4 + (4 << 20)),
        name="forget_gate_cumsum",
    )(h, g.reshape(1, d), wf_t, bias.reshape(nh, 1))


def _fox_attn_kernel(q_ref, k_ref, v_ref, cum_ref, o_ref, m_scr, l_scr, acc_scr, *, bq):
    qi = pl.program_id(1)
    t0 = pl.multiple_of(qi * bq, bq)
    c0 = cum_ref[:, pl.ds(t0, V7X_LANES)][:, :1]

    def bias_fn(start):
        return c0 - cum_ref[:, pl.ds(start, bq)]

    o = _flash_attend(q_ref[...], k_ref, v_ref, qi, bias_fn, m_scr, l_scr, acc_scr, bq=bq)
    o_ref[...] = o.astype(o_ref.dtype)


def fox_attention(qkv, cum, *, bq=ATTN_BLOCK):
    seq = qkv.shape[0]
    cum3 = cum.reshape(C_HEADS, 1, seq)
    return pl.pallas_call(
        functools.partial(_fox_attn_kernel, bq=bq),
        grid=(C_HEADS, seq // bq),
        in_specs=[
            pl.BlockSpec((bq, C_DIM), lambda h, i: (i, h)),
            pl.BlockSpec((seq, C_DIM), lambda h, i: (0, C_HEADS + h)),
            pl.BlockSpec((seq, C_DIM), lambda h, i: (0, 2 * C_HEADS + h)),
            pl.BlockSpec((None, 1, seq), lambda h, i: (h, 0, 0)),
        ],
        out_specs=pl.BlockSpec((bq, C_DIM), lambda h, i: (i, h)),
        out_shape=jax.ShapeDtypeStruct((seq, C_HEADS * C_DIM), BF16),
        scratch_shapes=_attn_scratch(bq),
        compiler_params=_cparams(("parallel", "arbitrary"), _attn_vmem(seq, bq, bq, 256)),
        name="fox_attention",
    )(qkv, qkv, qkv, cum3)


def _swiglu_accumulate(xn, wg, wu, wd, acc_ref):
    gate = jnp.dot(xn, wg, preferred_element_type=F32)
    up = jnp.dot(xn, wu, preferred_element_type=F32)
    act = (gate * jax.nn.sigmoid(gate) * up).astype(BF16)
    acc_ref[...] += jnp.dot(act, wd, preferred_element_type=F32)


def _ffn_kernel(h_ref, g_ref, wg_ref, wu_ref, wd_ref, o_ref, xn_ref, acc_ref):
    f = pl.program_id(1)

    @pl.when(f == 0)
    def _():
        xn_ref[...] = _rms(h_ref[...], g_ref[...], 1e-6).astype(BF16)
        acc_ref[...] = jnp.zeros(acc_ref.shape, F32)

    _swiglu_accumulate(xn_ref[...], wg_ref[...], wu_ref[...], wd_ref[...], acc_ref)

    @pl.when(f == pl.num_programs(1) - 1)
    def _():
        o_ref[...] = h_ref[...] + acc_ref[...]


def _ffn_vmem(tm, d, tf):
    return 4 * tm * d * 4 + tm * d * 2 + tm * d * 4 + 3 * 2 * d * tf * 2 + 4 * tm * tf * 4 + (6 << 20)


def dense_ffn(h, g, wg, wu, wd, *, tm=FFN_TM, tf=FFN_TF):
    m, d = h.shape
    ff = wg.shape[1]
    assert m % tm == 0 and ff % tf == 0
    return pl.pallas_call(
        _ffn_kernel,
        grid=(m // tm, ff // tf),
        in_specs=[
            pl.BlockSpec((tm, d), lambda i, f: (i, 0)),
            pl.BlockSpec((1, d), lambda i, f: (0, 0)),
            pl.BlockSpec((d, tf), lambda i, f: (0, f)),
            pl.BlockSpec((d, tf), lambda i, f: (0, f)),
            pl.BlockSpec((tf, d), lambda i, f: (f, 0)),
        ],
        out_specs=pl.BlockSpec((tm, d), lambda i, f: (i, 0)),
        out_shape=jax.ShapeDtypeStruct((m, d), F32),
        scratch_shapes=[pltpu.VMEM((tm, d), BF16), pltpu.VMEM((tm, d), F32)],
        compiler_params=_cparams(("parallel", "arbitrary"), _ffn_vmem(tm, d, tf)),
        name="dense_ffn",
    )(h, g.reshape(1, d), wg, wu, wd)


def _router_kernel(h_ref, g_ref, wr_ref, idx_ref, gate_ref, rank_ref, cnt_ref, cnt_scr):
    @pl.when(pl.program_id(0) == 0)
    def _():
        cnt_scr[...] = jnp.zeros(cnt_scr.shape, F32)

    tm = h_ref.shape[0]
    xn = _rms(h_ref[...], g_ref[...], 1e-6)
    logits = jnp.dot(xn, wr_ref[...], preferred_element_type=F32, precision=lax.Precision.HIGHEST)
    lane = lax.broadcasted_iota(jnp.int32, logits.shape, 1)
    lane_f = lane.astype(F32)
    valid = lane < N_EXPERTS
    logits = jnp.where(valid, logits, MASK_VALUE)
    e = jnp.exp(logits - jnp.max(logits, axis=1, keepdims=True))
    probs = jnp.where(valid, e / jnp.sum(e, axis=1, keepdims=True), -1.0)
    p1 = jnp.max(probs, axis=1, keepdims=True)
    i1 = jnp.min(jnp.where(probs == p1, lane_f, float(V7X_LANES)), axis=1, keepdims=True)
    rest = jnp.where(lane_f == i1, -1.0, probs)
    p2 = jnp.max(rest, axis=1, keepdims=True)
    i2 = jnp.min(jnp.where(rest == p2, lane_f, float(V7X_LANES)), axis=1, keepdims=True)
    hot1 = (lane_f == i1).astype(F32)
    hot2 = (lane_f == i2).astype(F32)
    r = lax.broadcasted_iota(jnp.int32, (tm, tm), 0)
    c = lax.broadcasted_iota(jnp.int32, (tm, tm), 1)
    strict_lower = (c < r).astype(BF16)
    before = jnp.dot(strict_lower, (hot1 + hot2).astype(BF16), preferred_element_type=F32)
    before = before + cnt_scr[...]
    rank1 = jnp.sum(before * hot1, axis=1, keepdims=True)
    rank2 = jnp.sum(before * hot2, axis=1, keepdims=True)
    cnt_scr[...] += jnp.sum(hot1 + hot2, axis=0, keepdims=True)
    cnt_ref[...] = cnt_scr[...]
    denom = p1 + p2
    idx_ref[...] = jnp.where(lane == 0, i1, jnp.where(lane == 1, i2, 0.0)).astype(jnp.int32)
    gate_ref[...] = jnp.where(lane == 0, p1 / denom, jnp.where(lane == 1, p2 / denom, 0.0))
    rank_ref[...] = jnp.where(lane == 0, rank1, jnp.where(lane == 1, rank2, 0.0)).astype(jnp.int32)


def moe_router(h, g, w_router_padded, *, tm=ROUTER_TM):
    seq, d = h.shape
    out = lambda dt: jax.ShapeDtypeStruct((seq, V7X_LANES), dt)
    row = pl.BlockSpec((tm, V7X_LANES), lambda i: (i, 0))
    return pl.pallas_call(
        _router_kernel,
        grid=(seq // tm,),
        in_specs=[
            pl.BlockSpec((tm, d), lambda i: (i, 0)),
            pl.BlockSpec((1, d), lambda i: (0, 0)),
            pl.BlockSpec((d, V7X_LANES), lambda i: (0, 0)),
        ],
        out_specs=[row, row, row, pl.BlockSpec((1, V7X_LANES), lambda i: (0, 0))],
        out_shape=[out(jnp.int32), out(F32), out(jnp.int32),
                   jax.ShapeDtypeStruct((1, V7X_LANES), F32)],
        scratch_shapes=[pltpu.VMEM((1, V7X_LANES), F32)],
        compiler_params=_cparams(("arbitrary",), 5 * tm * d * 4 + 4 * tm * tm * 4 + (6 << 20)),
        name="moe_router",
    )(h, g.reshape(1, d), w_router_padded)


def _gather_rows_kernel(rows_ref, src_hbm, o_ref, sem):
    n = o_ref.shape[0]

    def row_copy(r):
        return pltpu.make_async_copy(src_hbm.at[pl.ds(rows_ref[0, 0, r], 1)], o_ref.at[pl.ds(r, 1)], sem)

    def start(r, carry):
        row_copy(r).start()
        return carry

    def wait(r, carry):
        row_copy(r).wait()
        return carry

    lax.fori_loop(0, n, start, 0)
    lax.fori_loop(0, n, wait, 0)


def gather_rows(src, rows, *, tg=GATHER_ROWS):
    n = rows.shape[0]
    d = src.shape[1]
    assert n % tg == 0
    return pl.pallas_call(
        _gather_rows_kernel,
        grid=(n // tg,),
        in_specs=[
            pl.BlockSpec((1, 1, tg), lambda i: (i, 0, 0), memory_space=pltpu.SMEM),
            pl.BlockSpec(memory_space=pl.ANY),
        ],
        out_specs=pl.BlockSpec((tg, d), lambda i: (i, 0)),
        out_shape=jax.ShapeDtypeStruct((n, d), src.dtype),
        scratch_shapes=[pltpu.SemaphoreType.DMA(())],
        compiler_params=_cparams(("arbitrary",), 4 * tg * d * 4 + (4 << 20)),
        name="gather_rows",
    )(rows.reshape(n // tg, 1, tg), src)


def _moe_ffn_kernel(te_ref, nv_ref, x_ref, g_ref, wg_ref, wu_ref, wd_ref, o_ref, xn_ref, acc_ref):
    i, f = pl.program_id(0), pl.program_id(1)
    live = i < nv_ref[0]

    @pl.when(f == 0)
    def _():
        xn_ref[...] = _rms(x_ref[...], g_ref[...], 1e-6).astype(BF16)
        acc_ref[...] = jnp.zeros(acc_ref.shape, F32)

    @pl.when(live)
    def _():
        _swiglu_accumulate(xn_ref[...], wg_ref[...], wu_ref[...], wd_ref[...], acc_ref)

    @pl.when(f == pl.num_programs(1) - 1)
    def _():
        o_ref[...] = acc_ref[...]


def moe_ffn(x_sorted, g, wg, wu, wd, tile_expert, n_live, *, tm=MOE_TM, tf=FFN_TF):
    rows, d = x_sorted.shape
    ff = wg.shape[2]
    nf = ff // tf
    assert rows % tm == 0 and ff % tf == 0

    def f_idx(i, f, nv):
        return jnp.where(i < nv[0], f, nf - 1)

    grid_spec = pltpu.PrefetchScalarGridSpec(
        num_scalar_prefetch=2,
        grid=(rows // tm, nf),
        in_specs=[
            pl.BlockSpec((tm, d), lambda i, f, te, nv: (i, 0)),
            pl.BlockSpec((1, d), lambda i, f, te, nv: (0, 0)),
            pl.BlockSpec((None, d, tf), lambda i, f, te, nv: (te[i], 0, f_idx(i, f, nv))),
            pl.BlockSpec((None, d, tf), lambda i, f, te, nv: (te[i], 0, f_idx(i, f, nv))),
            pl.BlockSpec((None, tf, d), lambda i, f, te, nv: (te[i], f_idx(i, f, nv), 0)),
        ],
        out_specs=pl.BlockSpec((tm, d), lambda i, f, te, nv: (i, 0)),
        scratch_shapes=[pltpu.VMEM((tm, d), BF16), pltpu.VMEM((tm, d), F32)],
    )
    return pl.pallas_call(
        _moe_ffn_kernel,
        grid_spec=grid_spec,
        out_shape=jax.ShapeDtypeStruct((rows, d), F32),
        compiler_params=_cparams(("arbitrary", "arbitrary"), _ffn_vmem(tm, d, tf)),
        name="moe_ffn",
    )(tile_expert, n_live, x_sorted, g.reshape(1, d), wg, wu, wd)


def _combine_kernel(d0_ref, d1_ref, y_hbm, h_ref, gate_ref, fg_ref, o_ref, y0_buf, y1_buf, sem):
    n = h_ref.shape[0]

    def copies(r):
        return (pltpu.make_async_copy(y_hbm.at[pl.ds(d0_ref[0, 0, r], 1)], y0_buf.at[pl.ds(r, 1)], sem.at[0]),
                pltpu.make_async_copy(y_hbm.at[pl.ds(d1_ref[0, 0, r], 1)], y1_buf.at[pl.ds(r, 1)], sem.at[1]))

    def start(r, carry):
        a, b = copies(r)
        a.start()
        b.start()
        return carry

    def wait(r, carry):
        a, b = copies(r)
        a.wait()
        b.wait()
        return carry

    lax.fori_loop(0, n, start, 0)
    lax.fori_loop(0, n, wait, 0)
    gate = gate_ref[...]
    out = h_ref[...] + gate[:, 0:1] * y0_buf[...] + gate[:, 1:2] * y1_buf[...]
    o_ref[...] = _rms(out, fg_ref[...], 1e-6)


def moe_combine(y_sorted, h, gates, dest0, dest1, final_gain, *, tc=GATHER_ROWS):
    seq, d = h.shape
    smem_rows = pl.BlockSpec((1, 1, tc), lambda i: (i, 0, 0), memory_space=pltpu.SMEM)
    return pl.pallas_call(
        _combine_kernel,
        grid=(seq // tc,),
        in_specs=[
            smem_rows, smem_rows,
            pl.BlockSpec(memory_space=pl.ANY),
            pl.BlockSpec((tc, d), lambda i: (i, 0)),
            pl.BlockSpec((tc, V7X_LANES), lambda i: (i, 0)),
            pl.BlockSpec((1, d), lambda i: (0, 0)),
        ],
        out_specs=pl.BlockSpec((tc, d), lambda i: (i, 0)),
        out_shape=jax.ShapeDtypeStruct((seq, d), F32),
        scratch_shapes=[pltpu.VMEM((tc, d), F32), pltpu.VMEM((tc, d), F32),
                        pltpu.SemaphoreType.DMA((2,))],
        compiler_params=_cparams(("arbitrary",), 10 * tc * d * 4 + (4 << 20)),
        name="moe_combine",
    )(dest0.reshape(seq // tc, 1, tc), dest1.reshape(seq // tc, 1, tc), y_sorted, h, gates,
      final_gain.reshape(1, d))


def _swap_halves(w):
    half = w.shape[-1] // 2
    return jnp.concatenate([w[..., half:], w[..., :half]], axis=-1)


def _mla_weights(w_uq, w_ukv):
    lora = w_uq.shape[0]
    uq = w_uq.reshape(lora, B_HEADS, B_NOPE + B_ROPE)
    nope, pe = uq[..., :B_NOPE], uq[..., B_NOPE:]
    zpad = jnp.zeros((lora, B_HEADS, 128 - B_ROPE), w_uq.dtype)
    wq1 = jnp.concatenate([nope, pe, zpad], axis=-1).reshape(lora, B_HEADS * 256)
    wq2 = jnp.concatenate([_swap_halves(pe), zpad], axis=-1).reshape(lora, B_HEADS * 128)
    ukv = w_ukv.reshape(lora, B_HEADS, B_NOPE + B_V)
    wkn = ukv[..., :B_NOPE].reshape(lora, B_HEADS * B_NOPE)
    wv = ukv[..., B_NOPE:].reshape(lora, B_HEADS * B_V)
    return wq1.astype(BF16), wq2.astype(BF16), wkn.astype(BF16), wv.astype(BF16)


def _rope_tables(seq):
    inv = ROPE_THETA ** (-jnp.arange(0, B_ROPE, 2, dtype=F32) / B_ROPE)
    ang = jnp.arange(seq, dtype=F32)[:, None] * inv[None, :]
    cos, sin = jnp.cos(ang), jnp.sin(ang)
    zeros = jnp.zeros((seq, 128 - B_ROPE), F32)
    return (jnp.concatenate([cos, cos, zeros], axis=1), jnp.concatenate([-sin, sin, zeros], axis=1))


def _dispatch_plan(idx, rank, counts, tm):
    seq = idx.shape[0]
    n_pairs = seq * TOP_K
    n_tiles = n_pairs // tm + N_EXPERTS
    cnt = counts[0, :N_EXPERTS].astype(jnp.int32)
    tiles_per = (cnt + tm - 1) // tm
    tile_end = jnp.cumsum(tiles_per)
    row_start = (tile_end - tiles_per) * tm
    e = idx[:, :TOP_K]
    dest = row_start[e] + rank[:, :TOP_K]
    tok = jnp.broadcast_to(jnp.arange(seq, dtype=jnp.int32)[:, None], (seq, TOP_K))
    row_token = jnp.zeros((n_tiles * tm,), jnp.int32).at[dest.reshape(-1)].set(tok.reshape(-1))
    n_live = tile_end[-1]
    tile_id = jnp.arange(n_tiles, dtype=jnp.int32)
    tile_expert = jnp.sum((tile_id[:, None] >= tile_end[None, :]).astype(jnp.int32), axis=1)
    last_expert = jnp.sum((n_live - 1 >= tile_end).astype(jnp.int32))
    tile_expert = jnp.where(tile_id < n_live, tile_expert, last_expert).astype(jnp.int32)
    return dest.astype(jnp.int32), row_token, tile_expert, n_live.reshape(1).astype(jnp.int32)


def kernel(x, ev_attn_norm, ev_w_in, ev_q_norm, ev_w_uq, ev_kv_norm, ev_w_ukv, ev_lambda_q1, ev_lambda_k1, ev_lambda_q2, ev_lambda_k2, ev_subln, ev_w_out, ev_ffn_norm, ev_ffn_w_gate, ev_ffn_w_up, ev_ffn_w_down, od_attn_norm, od_w_in, od_forget_bias, od_w_out, od_ffn_norm, od_router, od_moe_w_gate, od_moe_w_up, od_moe_w_down, final_norm):
    batch, seq, d = x.shape
    assert batch == 1
    h = x.reshape(seq, d)

    lambda_init = 0.8 - 0.6 * math.exp(-0.3 * 0)
    w_in = ev_w_in[0]
    n_a = 2 * A_HEADS * 2 * A_QK + A_HEADS * A_V
    w_a = w_in[:, :n_a].astype(BF16)
    scale_a = jnp.concatenate([jnp.full((A_HEADS * 2 * A_QK,), A_QK ** -0.5, F32),
                               jnp.ones((n_a - A_HEADS * 2 * A_QK,), F32)])
    w_b = jnp.concatenate([w_in[:, n_a:], _swap_halves(w_in[:, -B_ROPE:])], axis=1).astype(BF16)
    qkv_a = rms_matmul(h, ev_attn_norm[0], w_a, scale_a, BF16)
    c_b = rms_matmul(h, ev_attn_norm[0], w_b, jnp.ones((w_b.shape[1],), F32), F32, tn=384)
    oa = diff_attention(qkv_a, ev_lambda_q1[0], ev_lambda_k1[0], ev_lambda_q2[0], ev_lambda_k2[0],
                        ev_subln[0], lambda_init)
    wq1, wq2, wkn, wv = _mla_weights(ev_w_uq[0], ev_w_ukv[0])
    cos_t, sin_t = _rope_tables(seq)
    q_b, k_b, v_b = mla_prep(c_b, ev_q_norm[0], ev_kv_norm[0], wq1, wq2, wkn, wv, cos_t, sin_t)
    ob = mla_attention(q_b, k_b, v_b)
    w_out = ev_w_out[0].astype(BF16)
    h = proj_residual([oa, ob], [w_out[:A_HEADS * A_V], w_out[A_HEADS * A_V:]], h)
    h = dense_ffn(h, ev_ffn_norm[0], ev_ffn_w_gate[0].astype(BF16), ev_ffn_w_up[0].astype(BF16),
                  ev_ffn_w_down[0].astype(BF16))

    width = C_HEADS * C_DIM
    w_qkv = od_w_in[0][:, :3 * width].astype(BF16)
    scale_c = jnp.concatenate([jnp.full((width,), C_DIM ** -0.5, F32), jnp.ones((2 * width,), F32)])
    qkv_c = rms_matmul(h, od_attn_norm[0], w_qkv, scale_c, BF16)
    wf_t = od_w_in[0][:, 3 * width:].T.astype(BF16)
    cum = forget_gate_cumsum(h, od_attn_norm[0], wf_t, od_forget_bias[0])
    oc = fox_attention(qkv_c, cum)
    h = proj_residual([oc], [od_w_out[0].astype(BF16)], h)

    w_router = jnp.zeros((d, V7X_LANES), F32).at[:, :N_EXPERTS].set(od_router[0])
    idx, gates, rank, counts = moe_router(h, od_ffn_norm[0], w_router)
    dest, row_token, tile_expert, n_live = _dispatch_plan(idx, rank, counts, MOE_TM)
    x_sorted = gather_rows(h, row_token)
    y_sorted = moe_ffn(x_sorted, od_ffn_norm[0], od_moe_w_gate[0].astype(BF16),
                       od_moe_w_up[0].astype(BF16), od_moe_w_down[0].astype(BF16), tile_expert, n_live)
    out = moe_combine(y_sorted, h, gates, dest[:, 0], dest[:, 1], final_norm)
    return out.reshape(batch, seq, d)
```

```python
import functools
import math

import jax
import jax.numpy as jnp
from jax import lax
from jax.experimental import pallas as pl
from jax.experimental.pallas import tpu as pltpu

F32 = jnp.float32
BF16 = jnp.bfloat16

V7X_LANES = 128
V7X_VMEM_REQUEST_CAP = 60000 * 1024

A_HEADS, A_QK, A_V = 8, 64, 128
B_HEADS, B_LORA, B_NOPE, B_ROPE, B_V = 8, 512, 128, 64, 128
C_HEADS, C_DIM = 16, 128
N_EXPERTS, TOP_K = 8, 2
ROPE_THETA = 10000.0
MASK_VALUE = -1e30
LOG2E = math.log2(math.e)

ATTN_BLOCK = 1024
DIFF_ATTN_BLOCK = 1024
SOFTMAX_ROWS = 128
PROJ_TM, PROJ_TN = 1024, 512
FFN_TM, FFN_TF = 512, 512
MOE_TM = 512
GATHER_ROWS = 256
PREP_TM = 512
GATE_TM = 1024
ROUTER_TM = 512


def _cparams(semantics, vmem_bytes):
    return pltpu.CompilerParams(
        dimension_semantics=semantics,
        vmem_limit_bytes=int(min(V7X_VMEM_REQUEST_CAP, vmem_bytes)))


def _rms(x, g, eps):
    return x * lax.rsqrt(jnp.mean(x * x, axis=-1, keepdims=True) + eps) * g


def _rms_matmul_kernel(x_ref, g_ref, w_ref, cs_ref, o_ref, xn_ref, *, eps):
    @pl.when(pl.program_id(1) == 0)
    def _():
        xn_ref[...] = _rms(x_ref[...], g_ref[...], eps).astype(BF16)

    acc = jnp.dot(xn_ref[...], w_ref[...], preferred_element_type=F32)
    o_ref[...] = (acc * cs_ref[...]).astype(o_ref.dtype)


def rms_matmul(x, g, w, col_scale, out_dtype, *, eps=1e-6, tm=PROJ_TM, tn=PROJ_TN):
    m, k = x.shape
    n = w.shape[1]
    assert m % tm == 0 and n % tn == 0, (m, n, tm, tn)
    vmem = 2 * tm * k * 4 + tm * k * 2 + 2 * k * tn * 2 + 4 * tm * tn * 4 + (4 << 20)
    return pl.pallas_call(
        functools.partial(_rms_matmul_kernel, eps=eps),
        grid=(m // tm, n // tn),
        in_specs=[
            pl.BlockSpec((tm, k), lambda i, j: (i, 0)),
            pl.BlockSpec((1, k), lambda i, j: (0, 0)),
            pl.BlockSpec((k, tn), lambda i, j: (0, j)),
            pl.BlockSpec((1, tn), lambda i, j: (0, j)),
        ],
        out_specs=pl.BlockSpec((tm, tn), lambda i, j: (i, j)),
        out_shape=jax.ShapeDtypeStruct((m, n), out_dtype),
        scratch_shapes=[pltpu.VMEM((tm, k), BF16)],
        compiler_params=_cparams(("parallel", "arbitrary"), vmem),
        name="rms_matmul",
    )(x, g.reshape(1, k), w, col_scale.reshape(1, n))


def _proj_residual_kernel(*refs, n_in):
    a_refs, w_refs = refs[:n_in], refs[n_in:2 * n_in]
    h_ref, o_ref = refs[2 * n_in], refs[2 * n_in + 1]
    acc = h_ref[...]
    for a_ref, w_ref in zip(a_refs, w_refs):
        acc = acc + jnp.dot(a_ref[...], w_ref[...], preferred_element_type=F32)
    o_ref[...] = acc


def proj_residual(a_list, w_list, h, *, tm=PROJ_TM, tn=PROJ_TN):
    m, n = h.shape
    n_in = len(a_list)
    assert m % tm == 0 and n % tn == 0
    ks = [a.shape[1] for a in a_list]
    vmem = sum(2 * tm * k * 2 + 2 * k * tn * 2 for k in ks) + 6 * tm * tn * 4 + (4 << 20)
    in_specs = [pl.BlockSpec((tm, k), lambda i, j: (i, 0)) for k in ks]
    in_specs += [pl.BlockSpec((k, tn), lambda i, j: (0, j)) for k in ks]
    in_specs += [pl.BlockSpec((tm, tn), lambda i, j: (i, j))]
    return pl.pallas_call(
        functools.partial(_proj_residual_kernel, n_in=n_in),
        grid=(m // tm, n // tn),
        in_specs=in_specs,
        out_specs=pl.BlockSpec((tm, tn), lambda i, j: (i, j)),
        out_shape=jax.ShapeDtypeStruct((m, n), F32),
        compiler_params=_cparams(("parallel", "arbitrary"), vmem),
        name="proj_residual",
    )(*a_list, *w_list, h)


def _flash_attend(q, k_ref, v_ref, qi, bias_fn, scratch, *, bq):
    s_bufs, p_bufs, a_bufs = scratch[0:2], scratch[2:4], scratch[4:6]
    m_scr, l_scr, acc_scr = scratch[6:9]
    rows, bk = s_bufs[0].shape
    assert bq == 2 * bk
    n_lane_chunks = bk // V7X_LANES
    m_scr[...] = jnp.full(m_scr.shape, MASK_VALUE, F32)
    l_scr[...] = jnp.zeros(l_scr.shape, F32)
    acc_scr[...] = jnp.zeros(acc_scr.shape, F32)
    a_bufs[1][...] = jnp.ones(a_bufs[1].shape, F32)
    p_bufs[1][...] = jnp.zeros(p_bufs[1].shape, BF16)

    def scores(j, slot):
        start = pl.multiple_of(j * bk, bk)
        s = lax.dot_general(q, k_ref[pl.ds(start, bk), :], (((1,), (1,)), ((), ())),
                            preferred_element_type=F32)
        s_bufs[slot][...] = s if bias_fn is None else s + bias_fn(start)

    def softmax(slot, diag_offset=None):
        s_buf, p_buf, a_buf = s_bufs[slot], p_bufs[slot], a_bufs[slot]
        for r0 in range(0, rows, SOFTMAX_ROWS):
            sl = slice(r0, r0 + SOFTMAX_ROWS)
            first_row = r0 % bq
            if diag_offset is not None and first_row + SOFTMAX_ROWS - 1 < diag_offset:
                continue
            s = s_buf[sl, :]
            if diag_offset is not None and first_row < diag_offset + bk - 1:
                row = lax.broadcasted_iota(jnp.int32, s.shape, 0) + first_row
                col = lax.broadcasted_iota(jnp.int32, s.shape, 1) + diag_offset
                s = jnp.where(col <= row, s, MASK_VALUE)
            chunks = [s[:, i * V7X_LANES:(i + 1) * V7X_LANES] for i in range(n_lane_chunks)]
            m_lane = chunks[0]
            for x in chunks[1:]:
                m_lane = jnp.maximum(m_lane, x)
            m_prev = m_scr[sl, :]
            m_next = jnp.maximum(m_prev, jnp.max(m_lane, axis=1, keepdims=True))
            alpha = jnp.exp2(m_prev - m_next)
            l_new = alpha * l_scr[sl, :]
            for i, x in enumerate(chunks):
                p = jnp.exp2(x - m_next)
                l_new = l_new + p
                p_buf[sl, i * V7X_LANES:(i + 1) * V7X_LANES] = p.astype(BF16)
            l_scr[sl, :] = l_new
            m_scr[sl, :] = m_next
            a_buf[sl, :] = alpha

    def pv(j, slot, first_live_row=0):
        start = pl.multiple_of(j * bk, bk)
        v = v_ref[pl.ds(start, bk), :]
        live = ([slice(0, rows)] if first_live_row == 0 else
                [slice(r0 + first_live_row, r0 + bq) for r0 in range(0, rows, bq)])
        for sl in live:
            acc_scr[sl, :] = acc_scr[sl, :] * a_bufs[slot][sl, :] + jnp.dot(
                p_bufs[slot][sl, :], v, preferred_element_type=F32)

    def two_steps(j, diag):
        pv(jnp.maximum(j - 1, 0), 1)
        softmax(0, 0 if diag else None)
        scores(j + 1, 1)
        pv(j, 0)
        softmax(1, bk if diag else None)
        if not diag:
            scores(j + 2, 0)

    scores(0, 0)

    def body(i, carry):
        two_steps(2 * i, False)
        return carry

    lax.fori_loop(0, qi, body, 0)
    two_steps(2 * qi, True)
    pv(2 * qi + 1, 1, first_live_row=bk)
    return acc_scr[...] / jnp.sum(l_scr[...], axis=1, keepdims=True)


def _attn_scratch(rows, bq):
    bk = bq // 2
    stat = lambda: pltpu.VMEM((rows, V7X_LANES), F32)
    s_buf = lambda: pltpu.VMEM((rows, bk), F32)
    p_buf = lambda: pltpu.VMEM((rows, bk), BF16)
    return [s_buf(), s_buf(), p_buf(), p_buf(), stat(), stat(), stat(), stat(), stat()]


def _attn_vmem(seq, rows, bq, kv_width):
    resident = 2 * seq * kv_width * 2
    buffers = 2 * rows * (bq // 2) * (4 + 2) + 5 * rows * V7X_LANES * 4
    temps = 2 * rows * (bq // 2) * 4
    return resident + buffers + temps + (6 << 20)


def _diff_attn_kernel(slope_ref, q_ref, k_ref, v_ref, lq1_ref, lk1_ref, lq2_ref, lk2_ref,
                      subln_ref, o_ref, *scratch, bq, lambda_init):
    h, qi = pl.program_id(0), pl.program_id(1)
    q = q_ref[...]
    lane = lax.broadcasted_iota(jnp.int32, q.shape, 1)
    zero = jnp.zeros_like(q)
    qs = jnp.concatenate([jnp.where(lane < A_QK, q, zero), jnp.where(lane >= A_QK, q, zero)], axis=0)
    slope = slope_ref[h] * LOG2E
    t0 = qi * bq

    def bias_fn(start):
        kpos = lax.broadcasted_iota(jnp.int32, (1, bq // 2), 1) + (start - t0)
        return slope * kpos.astype(F32)

    o = _flash_attend(qs, k_ref, v_ref, qi, bias_fn, scratch, bq=bq)
    lam = (jnp.exp(jnp.sum(lq1_ref[...] * lk1_ref[...], axis=1, keepdims=True))
           - jnp.exp(jnp.sum(lq2_ref[...] * lk2_ref[...], axis=1, keepdims=True)) + lambda_init)
    d = o[:bq] - lam * o[bq:]
    o_ref[...] = (_rms(d, subln_ref[...], 1e-5) * (1.0 - lambda_init)).astype(o_ref.dtype)


def diff_attention(qkv, lq1, lk1, lq2, lk2, subln, lambda_init, *, bq=DIFF_ATTN_BLOCK):
    seq = qkv.shape[0]
    nq = seq // bq
    slopes = jnp.exp2(-8.0 * jnp.arange(1, A_HEADS + 1, dtype=F32) / A_HEADS)
    vec = lambda: pl.BlockSpec((1, A_QK), lambda h, i, s: (0, 0))
    grid_spec = pltpu.PrefetchScalarGridSpec(
        num_scalar_prefetch=1,
        grid=(A_HEADS, nq),
        in_specs=[
            pl.BlockSpec((bq, 128), lambda h, i, s: (i, h)),
            pl.BlockSpec((seq, 128), lambda h, i, s: (0, A_HEADS + h)),
            pl.BlockSpec((seq, 128), lambda h, i, s: (0, 2 * A_HEADS + h)),
            vec(), vec(), vec(), vec(),
            pl.BlockSpec((1, A_V), lambda h, i, s: (0, 0)),
        ],
        out_specs=pl.BlockSpec((bq, A_V), lambda h, i, s: (i, h)),
        scratch_shapes=_attn_scratch(2 * bq, bq),
    )
    return pl.pallas_call(
        functools.partial(_diff_attn_kernel, bq=bq, lambda_init=lambda_init),
        grid_spec=grid_spec,
        out_shape=jax.ShapeDtypeStruct((seq, A_HEADS * A_V), BF16),
        compiler_params=_cparams(("parallel", "arbitrary"), _attn_vmem(seq, 2 * bq, bq, 256)),
        name="diff_attention",
    )(slopes, qkv, qkv, qkv, lq1.reshape(1, -1), lk1.reshape(1, -1), lq2.reshape(1, -1),
      lk2.reshape(1, -1), subln.reshape(1, -1))


def _mla_prep_kernel(c_ref, qn_ref, kvn_ref, wq1_ref, wq2_ref, wkn_ref, wv_ref, cos_ref, sin_ref,
                     q_ref, k_ref, v_ref, *, scale):
    c = c_ref[...]
    cqn = _rms(c[:, :B_LORA], qn_ref[...], 1e-6).astype(BF16)
    ckvn = _rms(c[:, B_LORA:2 * B_LORA], kvn_ref[...], 1e-6).astype(BF16)
    kp = c[:, 2 * B_LORA:]
    cos, sin = cos_ref[...], sin_ref[...]
    qa = jnp.dot(cqn, wq1_ref[...], preferred_element_type=F32)
    qb = jnp.dot(cqn, wq2_ref[...], preferred_element_type=F32)
    kn = jnp.dot(ckvn, wkn_ref[...], preferred_element_type=F32)
    v_ref[...] = jnp.dot(ckvn, wv_ref[...], preferred_element_type=F32).astype(v_ref.dtype)
    kr = (kp * cos + pltpu.roll(kp, 64, axis=1) * sin).astype(k_ref.dtype)
    for h in range(B_HEADS):
        lo, hi = 256 * h, 256 * h + 128
        q_ref[:, lo:hi] = (qa[:, lo:hi] * scale).astype(q_ref.dtype)
        pe = qa[:, hi:hi + 128] * cos + qb[:, 128 * h:128 * (h + 1)] * sin
        q_ref[:, hi:hi + 128] = (pe * scale).astype(q_ref.dtype)
        k_ref[:, lo:hi] = kn[:, 128 * h:128 * (h + 1)].astype(k_ref.dtype)
        k_ref[:, hi:hi + 128] = kr


def mla_prep(c, q_norm, kv_norm, wq1, wq2, wkn, wv, cos_t, sin_t, *, tm=PREP_TM):
    seq = c.shape[0]
    scale = (B_NOPE + B_ROPE) ** -0.5 * LOG2E
    full = lambda a: pl.BlockSpec(a.shape, lambda i: (0, 0))
    row = lambda w: pl.BlockSpec((tm, w), lambda i: (i, 0))
    qn, kvn = q_norm.reshape(1, -1), kv_norm.reshape(1, -1)
    weights = 2 * 2 * (wq1.size + wq2.size + wkn.size + wv.size)
    vmem = weights + 2 * tm * (1152 * 4 + 256 * 4 + 5120 * 2) + tm * 5120 * 4 * 2 + (6 << 20)
    return pl.pallas_call(
        functools.partial(_mla_prep_kernel, scale=scale),
        grid=(seq // tm,),
        in_specs=[row(c.shape[1]), full(qn), full(kvn), full(wq1), full(wq2), full(wkn), full(wv),
                  row(128), row(128)],
        out_specs=[row(2048), row(2048), row(1024)],
        out_shape=[jax.ShapeDtypeStruct((seq, 2048), BF16), jax.ShapeDtypeStruct((seq, 2048), BF16),
                   jax.ShapeDtypeStruct((seq, 1024), BF16)],
        compiler_params=_cparams(("parallel",), vmem),
        name="mla_prep",
    )(c, qn, kvn, wq1, wq2, wkn, wv, cos_t, sin_t)


def _mla_attn_kernel(q_ref, k_ref, v_ref, o_ref, *scratch, bq):
    o = _flash_attend(q_ref[...], k_ref, v_ref, pl.program_id(1), None, scratch, bq=bq)
    o_ref[...] = o.astype(o_ref.dtype)


def mla_attention(q, k, v, *, bq=ATTN_BLOCK):
    seq = q.shape[0]
    return pl.pallas_call(
        functools.partial(_mla_attn_kernel, bq=bq),
        grid=(B_HEADS, seq // bq),
        in_specs=[
            pl.BlockSpec((bq, 256), lambda h, i: (i, h)),
            pl.BlockSpec((seq, 256), lambda h, i: (0, h)),
            pl.BlockSpec((seq, 128), lambda h, i: (0, h)),
        ],
        out_specs=pl.BlockSpec((bq, B_V), lambda h, i: (i, h)),
        out_shape=jax.ShapeDtypeStruct((seq, B_HEADS * B_V), BF16),
        scratch_shapes=_attn_scratch(bq, bq),
        compiler_params=_cparams(("parallel", "arbitrary"), _attn_vmem(seq, bq, bq, 384)),
        name="mla_attention",
    )(q, k, v)


def _forget_gate_kernel(h_ref, g_ref, wf_ref, b_ref, cum_ref, carry_scr):
    @pl.when(pl.program_id(0) == 0)
    def _():
        carry_scr[...] = jnp.zeros(carry_scr.shape, F32)

    xn = _rms(h_ref[...], g_ref[...], 1e-6).astype(BF16)
    f = lax.dot_general(wf_ref[...], xn, (((1,), (1,)), ((), ())), preferred_element_type=F32)
    z = f + b_ref[...]
    log_f = jnp.minimum(z, 0.0) - jnp.log(1.0 + jnp.exp(-jnp.abs(z)))
    r = lax.broadcasted_iota(jnp.int32, (V7X_LANES, V7X_LANES), 0)
    c = lax.broadcasted_iota(jnp.int32, (V7X_LANES, V7X_LANES), 1)
    tri = (r <= c).astype(F32)
    carry = carry_scr[...]
    for j in range(log_f.shape[1] // V7X_LANES):
        sl = slice(j * V7X_LANES, (j + 1) * V7X_LANES)
        loc = jnp.dot(log_f[:, sl], tri, preferred_element_type=F32,
                      precision=lax.Precision.HIGHEST) + carry
        cum_ref[:, sl] = loc
        carry = jnp.broadcast_to(loc[:, V7X_LANES - 1:], carry.shape)
    carry_scr[...] = carry


def forget_gate_cumsum(h, g, wf_t, bias, *, tm=GATE_TM):
    seq, d = h.shape
    nh = wf_t.shape[0]
    return pl.pallas_call(
        _forget_gate_kernel,
        grid=(seq // tm,),
        in_specs=[
            pl.BlockSpec((tm, d), lambda i: (i, 0)),
            pl.BlockSpec((1, d), lambda i: (0, 0)),
            pl.BlockSpec((nh, d), lambda i: (0, 0)),
            pl.BlockSpec((nh, 1), lambda i: (0, 0)),
        ],
        out_specs=pl.BlockSpec((nh, tm), lambda i: (0, i)),
        out_shape=jax.ShapeDtypeStruct((nh, seq), F32),
        scratch_shapes=[pltpu.VMEM((nh, V7X_LANES), F32)],
        compiler_params=_cparams(("arbitrary",), 2 * tm * d * 4 + 3 * tm * d * 4 + (4 << 20)),
        name="forget_gate_cumsum",
    )(h, g.reshape(1, d), wf_t, bias.reshape(nh, 1))


def _fox_attn_kernel(q_ref, k_ref, v_ref, cum_ref, o_ref, *scratch, bq):
    qi = pl.program_id(1)
    t0 = pl.multiple_of(qi * bq, bq)
    c0 = cum_ref[:, pl.ds(t0, V7X_LANES)][:, :1]

    def bias_fn(start):
        return (c0 - cum_ref[:, pl.ds(start, bq // 2)]) * LOG2E

    o = _flash_attend(q_ref[...], k_ref, v_ref, qi, bias_fn, scratch, bq=bq)
    o_ref[...] = o.astype(o_ref.dtype)


def fox_attention(qkv, cum, *, bq=ATTN_BLOCK):
    seq = qkv.shape[0]
    cum3 = cum.reshape(C_HEADS, 1, seq)
    return pl.pallas_call(
        functools.partial(_fox_attn_kernel, bq=bq),
        grid=(C_HEADS, seq // bq),
        in_specs=[
            pl.BlockSpec((bq, C_DIM), lambda h, i: (i, h)),
            pl.BlockSpec((seq, C_DIM), lambda h, i: (0, C_HEADS + h)),
            pl.BlockSpec((seq, C_DIM), lambda h, i: (0, 2 * C_HEADS + h)),
            pl.BlockSpec((None, 1, seq), lambda h, i: (h, 0, 0)),
        ],
        out_specs=pl.BlockSpec((bq, C_DIM), lambda h, i: (i, h)),
        out_shape=jax.ShapeDtypeStruct((seq, C_HEADS * C_DIM), BF16),
        scratch_shapes=_attn_scratch(bq, bq),
        compiler_params=_cparams(("parallel", "arbitrary"), _attn_vmem(seq, bq, bq, 256)),
        name="fox_attention",
    )(qkv, qkv, qkv, cum3)


def _swiglu_accumulate(xn, wg, wu, wd, acc_ref):
    gate = jnp.dot(xn, wg, preferred_element_type=F32)
    up = jnp.dot(xn, wu, preferred_element_type=F32)
    act = (gate * jax.nn.sigmoid(gate) * up).astype(BF16)
    acc_ref[...] += jnp.dot(act, wd, preferred_element_type=F32)


def _ffn_kernel(h_ref, g_ref, wg_ref, wu_ref, wd_ref, o_ref, xn_ref, acc_ref):
    f = pl.program_id(1)

    @pl.when(f == 0)
    def _():
        xn_ref[...] = _rms(h_ref[...], g_ref[...], 1e-6).astype(BF16)
        acc_ref[...] = jnp.zeros(acc_ref.shape, F32)

    _swiglu_accumulate(xn_ref[...], wg_ref[...], wu_ref[...], wd_ref[...], acc_ref)

    @pl.when(f == pl.num_programs(1) - 1)
    def _():
        o_ref[...] = h_ref[...] + acc_ref[...]


def _ffn_vmem(tm, d, tf):
    return 4 * tm * d * 4 + tm * d * 2 + tm * d * 4 + 3 * 2 * d * tf * 2 + 4 * tm * tf * 4 + (6 << 20)


def dense_ffn(h, g, wg, wu, wd, *, tm=FFN_TM, tf=FFN_TF):
    m, d = h.shape
    ff = wg.shape[1]
    assert m % tm == 0 and ff % tf == 0
    return pl.pallas_call(
        _ffn_kernel,
        grid=(m // tm, ff // tf),
        in_specs=[
            pl.BlockSpec((tm, d), lambda i, f: (i, 0)),
            pl.BlockSpec((1, d), lambda i, f: (0, 0)),
            pl.BlockSpec((d, tf), lambda i, f: (0, f)),
            pl.BlockSpec((d, tf), lambda i, f: (0, f)),
            pl.BlockSpec((tf, d), lambda i, f: (f, 0)),
        ],
        out_specs=pl.BlockSpec((tm, d), lambda i, f: (i, 0)),
        out_shape=jax.ShapeDtypeStruct((m, d), F32),
        scratch_shapes=[pltpu.VMEM((tm, d), BF16), pltpu.VMEM((tm, d), F32)],
        compiler_params=_cparams(("parallel", "arbitrary"), _ffn_vmem(tm, d, tf)),
        name="dense_ffn",
    )(h, g.reshape(1, d), wg, wu, wd)


def _router_kernel(h_ref, g_ref, wr_ref, idx_ref, gate_ref, rank_ref, cnt_ref, cnt_scr):
    @pl.when(pl.program_id(0) == 0)
    def _():
        cnt_scr[...] = jnp.zeros(cnt_scr.shape, F32)

    tm = h_ref.shape[0]
    xn = _rms(h_ref[...], g_ref[...], 1e-6)
    logits = jnp.dot(xn, wr_ref[...], preferred_element_type=F32, precision=lax.Precision.HIGHEST)
    lane = lax.broadcasted_iota(jnp.int32, logits.shape, 1)
    lane_f = lane.astype(F32)
    valid = lane < N_EXPERTS
    logits = jnp.where(valid, logits, MASK_VALUE)
    e = jnp.exp(logits - jnp.max(logits, axis=1, keepdims=True))
    probs = jnp.where(valid, e / jnp.sum(e, axis=1, keepdims=True), -1.0)
    p1 = jnp.max(probs, axis=1, keepdims=True)
    i1 = jnp.min(jnp.where(probs == p1, lane_f, float(V7X_LANES)), axis=1, keepdims=True)
    rest = jnp.where(lane_f == i1, -1.0, probs)
    p2 = jnp.max(rest, axis=1, keepdims=True)
    i2 = jnp.min(jnp.where(rest == p2, lane_f, float(V7X_LANES)), axis=1, keepdims=True)
    hot1 = (lane_f == i1).astype(F32)
    hot2 = (lane_f == i2).astype(F32)
    r = lax.broadcasted_iota(jnp.int32, (tm, tm), 0)
    c = lax.broadcasted_iota(jnp.int32, (tm, tm), 1)
    strict_lower = (c < r).astype(BF16)
    before = jnp.dot(strict_lower, (hot1 + hot2).astype(BF16), preferred_element_type=F32)
    before = before + cnt_scr[...]
    rank1 = jnp.sum(before * hot1, axis=1, keepdims=True)
    rank2 = jnp.sum(before * hot2, axis=1, keepdims=True)
    cnt_scr[...] += jnp.sum(hot1 + hot2, axis=0, keepdims=True)
    cnt_ref[...] = cnt_scr[...]
    denom = p1 + p2
    idx_ref[...] = jnp.where(lane == 0, i1, jnp.where(lane == 1, i2, 0.0)).astype(jnp.int32)
    gate_ref[...] = jnp.where(lane == 0, p1 / denom, jnp.where(lane == 1, p2 / denom, 0.0))
    rank_ref[...] = jnp.where(lane == 0, rank1, jnp.where(lane == 1, rank2, 0.0)).astype(jnp.int32)


def moe_router(h, g, w_router_padded, *, tm=ROUTER_TM):
    seq, d = h.shape
    out = lambda dt: jax.ShapeDtypeStruct((seq, V7X_LANES), dt)
    row = pl.BlockSpec((tm, V7X_LANES), lambda i: (i, 0))
    return pl.pallas_call(
        _router_kernel,
        grid=(seq // tm,),
        in_specs=[
            pl.BlockSpec((tm, d), lambda i: (i, 0)),
            pl.BlockSpec((1, d), lambda i: (0, 0)),
            pl.BlockSpec((d, V7X_LANES), lambda i: (0, 0)),
        ],
        out_specs=[row, row, row, pl.BlockSpec((1, V7X_LANES), lambda i: (0, 0))],
        out_shape=[out(jnp.int32), out(F32), out(jnp.int32),
                   jax.ShapeDtypeStruct((1, V7X_LANES), F32)],
        scratch_shapes=[pltpu.VMEM((1, V7X_LANES), F32)],
        compiler_params=_cparams(("arbitrary",), 5 * tm * d * 4 + 4 * tm * tm * 4 + (6 << 20)),
        name="moe_router",
    )(h, g.reshape(1, d), w_router_padded)


def _gather_rows_kernel(rows_ref, src_hbm, o_ref, sem):
    n = o_ref.shape[0]

    def row_copy(r):
        return pltpu.make_async_copy(src_hbm.at[pl.ds(rows_ref[0, 0, r], 1)], o_ref.at[pl.ds(r, 1)], sem)

    def start(r, carry):
        row_copy(r).start()
        return carry

    def wait(r, carry):
        row_copy(r).wait()
        return carry

    lax.fori_loop(0, n, start, 0)
    lax.fori_loop(0, n, wait, 0)


def gather_rows(src, rows, *, tg=GATHER_ROWS):
    n = rows.shape[0]
    d = src.shape[1]
    assert n % tg == 0
    return pl.pallas_call(
        _gather_rows_kernel,
        grid=(n // tg,),
        in_specs=[
            pl.BlockSpec((1, 1, tg), lambda i: (i, 0, 0), memory_space=pltpu.SMEM),
            pl.BlockSpec(memory_space=pl.ANY),
        ],
        out_specs=pl.BlockSpec((tg, d), lambda i: (i, 0)),
        out_shape=jax.ShapeDtypeStruct((n, d), src.dtype),
        scratch_shapes=[pltpu.SemaphoreType.DMA(())],
        compiler_params=_cparams(("arbitrary",), 4 * tg * d * 4 + (4 << 20)),
        name="gather_rows",
    )(rows.reshape(n // tg, 1, tg), src)


def _moe_ffn_kernel(te_ref, nv_ref, x_ref, g_ref, wg_ref, wu_ref, wd_ref, o_ref, xn_ref, acc_ref):
    i, f = pl.program_id(0), pl.program_id(1)
    live = i < nv_ref[0]

    @pl.when(f == 0)
    def _():
        xn_ref[...] = _rms(x_ref[...], g_ref[...], 1e-6).astype(BF16)
        acc_ref[...] = jnp.zeros(acc_ref.shape, F32)

    @pl.when(live)
    def _():
        _swiglu_accumulate(xn_ref[...], wg_ref[...], wu_ref[...], wd_ref[...], acc_ref)

    @pl.when(f == pl.num_programs(1) - 1)
    def _():
        o_ref[...] = acc_ref[...]


def moe_ffn(x_sorted, g, wg, wu, wd, tile_expert, n_live, *, tm=MOE_TM, tf=FFN_TF):
    rows, d = x_sorted.shape
    ff = wg.shape[2]
    nf = ff // tf
    assert rows % tm == 0 and ff % tf == 0

    def f_idx(i, f, nv):
        return jnp.where(i < nv[0], f, nf - 1)

    grid_spec = pltpu.PrefetchScalarGridSpec(
        num_scalar_prefetch=2,
        grid=(rows // tm, nf),
        in_specs=[
            pl.BlockSpec((tm, d), lambda i, f, te, nv: (i, 0)),
            pl.BlockSpec((1, d), lambda i, f, te, nv: (0, 0)),
            pl.BlockSpec((None, d, tf), lambda i, f, te, nv: (te[i], 0, f_idx(i, f, nv))),
            pl.BlockSpec((None, d, tf), lambda i, f, te, nv: (te[i], 0, f_idx(i, f, nv))),
            pl.BlockSpec((None, tf, d), lambda i, f, te, nv: (te[i], f_idx(i, f, nv), 0)),
        ],
        out_specs=pl.BlockSpec((tm, d), lambda i, f, te, nv: (i, 0)),
        scratch_shapes=[pltpu.VMEM((tm, d), BF16), pltpu.VMEM((tm, d), F32)],
    )
    return pl.pallas_call(
        _moe_ffn_kernel,
        grid_spec=grid_spec,
        out_shape=jax.ShapeDtypeStruct((rows, d), F32),
        compiler_params=_cparams(("arbitrary", "arbitrary"), _ffn_vmem(tm, d, tf)),
        name="moe_ffn",
    )(tile_expert, n_live, x_sorted, g.reshape(1, d), wg, wu, wd)


def _combine_kernel(d0_ref, d1_ref, y_hbm, h_ref, gate_ref, fg_ref, o_ref, y0_buf, y1_buf, sem):
    n = h_ref.shape[0]

    def copies(r):
        return (pltpu.make_async_copy(y_hbm.at[pl.ds(d0_ref[0, 0, r], 1)], y0_buf.at[pl.ds(r, 1)], sem.at[0]),
                pltpu.make_async_copy(y_hbm.at[pl.ds(d1_ref[0, 0, r], 1)], y1_buf.at[pl.ds(r, 1)], sem.at[1]))

    def start(r, carry):
        a, b = copies(r)
        a.start()
        b.start()
        return carry

    def wait(r, carry):
        a, b = copies(r)
        a.wait()
        b.wait()
        return carry

    lax.fori_loop(0, n, start, 0)
    lax.fori_loop(0, n, wait, 0)
    gate = gate_ref[...]
    out = h_ref[...] + gate[:, 0:1] * y0_buf[...] + gate[:, 1:2] * y1_buf[...]
    o_ref[...] = _rms(out, fg_ref[...], 1e-6)


def moe_combine(y_sorted, h, gates, dest0, dest1, final_gain, *, tc=GATHER_ROWS):
    seq, d = h.shape
    smem_rows = pl.BlockSpec((1, 1, tc), lambda i: (i, 0, 0), memory_space=pltpu.SMEM)
    return pl.pallas_call(
        _combine_kernel,
        grid=(seq // tc,),
        in_specs=[
            smem_rows, smem_rows,
            pl.BlockSpec(memory_space=pl.ANY),
            pl.BlockSpec((tc, d), lambda i: (i, 0)),
            pl.BlockSpec((tc, V7X_LANES), lambda i: (i, 0)),
            pl.BlockSpec((1, d), lambda i: (0, 0)),
        ],
        out_specs=pl.BlockSpec((tc, d), lambda i: (i, 0)),
        out_shape=jax.ShapeDtypeStruct((seq, d), F32),
        scratch_shapes=[pltpu.VMEM((tc, d), F32), pltpu.VMEM((tc, d), F32),
                        pltpu.SemaphoreType.DMA((2,))],
        compiler_params=_cparams(("arbitrary",), 10 * tc * d * 4 + (4 << 20)),
        name="moe_combine",
    )(dest0.reshape(seq // tc, 1, tc), dest1.reshape(seq // tc, 1, tc), y_sorted, h, gates,
      final_gain.reshape(1, d))


def _swap_halves(w):
    half = w.shape[-1] // 2
    return jnp.concatenate([w[..., half:], w[..., :half]], axis=-1)


def _mla_weights(w_uq, w_ukv):
    lora = w_uq.shape[0]
    uq = w_uq.reshape(lora, B_HEADS, B_NOPE + B_ROPE)
    nope, pe = uq[..., :B_NOPE], uq[..., B_NOPE:]
    zpad = jnp.zeros((lora, B_HEADS, 128 - B_ROPE), w_uq.dtype)
    wq1 = jnp.concatenate([nope, pe, zpad], axis=-1).reshape(lora, B_HEADS * 256)
    wq2 = jnp.concatenate([_swap_halves(pe), zpad], axis=-1).reshape(lora, B_HEADS * 128)
    ukv = w_ukv.reshape(lora, B_HEADS, B_NOPE + B_V)
    wkn = ukv[..., :B_NOPE].reshape(lora, B_HEADS * B_NOPE)
    wv = ukv[..., B_NOPE:].reshape(lora, B_HEADS * B_V)
    return wq1.astype(BF16), wq2.astype(BF16), wkn.astype(BF16), wv.astype(BF16)


def _rope_tables(seq):
    inv = ROPE_THETA ** (-jnp.arange(0, B_ROPE, 2, dtype=F32) / B_ROPE)
    ang = jnp.arange(seq, dtype=F32)[:, None] * inv[None, :]
    cos, sin = jnp.cos(ang), jnp.sin(ang)
    zeros = jnp.zeros((seq, 128 - B_ROPE), F32)
    return (jnp.concatenate([cos, cos, zeros], axis=1), jnp.concatenate([-sin, sin, zeros], axis=1))


def _dispatch_plan(idx, rank, counts, tm):
    seq = idx.shape[0]
    n_pairs = seq * TOP_K
    n_tiles = n_pairs // tm + N_EXPERTS
    cnt = counts[0, :N_EXPERTS].astype(jnp.int32)
    tiles_per = (cnt + tm - 1) // tm
    tile_end = jnp.cumsum(tiles_per)
    row_start = (tile_end - tiles_per) * tm
    e = idx[:, :TOP_K]
    dest = row_start[e] + rank[:, :TOP_K]
    tok = jnp.broadcast_to(jnp.arange(seq, dtype=jnp.int32)[:, None], (seq, TOP_K))
    row_token = jnp.zeros((n_tiles * tm,), jnp.int32).at[dest.reshape(-1)].set(tok.reshape(-1))
    n_live = tile_end[-1]
    tile_id = jnp.arange(n_tiles, dtype=jnp.int32)
    tile_expert = jnp.sum((tile_id[:, None] >= tile_end[None, :]).astype(jnp.int32), axis=1)
    last_expert = jnp.sum((n_live - 1 >= tile_end).astype(jnp.int32))
    tile_expert = jnp.where(tile_id < n_live, tile_expert, last_expert).astype(jnp.int32)
    return dest.astype(jnp.int32), row_token, tile_expert, n_live.reshape(1).astype(jnp.int32)


def kernel(x, ev_attn_norm, ev_w_in, ev_q_norm, ev_w_uq, ev_kv_norm, ev_w_ukv, ev_lambda_q1, ev_lambda_k1, ev_lambda_q2, ev_lambda_k2, ev_subln, ev_w_out, ev_ffn_norm, ev_ffn_w_gate, ev_ffn_w_up, ev_ffn_w_down, od_attn_norm, od_w_in, od_forget_bias, od_w_out, od_ffn_norm, od_router, od_moe_w_gate, od_moe_w_up, od_moe_w_down, final_norm):
    batch, seq, d = x.shape
    assert batch == 1
    h = x.reshape(seq, d)

    lambda_init = 0.8 - 0.6 * math.exp(-0.3 * 0)
    w_in = ev_w_in[0]
    n_a = 2 * A_HEADS * 2 * A_QK + A_HEADS * A_V
    w_a = w_in[:, :n_a].astype(BF16)
    scale_a = jnp.concatenate([jnp.full((A_HEADS * 2 * A_QK,), A_QK ** -0.5 * LOG2E, F32),
                               jnp.ones((n_a - A_HEADS * 2 * A_QK,), F32)])
    w_b = jnp.concatenate([w_in[:, n_a:], _swap_halves(w_in[:, -B_ROPE:])], axis=1).astype(BF16)
    qkv_a = rms_matmul(h, ev_attn_norm[0], w_a, scale_a, BF16)
    c_b = rms_matmul(h, ev_attn_norm[0], w_b, jnp.ones((w_b.shape[1],), F32), F32, tn=384)
    oa = diff_attention(qkv_a, ev_lambda_q1[0], ev_lambda_k1[0], ev_lambda_q2[0], ev_lambda_k2[0],
                        ev_subln[0], lambda_init)
    wq1, wq2, wkn, wv = _mla_weights(ev_w_uq[0], ev_w_ukv[0])
    cos_t, sin_t = _rope_tables(seq)
    q_b, k_b, v_b = mla_prep(c_b, ev_q_norm[0], ev_kv_norm[0], wq1, wq2, wkn, wv, cos_t, sin_t)
    ob = mla_attention(q_b, k_b, v_b)
    w_out = ev_w_out[0].astype(BF16)
    h = proj_residual([oa, ob], [w_out[:A_HEADS * A_V], w_out[A_HEADS * A_V:]], h)
    h = dense_ffn(h, ev_ffn_norm[0], ev_ffn_w_gate[0].astype(BF16), ev_ffn_w_up[0].astype(BF16),
                  ev_ffn_w_down[0].astype(BF16))

    width = C_HEADS * C_DIM
    w_qkv = od_w_in[0][:, :3 * width].astype(BF16)
    scale_c = jnp.concatenate([jnp.full((width,), C_DIM ** -0.5 * LOG2E, F32), jnp.ones((2 * width,), F32)])
    qkv_c = rms_matmul(h, od_attn_norm[0], w_qkv, scale_c, BF16)
    wf_t = od_w_in[0][:, 3 * width:].T.astype(BF16)
    cum = forget_gate_cumsum(h, od_attn_norm[0], wf_t, od_forget_bias[0])
    oc = fox_attention(qkv_c, cum)
    h = proj_residual([oc], [od_w_out[0].astype(BF16)], h)

    w_router = jnp.zeros((d, V7X_LANES), F32).at[:, :N_EXPERTS].set(od_router[0])
    idx, gates, rank, counts = moe_router(h, od_ffn_norm[0], w_router)
    dest, row_token, tile_expert, n_live = _dispatch_plan(idx, rank, counts, MOE_TM)
    x_sorted = gather_rows(h, row_token)
    y_sorted = moe_ffn(x_sorted, od_ffn_norm[0], od_moe_w_gate[0].astype(BF16),
                       od_moe_w_up[0].astype(BF16), od_moe_w_down[0].astype(BF16), tile_expert, n_live)
    out = moe_combine(y_sorted, h, gates, dest[:, 0], dest[:, 1], final_norm)
    return out.reshape(batch, seq, d)
```

```python
import functools
import math

import jax
import jax.numpy as jnp
from jax import lax
from jax.experimental import pallas as pl
from jax.experimental.pallas import tpu as pltpu

F32 = jnp.float32
BF16 = jnp.bfloat16

V7X_LANES = 128
V7X_VMEM_REQUEST_CAP = 60000 * 1024

A_HEADS, A_QK, A_V = 8, 64, 128
B_HEADS, B_LORA, B_NOPE, B_ROPE, B_V = 8, 512, 128, 64, 128
C_HEADS, C_DIM = 16, 128
N_EXPERTS, TOP_K = 8, 2
ROPE_THETA = 10000.0
MASK_VALUE = -1e30
LOG2E = math.log2(math.e)

ATTN_BLOCK = 1024
DIFF_ATTN_BLOCK = 1024
SOFTMAX_ROWS = 128
PROJ_TM, PROJ_TN = 1024, 512
FFN_TM, FFN_TF = 512, 512
MOE_TM = 512
GATHER_ROWS = 256
PREP_TM = 512
GATE_TM = 1024
ROUTER_TM = 512


def _cparams(semantics, vmem_bytes):
    return pltpu.CompilerParams(
        dimension_semantics=semantics,
        vmem_limit_bytes=int(min(V7X_VMEM_REQUEST_CAP, vmem_bytes)))


def _rms(x, g, eps):
    return x * lax.rsqrt(jnp.mean(x * x, axis=-1, keepdims=True) + eps) * g


def _rms_matmul_kernel(x_ref, g_ref, w_ref, cs_ref, o_ref, xn_ref, *, eps):
    @pl.when(pl.program_id(1) == 0)
    def _():
        xn_ref[...] = _rms(x_ref[...], g_ref[...], eps).astype(BF16)

    acc = jnp.dot(xn_ref[...], w_ref[...], preferred_element_type=F32)
    o_ref[...] = (acc * cs_ref[...]).astype(o_ref.dtype)


def rms_matmul(x, g, w, col_scale, out_dtype, *, eps=1e-6, tm=PROJ_TM, tn=PROJ_TN):
    m, k = x.shape
    n = w.shape[1]
    assert m % tm == 0 and n % tn == 0, (m, n, tm, tn)
    vmem = 2 * tm * k * 4 + tm * k * 2 + 2 * k * tn * 2 + 4 * tm * tn * 4 + (4 << 20)
    return pl.pallas_call(
        functools.partial(_rms_matmul_kernel, eps=eps),
        grid=(m // tm, n // tn),
        in_specs=[
            pl.BlockSpec((tm, k), lambda i, j: (i, 0)),
            pl.BlockSpec((1, k), lambda i, j: (0, 0)),
            pl.BlockSpec((k, tn), lambda i, j: (0, j)),
            pl.BlockSpec((1, tn), lambda i, j: (0, j)),
        ],
        out_specs=pl.BlockSpec((tm, tn), lambda i, j: (i, j)),
        out_shape=jax.ShapeDtypeStruct((m, n), out_dtype),
        scratch_shapes=[pltpu.VMEM((tm, k), BF16)],
        compiler_params=_cparams(("parallel", "arbitrary"), vmem),
        name="rms_matmul",
    )(x, g.reshape(1, k), w, col_scale.reshape(1, n))


def _proj_residual_kernel(*refs, n_in):
    a_refs, w_refs = refs[:n_in], refs[n_in:2 * n_in]
    h_ref, o_ref = refs[2 * n_in], refs[2 * n_in + 1]
    acc = h_ref[...]
    for a_ref, w_ref in zip(a_refs, w_refs):
        acc = acc + jnp.dot(a_ref[...], w_ref[...], preferred_element_type=F32)
    o_ref[...] = acc


def proj_residual(a_list, w_list, h, *, tm=PROJ_TM, tn=PROJ_TN):
    m, n = h.shape
    n_in = len(a_list)
    assert m % tm == 0 and n % tn == 0
    ks = [a.shape[1] for a in a_list]
    vmem = sum(2 * tm * k * 2 + 2 * k * tn * 2 for k in ks) + 6 * tm * tn * 4 + (4 << 20)
    in_specs = [pl.BlockSpec((tm, k), lambda i, j: (i, 0)) for k in ks]
    in_specs += [pl.BlockSpec((k, tn), lambda i, j: (0, j)) for k in ks]
    in_specs += [pl.BlockSpec((tm, tn), lambda i, j: (i, j))]
    return pl.pallas_call(
        functools.partial(_proj_residual_kernel, n_in=n_in),
        grid=(m // tm, n // tn),
        in_specs=in_specs,
        out_specs=pl.BlockSpec((tm, tn), lambda i, j: (i, j)),
        out_shape=jax.ShapeDtypeStruct((m, n), F32),
        compiler_params=_cparams(("parallel", "arbitrary"), vmem),
        name="proj_residual",
    )(*a_list, *w_list, h)


def _flash_attend(q, k_ref, v_ref, qi, bias_fn, scratch, *, bq):
    s_bufs, p_bufs, a_bufs = scratch[0:2], scratch[2:4], scratch[4:6]
    m_scr, l_scr, acc_scr = scratch[6:9]
    rows, bk = s_bufs[0].shape
    assert bq == 2 * bk
    n_lane_chunks = bk // V7X_LANES
    m_scr[...] = jnp.full(m_scr.shape, MASK_VALUE, F32)
    l_scr[...] = jnp.zeros(l_scr.shape, F32)
    acc_scr[...] = jnp.zeros(acc_scr.shape, F32)
    a_bufs[1][...] = jnp.ones(a_bufs[1].shape, F32)
    p_bufs[1][...] = jnp.zeros(p_bufs[1].shape, BF16)

    def scores(j, slot):
        start = pl.multiple_of(j * bk, bk)
        s = lax.dot_general(q, k_ref[pl.ds(start, bk), :], (((1,), (1,)), ((), ())),
                            preferred_element_type=F32)
        s_bufs[slot][...] = s if bias_fn is None else s + bias_fn(start)

    def softmax(slot, diag_offset=None):
        s_buf, p_buf, a_buf = s_bufs[slot], p_bufs[slot], a_bufs[slot]
        for r0 in range(0, rows, SOFTMAX_ROWS):
            sl = slice(r0, r0 + SOFTMAX_ROWS)
            first_row = r0 % bq
            if diag_offset is not None and first_row + SOFTMAX_ROWS - 1 < diag_offset:
                continue
            s = s_buf[sl, :]
            if diag_offset is not None and first_row < diag_offset + bk - 1:
                row = lax.broadcasted_iota(jnp.int32, s.shape, 0) + first_row
                col = lax.broadcasted_iota(jnp.int32, s.shape, 1) + diag_offset
                s = jnp.where(col <= row, s, MASK_VALUE)
            chunks = [s[:, i * V7X_LANES:(i + 1) * V7X_LANES] for i in range(n_lane_chunks)]
            m_lane = chunks[0]
            for x in chunks[1:]:
                m_lane = jnp.maximum(m_lane, x)
            m_prev = m_scr[sl, :]
            m_next = jnp.maximum(m_prev, jnp.max(m_lane, axis=1, keepdims=True))
            alpha = jnp.exp2(m_prev - m_next)
            l_new = alpha * l_scr[sl, :]
            for i, x in enumerate(chunks):
                p = jnp.exp2(x - m_next)
                l_new = l_new + p
                p_buf[sl, i * V7X_LANES:(i + 1) * V7X_LANES] = p.astype(BF16)
            l_scr[sl, :] = l_new
            m_scr[sl, :] = m_next
            a_buf[sl, :] = alpha

    def pv(j, slot, first_live_row=0):
        start = pl.multiple_of(j * bk, bk)
        v = v_ref[pl.ds(start, bk), :]
        live = ([slice(0, rows)] if first_live_row == 0 else
                [slice(r0 + first_live_row, r0 + bq) for r0 in range(0, rows, bq)])
        for sl in live:
            acc_scr[sl, :] = acc_scr[sl, :] * a_bufs[slot][sl, :] + jnp.dot(
                p_bufs[slot][sl, :], v, preferred_element_type=F32)

    def two_steps(j, diag):
        pv(jnp.maximum(j - 1, 0), 1)
        softmax(0, 0 if diag else None)
        scores(j + 1, 1)
        pv(j, 0)
        softmax(1, bk if diag else None)
        if not diag:
            scores(j + 2, 0)

    scores(0, 0)

    def body(i, carry):
        two_steps(4 * i, False)
        two_steps(4 * i + 2, False)
        return carry

    lax.fori_loop(0, qi // 2, body, 0)

    @pl.when(qi % 2 == 1)
    def _():
        two_steps(2 * qi - 2, False)

    two_steps(2 * qi, True)
    pv(2 * qi + 1, 1, first_live_row=bk)
    return acc_scr[...] / jnp.sum(l_scr[...], axis=1, keepdims=True)


def _attn_scratch(rows, bq):
    bk = bq // 2
    stat = lambda: pltpu.VMEM((rows, V7X_LANES), F32)
    s_buf = lambda: pltpu.VMEM((rows, bk), F32)
    p_buf = lambda: pltpu.VMEM((rows, bk), BF16)
    return [s_buf(), s_buf(), p_buf(), p_buf()] + [stat() for _ in range(5)]


def _attn_vmem(seq, rows, bq, kv_width):
    resident = 2 * seq * kv_width * 2
    buffers = 2 * rows * (bq // 2) * (4 + 2) + 5 * rows * V7X_LANES * 4
    temps = 2 * rows * (bq // 2) * 4
    return resident + buffers + temps + (6 << 20)


def _diff_attn_kernel(slope_ref, q_ref, k_ref, v_ref, lq1_ref, lk1_ref, lq2_ref, lk2_ref,
                      subln_ref, o_ref, *scratch, bq, lambda_init):
    h, qi = pl.program_id(0), pl.program_id(1)
    q = q_ref[...]
    lane = lax.broadcasted_iota(jnp.int32, q.shape, 1)
    zero = jnp.zeros_like(q)
    qs = jnp.concatenate([jnp.where(lane < A_QK, q, zero), jnp.where(lane >= A_QK, q, zero)], axis=0)
    slope = slope_ref[h] * LOG2E
    t0 = qi * bq

    def bias_fn(start):
        kpos = lax.broadcasted_iota(jnp.int32, (1, bq // 2), 1) + (start - t0)
        return slope * kpos.astype(F32)

    o = _flash_attend(qs, k_ref, v_ref, qi, bias_fn, scratch, bq=bq)
    lam = (jnp.exp(jnp.sum(lq1_ref[...] * lk1_ref[...], axis=1, keepdims=True))
           - jnp.exp(jnp.sum(lq2_ref[...] * lk2_ref[...], axis=1, keepdims=True)) + lambda_init)
    d = o[:bq] - lam * o[bq:]
    o_ref[...] = (_rms(d, subln_ref[...], 1e-5) * (1.0 - lambda_init)).astype(o_ref.dtype)


def diff_attention(qkv, lq1, lk1, lq2, lk2, subln, lambda_init, *, bq=DIFF_ATTN_BLOCK):
    seq = qkv.shape[0]
    nq = seq // bq
    slopes = jnp.exp2(-8.0 * jnp.arange(1, A_HEADS + 1, dtype=F32) / A_HEADS)
    vec = lambda: pl.BlockSpec((1, A_QK), lambda h, i, s: (0, 0))
    grid_spec = pltpu.PrefetchScalarGridSpec(
        num_scalar_prefetch=1,
        grid=(A_HEADS, nq),
        in_specs=[
            pl.BlockSpec((bq, 128), lambda h, i, s: (i, h)),
            pl.BlockSpec((seq, 128), lambda h, i, s: (0, A_HEADS + h)),
            pl.BlockSpec((seq, 128), lambda h, i, s: (0, 2 * A_HEADS + h)),
            vec(), vec(), vec(), vec(),
            pl.BlockSpec((1, A_V), lambda h, i, s: (0, 0)),
        ],
        out_specs=pl.BlockSpec((bq, A_V), lambda h, i, s: (i, h)),
        scratch_shapes=_attn_scratch(2 * bq, bq),
    )
    return pl.pallas_call(
        functools.partial(_diff_attn_kernel, bq=bq, lambda_init=lambda_init),
        grid_spec=grid_spec,
        out_shape=jax.ShapeDtypeStruct((seq, A_HEADS * A_V), BF16),
        compiler_params=_cparams(("parallel", "arbitrary"), _attn_vmem(seq, 2 * bq, bq, 256)),
        name="diff_attention",
    )(slopes, qkv, qkv, qkv, lq1.reshape(1, -1), lk1.reshape(1, -1), lq2.reshape(1, -1),
      lk2.reshape(1, -1), subln.reshape(1, -1))


def _mla_prep_kernel(c_ref, qn_ref, kvn_ref, wq1_ref, wq2_ref, wkn_ref, wv_ref, cos_ref, sin_ref,
                     q_ref, k_ref, v_ref, *, scale):
    c = c_ref[...]
    cqn = _rms(c[:, :B_LORA], qn_ref[...], 1e-6).astype(BF16)
    ckvn = _rms(c[:, B_LORA:2 * B_LORA], kvn_ref[...], 1e-6).astype(BF16)
    kp = c[:, 2 * B_LORA:]
    cos, sin = cos_ref[...], sin_ref[...]
    qa = jnp.dot(cqn, wq1_ref[...], preferred_element_type=F32)
    qb = jnp.dot(cqn, wq2_ref[...], preferred_element_type=F32)
    kn = jnp.dot(ckvn, wkn_ref[...], preferred_element_type=F32)
    v_ref[...] = jnp.dot(ckvn, wv_ref[...], preferred_element_type=F32).astype(v_ref.dtype)
    kr = (kp * cos + pltpu.roll(kp, 64, axis=1) * sin).astype(k_ref.dtype)
    for h in range(B_HEADS):
        lo, hi = 256 * h, 256 * h + 128
        q_ref[:, lo:hi] = (qa[:, lo:hi] * scale).astype(q_ref.dtype)
        pe = qa[:, hi:hi + 128] * cos + qb[:, 128 * h:128 * (h + 1)] * sin
        q_ref[:, hi:hi + 128] = (pe * scale).astype(q_ref.dtype)
        k_ref[:, lo:hi] = kn[:, 128 * h:128 * (h + 1)].astype(k_ref.dtype)
        k_ref[:, hi:hi + 128] = kr


def mla_prep(c, q_norm, kv_norm, wq1, wq2, wkn, wv, cos_t, sin_t, *, tm=PREP_TM):
    seq = c.shape[0]
    scale = (B_NOPE + B_ROPE) ** -0.5 * LOG2E
    full = lambda a: pl.BlockSpec(a.shape, lambda i: (0, 0))
    row = lambda w: pl.BlockSpec((tm, w), lambda i: (i, 0))
    qn, kvn = q_norm.reshape(1, -1), kv_norm.reshape(1, -1)
    weights = 2 * 2 * (wq1.size + wq2.size + wkn.size + wv.size)
    vmem = weights + 2 * tm * (1152 * 4 + 256 * 4 + 5120 * 2) + tm * 5120 * 4 * 2 + (6 << 20)
    return pl.pallas_call(
        functools.partial(_mla_prep_kernel, scale=scale),
        grid=(seq // tm,),
        in_specs=[row(c.shape[1]), full(qn), full(kvn), full(wq1), full(wq2), full(wkn), full(wv),
                  row(128), row(128)],
        out_specs=[row(2048), row(2048), row(1024)],
        out_shape=[jax.ShapeDtypeStruct((seq, 2048), BF16), jax.ShapeDtypeStruct((seq, 2048), BF16),
                   jax.ShapeDtypeStruct((seq, 1024), BF16)],
        compiler_params=_cparams(("parallel",), vmem),
        name="mla_prep",
    )(c, qn, kvn, wq1, wq2, wkn, wv, cos_t, sin_t)


def _mla_attn_kernel(q_ref, k_ref, v_ref, o_ref, *scratch, bq):
    o = _flash_attend(q_ref[...], k_ref, v_ref, pl.program_id(1), None, scratch, bq=bq)
    o_ref[...] = o.astype(o_ref.dtype)


def mla_attention(q, k, v, *, bq=ATTN_BLOCK):
    seq = q.shape[0]
    return pl.pallas_call(
        functools.partial(_mla_attn_kernel, bq=bq),
        grid=(B_HEADS, seq // bq),
        in_specs=[
            pl.BlockSpec((bq, 256), lambda h, i: (i, h)),
            pl.BlockSpec((seq, 256), lambda h, i: (0, h)),
            pl.BlockSpec((seq, 128), lambda h, i: (0, h)),
        ],
        out_specs=pl.BlockSpec((bq, B_V), lambda h, i: (i, h)),
        out_shape=jax.ShapeDtypeStruct((seq, B_HEADS * B_V), BF16),
        scratch_shapes=_attn_scratch(bq, bq),
        compiler_params=_cparams(("parallel", "arbitrary"), _attn_vmem(seq, bq, bq, 384)),
        name="mla_attention",
    )(q, k, v)


def _forget_gate_kernel(h_ref, g_ref, wf_ref, b_ref, cum_ref, carry_scr):
    @pl.when(pl.program_id(0) == 0)
    def _():
        carry_scr[...] = jnp.zeros(carry_scr.shape, F32)

    xn = _rms(h_ref[...], g_ref[...], 1e-6).astype(BF16)
    f = lax.dot_general(wf_ref[...], xn, (((1,), (1,)), ((), ())), preferred_element_type=F32)
    z = f + b_ref[...]
    log_f = jnp.minimum(z, 0.0) - jnp.log(1.0 + jnp.exp(-jnp.abs(z)))
    r = lax.broadcasted_iota(jnp.int32, (V7X_LANES, V7X_LANES), 0)
    c = lax.broadcasted_iota(jnp.int32, (V7X_LANES, V7X_LANES), 1)
    tri = (r <= c).astype(F32)
    carry = carry_scr[...]
    for j in range(log_f.shape[1] // V7X_LANES):
        sl = slice(j * V7X_LANES, (j + 1) * V7X_LANES)
        loc = jnp.dot(log_f[:, sl], tri, preferred_element_type=F32,
                      precision=lax.Precision.HIGHEST) + carry
        cum_ref[:, sl] = loc
        carry = jnp.broadcast_to(loc[:, V7X_LANES - 1:], carry.shape)
    carry_scr[...] = carry


def forget_gate_cumsum(h, g, wf_t, bias, *, tm=GATE_TM):
    seq, d = h.shape
    nh = wf_t.shape[0]
    return pl.pallas_call(
        _forget_gate_kernel,
        grid=(seq // tm,),
        in_specs=[
            pl.BlockSpec((tm, d), lambda i: (i, 0)),
            pl.BlockSpec((1, d), lambda i: (0, 0)),
            pl.BlockSpec((nh, d), lambda i: (0, 0)),
            pl.BlockSpec((nh, 1), lambda i: (0, 0)),
        ],
        out_specs=pl.BlockSpec((nh, tm), lambda i: (0, i)),
        out_shape=jax.ShapeDtypeStruct((nh, seq), F32),
        scratch_shapes=[pltpu.VMEM((nh, V7X_LANES), F32)],
        compiler_params=_cparams(("arbitrary",), 2 * tm * d * 4 + 3 * tm * d * 4 + (4 << 20)),
        name="forget_gate_cumsum",
    )(h, g.reshape(1, d), wf_t, bias.reshape(nh, 1))


def _fox_attn_kernel(q_ref, k_ref, v_ref, cum_ref, o_ref, *scratch, bq):
    qi = pl.program_id(1)
    t0 = pl.multiple_of(qi * bq, bq)
    c0 = cum_ref[:, pl.ds(t0, V7X_LANES)][:, :1]

    def bias_fn(start):
        return (c0 - cum_ref[:, pl.ds(start, bq // 2)]) * LOG2E

    o = _flash_attend(q_ref[...], k_ref, v_ref, qi, bias_fn, scratch, bq=bq)
    o_ref[...] = o.astype(o_ref.dtype)


def fox_attention(qkv, cum, *, bq=ATTN_BLOCK):
    seq = qkv.shape[0]
    cum3 = cum.reshape(C_HEADS, 1, seq)
    return pl.pallas_call(
        functools.partial(_fox_attn_kernel, bq=bq),
        grid=(C_HEADS, seq // bq),
        in_specs=[
            pl.BlockSpec((bq, C_DIM), lambda h, i: (i, h)),
            pl.BlockSpec((seq, C_DIM), lambda h, i: (0, C_HEADS + h)),
            pl.BlockSpec((seq, C_DIM), lambda h, i: (0, 2 * C_HEADS + h)),
            pl.BlockSpec((None, 1, seq), lambda h, i: (h, 0, 0)),
        ],
        out_specs=pl.BlockSpec((bq, C_DIM), lambda h, i: (i, h)),
        out_shape=jax.ShapeDtypeStruct((seq, C_HEADS * C_DIM), BF16),
        scratch_shapes=_attn_scratch(bq, bq),
        compiler_params=_cparams(("parallel", "arbitrary"), _attn_vmem(seq, bq, bq, 256)),
        name="fox_attention",
    )(qkv, qkv, qkv, cum3)


def _swiglu_accumulate(xn, wg, wu, wd, acc_ref):
    gate = jnp.dot(xn, wg, preferred_element_type=F32)
    up = jnp.dot(xn, wu, preferred_element_type=F32)
    act = (gate * jax.nn.sigmoid(gate) * up).astype(BF16)
    acc_ref[...] += jnp.dot(act, wd, preferred_element_type=F32)


def _ffn_kernel(h_ref, g_ref, wg_ref, wu_ref, wd_ref, o_ref, xn_ref, acc_ref):
    f = pl.program_id(1)

    @pl.when(f == 0)
    def _():
        xn_ref[...] = _rms(h_ref[...], g_ref[...], 1e-6).astype(BF16)
        acc_ref[...] = jnp.zeros(acc_ref.shape, F32)

    _swiglu_accumulate(xn_ref[...], wg_ref[...], wu_ref[...], wd_ref[...], acc_ref)

    @pl.when(f == pl.num_programs(1) - 1)
    def _():
        o_ref[...] = h_ref[...] + acc_ref[...]


def _ffn_vmem(tm, d, tf):
    return 4 * tm * d * 4 + tm * d * 2 + tm * d * 4 + 3 * 2 * d * tf * 2 + 4 * tm * tf * 4 + (6 << 20)


def dense_ffn(h, g, wg, wu, wd, *, tm=FFN_TM, tf=FFN_TF):
    m, d = h.shape
    ff = wg.shape[1]
    assert m % tm == 0 and ff % tf == 0
    return pl.pallas_call(
        _ffn_kernel,
        grid=(m // tm, ff // tf),
        in_specs=[
            pl.BlockSpec((tm, d), lambda i, f: (i, 0)),
            pl.BlockSpec((1, d), lambda i, f: (0, 0)),
            pl.BlockSpec((d, tf), lambda i, f: (0, f)),
            pl.BlockSpec((d, tf), lambda i, f: (0, f)),
            pl.BlockSpec((tf, d), lambda i, f: (f, 0)),
        ],
        out_specs=pl.BlockSpec((tm, d), lambda i, f: (i, 0)),
        out_shape=jax.ShapeDtypeStruct((m, d), F32),
        scratch_shapes=[pltpu.VMEM((tm, d), BF16), pltpu.VMEM((tm, d), F32)],
        compiler_params=_cparams(("parallel", "arbitrary"), _ffn_vmem(tm, d, tf)),
        name="dense_ffn",
    )(h, g.reshape(1, d), wg, wu, wd)


def _router_kernel(h_ref, g_ref, wr_ref, idx_ref, gate_ref, rank_ref, cnt_ref, cnt_scr):
    @pl.when(pl.program_id(0) == 0)
    def _():
        cnt_scr[...] = jnp.zeros(cnt_scr.shape, F32)

    tm = h_ref.shape[0]
    xn = _rms(h_ref[...], g_ref[...], 1e-6)
    logits = jnp.dot(xn, wr_ref[...], preferred_element_type=F32, precision=lax.Precision.HIGHEST)
    lane = lax.broadcasted_iota(jnp.int32, logits.shape, 1)
    lane_f = lane.astype(F32)
    valid = lane < N_EXPERTS
    logits = jnp.where(valid, logits, MASK_VALUE)
    e = jnp.exp(logits - jnp.max(logits, axis=1, keepdims=True))
    probs = jnp.where(valid, e / jnp.sum(e, axis=1, keepdims=True), -1.0)
    p1 = jnp.max(probs, axis=1, keepdims=True)
    i1 = jnp.min(jnp.where(probs == p1, lane_f, float(V7X_LANES)), axis=1, keepdims=True)
    rest = jnp.where(lane_f == i1, -1.0, probs)
    p2 = jnp.max(rest, axis=1, keepdims=True)
    i2 = jnp.min(jnp.where(rest == p2, lane_f, float(V7X_LANES)), axis=1, keepdims=True)
    hot1 = (lane_f == i1).astype(F32)
    hot2 = (lane_f == i2).astype(F32)
    r = lax.broadcasted_iota(jnp.int32, (tm, tm), 0)
    c = lax.broadcasted_iota(jnp.int32, (tm, tm), 1)
    strict_lower = (c < r).astype(BF16)
    before = jnp.dot(strict_lower, (hot1 + hot2).astype(BF16), preferred_element_type=F32)
    before = before + cnt_scr[...]
    rank1 = jnp.sum(before * hot1, axis=1, keepdims=True)
    rank2 = jnp.sum(before * hot2, axis=1, keepdims=True)
    cnt_scr[...] += jnp.sum(hot1 + hot2, axis=0, keepdims=True)
    cnt_ref[...] = cnt_scr[...]
    denom = p1 + p2
    idx_ref[...] = jnp.where(lane == 0, i1, jnp.where(lane == 1, i2, 0.0)).astype(jnp.int32)
    gate_ref[...] = jnp.where(lane == 0, p1 / denom, jnp.where(lane == 1, p2 / denom, 0.0))
    rank_ref[...] = jnp.where(lane == 0, rank1, jnp.where(lane == 1, rank2, 0.0)).astype(jnp.int32)


def moe_router(h, g, w_router_padded, *, tm=ROUTER_TM):
    seq, d = h.shape
    out = lambda dt: jax.ShapeDtypeStruct((seq, V7X_LANES), dt)
    row = pl.BlockSpec((tm, V7X_LANES), lambda i: (i, 0))
    return pl.pallas_call(
        _router_kernel,
        grid=(seq // tm,),
        in_specs=[
            pl.BlockSpec((tm, d), lambda i: (i, 0)),
            pl.BlockSpec((1, d), lambda i: (0, 0)),
            pl.BlockSpec((d, V7X_LANES), lambda i: (0, 0)),
        ],
        out_specs=[row, row, row, pl.BlockSpec((1, V7X_LANES), lambda i: (0, 0))],
        out_shape=[out(jnp.int32), out(F32), out(jnp.int32),
                   jax.ShapeDtypeStruct((1, V7X_LANES), F32)],
        scratch_shapes=[pltpu.VMEM((1, V7X_LANES), F32)],
        compiler_params=_cparams(("arbitrary",), 5 * tm * d * 4 + 4 * tm * tm * 4 + (6 << 20)),
        name="moe_router",
    )(h, g.reshape(1, d), w_router_padded)


def _moe_ffn_kernel(te_ref, nv_ref, rows_ref, next_rows_ref, h_hbm, g_ref, wg_ref, wu_ref, wd_ref,
                    o_ref, x_buf, xn_ref, acc_ref, sem):
    i, f = pl.program_id(0), pl.program_id(1)
    live = i < nv_ref[0]
    tm = xn_ref.shape[0]
    slot = i % 2

    def row_copy(idx_ref, r, dst_slot):
        return pltpu.make_async_copy(h_hbm.at[pl.ds(idx_ref[0, 0, r], 1)],
                                     x_buf.at[dst_slot, pl.ds(r, 1)], sem.at[dst_slot])

    def start_tile(idx_ref, dst_slot):
        def body(r, carry):
            row_copy(idx_ref, r, dst_slot).start()
            return carry
        lax.fori_loop(0, tm, body, 0)

    @pl.when((f == 0) & (i == 0))
    def _():
        start_tile(rows_ref, slot)

    @pl.when(f == 0)
    def _():
        def wait(r, carry):
            row_copy(rows_ref, r, slot).wait()
            return carry
        lax.fori_loop(0, tm, wait, 0)

        @pl.when(i + 1 < pl.num_programs(0))
        def _():
            start_tile(next_rows_ref, 1 - slot)

        xn_ref[...] = _rms(x_buf[slot], g_ref[...], 1e-6).astype(BF16)
        acc_ref[...] = jnp.zeros(acc_ref.shape, F32)

    @pl.when(live)
    def _():
        _swiglu_accumulate(xn_ref[...], wg_ref[...], wu_ref[...], wd_ref[...], acc_ref)

    @pl.when(f == pl.num_programs(1) - 1)
    def _():
        o_ref[...] = acc_ref[...]


def moe_ffn(h, row_token, g, wg, wu, wd, tile_expert, n_live, *, tm=MOE_TM, tf=FFN_TF):
    rows, d = row_token.shape[0], h.shape[1]
    ff = wg.shape[2]
    nf = ff // tf
    n_tiles = rows // tm
    assert rows % tm == 0 and ff % tf == 0
    row_token = row_token.reshape(n_tiles, 1, tm)

    def f_idx(i, f, nv):
        return jnp.where(i < nv[0], f, nf - 1)

    grid_spec = pltpu.PrefetchScalarGridSpec(
        num_scalar_prefetch=2,
        grid=(n_tiles, nf),
        in_specs=[
            pl.BlockSpec((1, 1, tm), lambda i, f, te, nv: (i, 0, 0), memory_space=pltpu.SMEM),
            pl.BlockSpec((1, 1, tm), lambda i, f, te, nv: (jnp.minimum(i + 1, n_tiles - 1), 0, 0),
                         memory_space=pltpu.SMEM),
            pl.BlockSpec(memory_space=pl.ANY),
            pl.BlockSpec((1, d), lambda i, f, te, nv: (0, 0)),
            pl.BlockSpec((None, d, tf), lambda i, f, te, nv: (te[i], 0, f_idx(i, f, nv))),
            pl.BlockSpec((None, d, tf), lambda i, f, te, nv: (te[i], 0, f_idx(i, f, nv))),
            pl.BlockSpec((None, tf, d), lambda i, f, te, nv: (te[i], f_idx(i, f, nv), 0)),
        ],
        out_specs=pl.BlockSpec((tm, d), lambda i, f, te, nv: (i, 0)),
        scratch_shapes=[pltpu.VMEM((2, tm, d), F32), pltpu.VMEM((tm, d), BF16), pltpu.VMEM((tm, d), F32),
                        pltpu.SemaphoreType.DMA((2,))],
    )
    return pl.pallas_call(
        _moe_ffn_kernel,
        grid_spec=grid_spec,
        out_shape=jax.ShapeDtypeStruct((rows, d), F32),
        compiler_params=_cparams(("arbitrary", "arbitrary"), _ffn_vmem(tm, d, tf)),
        name="moe_ffn",
    )(tile_expert, n_live, row_token, row_token, h, g.reshape(1, d), wg, wu, wd)


def _combine_kernel(d0_ref, d1_ref, n0_ref, n1_ref, y_hbm, h_ref, gate_ref, fg_ref, o_ref, y_buf, sem):
    i = pl.program_id(0)
    n = h_ref.shape[0]
    slot = i % 2

    def copies(i0_ref, i1_ref, r, dst_slot):
        return (pltpu.make_async_copy(y_hbm.at[pl.ds(i0_ref[0, 0, r], 1)],
                                      y_buf.at[dst_slot, 0, pl.ds(r, 1)], sem.at[dst_slot, 0]),
                pltpu.make_async_copy(y_hbm.at[pl.ds(i1_ref[0, 0, r], 1)],
                                      y_buf.at[dst_slot, 1, pl.ds(r, 1)], sem.at[dst_slot, 1]))

    def start_tile(i0_ref, i1_ref, dst_slot):
        def body(r, carry):
            a, b = copies(i0_ref, i1_ref, r, dst_slot)
            a.start()
            b.start()
            return carry
        lax.fori_loop(0, n, body, 0)

    @pl.when(i == 0)
    def _():
        start_tile(d0_ref, d1_ref, slot)

    def wait(r, carry):
        a, b = copies(d0_ref, d1_ref, r, slot)
        a.wait()
        b.wait()
        return carry

    lax.fori_loop(0, n, wait, 0)

    @pl.when(i + 1 < pl.num_programs(0))
    def _():
        start_tile(n0_ref, n1_ref, 1 - slot)

    gate = gate_ref[...]
    out = h_ref[...] + gate[:, 0:1] * y_buf[slot, 0] + gate[:, 1:2] * y_buf[slot, 1]
    o_ref[...] = _rms(out, fg_ref[...], 1e-6)


def moe_combine(y_sorted, h, gates, dest0, dest1, final_gain, *, tc=GATHER_ROWS):
    seq, d = h.shape
    n_tiles = seq // tc
    cur_rows = pl.BlockSpec((1, 1, tc), lambda i: (i, 0, 0), memory_space=pltpu.SMEM)
    next_rows = pl.BlockSpec((1, 1, tc), lambda i: (jnp.minimum(i + 1, n_tiles - 1), 0, 0),
                             memory_space=pltpu.SMEM)
    dest0 = dest0.reshape(n_tiles, 1, tc)
    dest1 = dest1.reshape(n_tiles, 1, tc)
    return pl.pallas_call(
        _combine_kernel,
        grid=(n_tiles,),
        in_specs=[
            cur_rows, cur_rows, next_rows, next_rows,
            pl.BlockSpec(memory_space=pl.ANY),
            pl.BlockSpec((tc, d), lambda i: (i, 0)),
            pl.BlockSpec((tc, V7X_LANES), lambda i: (i, 0)),
            pl.BlockSpec((1, d), lambda i: (0, 0)),
        ],
        out_specs=pl.BlockSpec((tc, d), lambda i: (i, 0)),
        out_shape=jax.ShapeDtypeStruct((seq, d), F32),
        scratch_shapes=[pltpu.VMEM((2, 2, tc, d), F32), pltpu.SemaphoreType.DMA((2, 2))],
        compiler_params=_cparams(("arbitrary",), 12 * tc * d * 4 + (4 << 20)),
        name="moe_combine",
    )(dest0, dest1, dest0, dest1, y_sorted, h, gates, final_gain.reshape(1, d))


def _swap_halves(w):
    half = w.shape[-1] // 2
    return jnp.concatenate([w[..., half:], w[..., :half]], axis=-1)


def _mla_weights(w_uq, w_ukv):
    lora = w_uq.shape[0]
    uq = w_uq.reshape(lora, B_HEADS, B_NOPE + B_ROPE)
    nope, pe = uq[..., :B_NOPE], uq[..., B_NOPE:]
    zpad = jnp.zeros((lora, B_HEADS, 128 - B_ROPE), w_uq.dtype)
    wq1 = jnp.concatenate([nope, pe, zpad], axis=-1).reshape(lora, B_HEADS * 256)
    wq2 = jnp.concatenate([_swap_halves(pe), zpad], axis=-1).reshape(lora, B_HEADS * 128)
    ukv = w_ukv.reshape(lora, B_HEADS, B_NOPE + B_V)
    wkn = ukv[..., :B_NOPE].reshape(lora, B_HEADS * B_NOPE)
    wv = ukv[..., B_NOPE:].reshape(lora, B_HEADS * B_V)
    return wq1.astype(BF16), wq2.astype(BF16), wkn.astype(BF16), wv.astype(BF16)


def _rope_tables(seq):
    inv = ROPE_THETA ** (-jnp.arange(0, B_ROPE, 2, dtype=F32) / B_ROPE)
    ang = jnp.arange(seq, dtype=F32)[:, None] * inv[None, :]
    cos, sin = jnp.cos(ang), jnp.sin(ang)
    zeros = jnp.zeros((seq, 128 - B_ROPE), F32)
    return (jnp.concatenate([cos, cos, zeros], axis=1), jnp.concatenate([-sin, sin, zeros], axis=1))


def _dispatch_plan(idx, rank, counts, tm):
    seq = idx.shape[0]
    n_pairs = seq * TOP_K
    n_tiles = n_pairs // tm + N_EXPERTS
    cnt = counts[0, :N_EXPERTS].astype(jnp.int32)
    tiles_per = (cnt + tm - 1) // tm
    tile_end = jnp.cumsum(tiles_per)
    row_start = (tile_end - tiles_per) * tm
    e = idx[:, :TOP_K]
    dest = row_start[e] + rank[:, :TOP_K]
    tok = jnp.broadcast_to(jnp.arange(seq, dtype=jnp.int32)[:, None], (seq, TOP_K))
    row_token = jnp.zeros((n_tiles * tm,), jnp.int32).at[dest.reshape(-1)].set(tok.reshape(-1))
    n_live = tile_end[-1]
    tile_id = jnp.arange(n_tiles, dtype=jnp.int32)
    tile_expert = jnp.sum((tile_id[:, None] >= tile_end[None, :]).astype(jnp.int32), axis=1)
    last_expert = jnp.sum((n_live - 1 >= tile_end).astype(jnp.int32))
    tile_expert = jnp.where(tile_id < n_live, tile_expert, last_expert).astype(jnp.int32)
    return dest.astype(jnp.int32), row_token, tile_expert, n_live.reshape(1).astype(jnp.int32)


def kernel(x, ev_attn_norm, ev_w_in, ev_q_norm, ev_w_uq, ev_kv_norm, ev_w_ukv, ev_lambda_q1, ev_lambda_k1, ev_lambda_q2, ev_lambda_k2, ev_subln, ev_w_out, ev_ffn_norm, ev_ffn_w_gate, ev_ffn_w_up, ev_ffn_w_down, od_attn_norm, od_w_in, od_forget_bias, od_w_out, od_ffn_norm, od_router, od_moe_w_gate, od_moe_w_up, od_moe_w_down, final_norm):
    batch, seq, d = x.shape
    assert batch == 1
    h = x.reshape(seq, d)

    lambda_init = 0.8 - 0.6 * math.exp(-0.3 * 0)
    w_in = ev_w_in[0]
    n_a = 2 * A_HEADS * 2 * A_QK + A_HEADS * A_V
    w_a = w_in[:, :n_a].astype(BF16)
    scale_a = jnp.concatenate([jnp.full((A_HEADS * 2 * A_QK,), A_QK ** -0.5 * LOG2E, F32),
                               jnp.ones((n_a - A_HEADS * 2 * A_QK,), F32)])
    w_b = jnp.concatenate([w_in[:, n_a:], _swap_halves(w_in[:, -B_ROPE:])], axis=1).astype(BF16)
    qkv_a = rms_matmul(h, ev_attn_norm[0], w_a, scale_a, BF16)
    c_b = rms_matmul(h, ev_attn_norm[0], w_b, jnp.ones((w_b.shape[1],), F32), F32, tn=384)
    oa = diff_attention(qkv_a, ev_lambda_q1[0], ev_lambda_k1[0], ev_lambda_q2[0], ev_lambda_k2[0],
                        ev_subln[0], lambda_init)
    wq1, wq2, wkn, wv = _mla_weights(ev_w_uq[0], ev_w_ukv[0])
    cos_t, sin_t = _rope_tables(seq)
    q_b, k_b, v_b = mla_prep(c_b, ev_q_norm[0], ev_kv_norm[0], wq1, wq2, wkn, wv, cos_t, sin_t)
    ob = mla_attention(q_b, k_b, v_b)
    w_out = ev_w_out[0].astype(BF16)
    h = proj_residual([oa, ob], [w_out[:A_HEADS * A_V], w_out[A_HEADS * A_V:]], h)
    h = dense_ffn(h, ev_ffn_norm[0], ev_ffn_w_gate[0].astype(BF16), ev_ffn_w_up[0].astype(BF16),
                  ev_ffn_w_down[0].astype(BF16))

    width = C_HEADS * C_DIM
    w_qkv = od_w_in[0][:, :3 * width].astype(BF16)
    scale_c = jnp.concatenate([jnp.full((width,), C_DIM ** -0.5 * LOG2E, F32), jnp.ones((2 * width,), F32)])
    qkv_c = rms_matmul(h, od_attn_norm[0], w_qkv, scale_c, BF16)
    wf_t = od_w_in[0][:, 3 * width:].T.astype(BF16)
    cum = forget_gate_cumsum(h, od_attn_norm[0], wf_t, od_forget_bias[0])
    oc = fox_attention(qkv_c, cum)
    h = proj_residual([oc], [od_w_out[0].astype(BF16)], h)

    w_router = jnp.zeros((d, V7X_LANES), F32).at[:, :N_EXPERTS].set(od_router[0])
    idx, gates, rank, counts = moe_router(h, od_ffn_norm[0], w_router)
    dest, row_token, tile_expert, n_live = _dispatch_plan(idx, rank, counts, MOE_TM)
    y_sorted = moe_ffn(h, row_token, od_ffn_norm[0], od_moe_w_gate[0].astype(BF16),
                       od_moe_w_up[0].astype(BF16), od_moe_w_down[0].astype(BF16), tile_expert, n_live)
    out = moe_combine(y_sorted, h, gates, dest[:, 0], dest[:, 1], final_norm)
    return out.reshape(batch, seq, d)
```

```python
import functools
import math

import jax
import jax.numpy as jnp
from jax import lax
from jax.experimental import pallas as pl
from jax.experimental.pallas import tpu as pltpu

F32 = jnp.float32
BF16 = jnp.bfloat16

V7X_LANES = 128
V7X_VMEM_REQUEST_CAP = 60000 * 1024

A_HEADS, A_QK, A_V = 8, 64, 128
B_HEADS, B_LORA, B_NOPE, B_ROPE, B_V = 8, 512, 128, 64, 128
C_HEADS, C_DIM = 16, 128
N_EXPERTS, TOP_K = 8, 2
ROPE_THETA = 10000.0
MASK_VALUE = -1e30
LOG2E = math.log2(math.e)

ATTN_BLOCK = 1024
DIFF_ATTN_BLOCK = 1024
SOFTMAX_ROWS = 128
PROJ_TM, PROJ_TN = 1024, 512
FFN_TM, FFN_TF = 512, 512
MOE_TM = 512
COMBINE_ROWS = 256
PREP_TM = 512
GATE_TM = 1024
ROUTER_TM = 512


def _cparams(semantics, vmem_bytes):
    return pltpu.CompilerParams(
        dimension_semantics=semantics,
        vmem_limit_bytes=int(min(V7X_VMEM_REQUEST_CAP, vmem_bytes)))


def _rms(x, g, eps):
    return x * lax.rsqrt(jnp.mean(x * x, axis=-1, keepdims=True) + eps) * g


def _rms_matmul_kernel(x_ref, g_ref, w_ref, cs_ref, o_ref, xn_ref, *, eps):
    @pl.when(pl.program_id(1) == 0)
    def _():
        xn_ref[...] = _rms(x_ref[...], g_ref[...], eps).astype(BF16)

    acc = jnp.dot(xn_ref[...], w_ref[...], preferred_element_type=F32)
    o_ref[...] = (acc * cs_ref[...]).astype(o_ref.dtype)


def rms_matmul(x, g, w, col_scale, out_dtype, *, eps=1e-6, tm=PROJ_TM, tn=PROJ_TN):
    m, k = x.shape
    n = w.shape[1]
    assert m % tm == 0 and n % tn == 0, (m, n, tm, tn)
    vmem = 2 * tm * k * 4 + tm * k * 2 + 2 * k * tn * 2 + 4 * tm * tn * 4 + (4 << 20)
    return pl.pallas_call(
        functools.partial(_rms_matmul_kernel, eps=eps),
        grid=(m // tm, n // tn),
        in_specs=[
            pl.BlockSpec((tm, k), lambda i, j: (i, 0)),
            pl.BlockSpec((1, k), lambda i, j: (0, 0)),
            pl.BlockSpec((k, tn), lambda i, j: (0, j)),
            pl.BlockSpec((1, tn), lambda i, j: (0, j)),
        ],
        out_specs=pl.BlockSpec((tm, tn), lambda i, j: (i, j)),
        out_shape=jax.ShapeDtypeStruct((m, n), out_dtype),
        scratch_shapes=[pltpu.VMEM((tm, k), BF16)],
        compiler_params=_cparams(("parallel", "arbitrary"), vmem),
        name="rms_matmul",
    )(x, g.reshape(1, k), w, col_scale.reshape(1, n))


def _proj_residual_kernel(*refs, n_in):
    a_refs, w_refs = refs[:n_in], refs[n_in:2 * n_in]
    h_ref, o_ref = refs[2 * n_in], refs[2 * n_in + 1]
    acc = h_ref[...]
    for a_ref, w_ref in zip(a_refs, w_refs):
        acc = acc + jnp.dot(a_ref[...], w_ref[...], preferred_element_type=F32)
    o_ref[...] = acc


def proj_residual(a_list, w_list, h, *, tm=PROJ_TM, tn=PROJ_TN):
    m, n = h.shape
    n_in = len(a_list)
    assert m % tm == 0 and n % tn == 0
    ks = [a.shape[1] for a in a_list]
    vmem = sum(2 * tm * k * 2 + 2 * k * tn * 2 for k in ks) + 6 * tm * tn * 4 + (4 << 20)
    in_specs = [pl.BlockSpec((tm, k), lambda i, j: (i, 0)) for k in ks]
    in_specs += [pl.BlockSpec((k, tn), lambda i, j: (0, j)) for k in ks]
    in_specs += [pl.BlockSpec((tm, tn), lambda i, j: (i, j))]
    return pl.pallas_call(
        functools.partial(_proj_residual_kernel, n_in=n_in),
        grid=(m // tm, n // tn),
        in_specs=in_specs,
        out_specs=pl.BlockSpec((tm, tn), lambda i, j: (i, j)),
        out_shape=jax.ShapeDtypeStruct((m, n), F32),
        compiler_params=_cparams(("parallel", "arbitrary"), vmem),
        name="proj_residual",
    )(*a_list, *w_list, h)


def _flash_attend(q, k_ref, v_ref, qi, bias_fn, scratch, *, bq):
    s_bufs, p_bufs, a_bufs = scratch[0:2], scratch[2:4], scratch[4:6]
    m_scr, l_scr, acc_scr = scratch[6:9]
    rows, bk = s_bufs[0].shape
    assert bq == 2 * bk
    n_lane_chunks = bk // V7X_LANES
    m_scr[...] = jnp.full(m_scr.shape, MASK_VALUE, F32)
    l_scr[...] = jnp.zeros(l_scr.shape, F32)
    acc_scr[...] = jnp.zeros(acc_scr.shape, F32)
    a_bufs[1][...] = jnp.ones(a_bufs[1].shape, F32)
    p_bufs[1][...] = jnp.zeros(p_bufs[1].shape, BF16)

    def scores(j, slot):
        start = pl.multiple_of(j * bk, bk)
        s = lax.dot_general(q, k_ref[pl.ds(start, bk), :], (((1,), (1,)), ((), ())),
                            preferred_element_type=F32)
        s_bufs[slot][...] = s if bias_fn is None else s + bias_fn(start)

    def softmax(slot, diag_offset=None):
        s_buf, p_buf, a_buf = s_bufs[slot], p_bufs[slot], a_bufs[slot]
        for r0 in range(0, rows, SOFTMAX_ROWS):
            sl = slice(r0, r0 + SOFTMAX_ROWS)
            first_row = r0 % bq
            if diag_offset is not None and first_row + SOFTMAX_ROWS - 1 < diag_offset:
                continue
            s = s_buf[sl, :]
            if diag_offset is not None and first_row < diag_offset + bk - 1:
                row = lax.broadcasted_iota(jnp.int32, s.shape, 0) + first_row
                col = lax.broadcasted_iota(jnp.int32, s.shape, 1) + diag_offset
                s = jnp.where(col <= row, s, MASK_VALUE)
            chunks = [s[:, i * V7X_LANES:(i + 1) * V7X_LANES] for i in range(n_lane_chunks)]
            m_lane = chunks[0]
            for x in chunks[1:]:
                m_lane = jnp.maximum(m_lane, x)
            m_prev = m_scr[sl, :]
            m_next = jnp.maximum(m_prev, jnp.max(m_lane, axis=1, keepdims=True))
            alpha = jnp.exp2(m_prev - m_next)
            l_new = alpha * l_scr[sl, :]
            for i, x in enumerate(chunks):
                p = jnp.exp2(x - m_next)
                l_new = l_new + p
                p_buf[sl, i * V7X_LANES:(i + 1) * V7X_LANES] = p.astype(BF16)
            l_scr[sl, :] = l_new
            m_scr[sl, :] = m_next
            a_buf[sl, :] = alpha

    def pv(j, slot, first_live_row=0):
        start = pl.multiple_of(j * bk, bk)
        v = v_ref[pl.ds(start, bk), :]
        live = ([slice(0, rows)] if first_live_row == 0 else
                [slice(r0 + first_live_row, r0 + bq) for r0 in range(0, rows, bq)])
        for sl in live:
            acc_scr[sl, :] = acc_scr[sl, :] * a_bufs[slot][sl, :] + jnp.dot(
                p_bufs[slot][sl, :], v, preferred_element_type=F32)

    def two_steps(j, diag):
        pv(jnp.maximum(j - 1, 0), 1)
        softmax(0, 0 if diag else None)
        scores(j + 1, 1)
        pv(j, 0)
        softmax(1, bk if diag else None)
        if not diag:
            scores(j + 2, 0)

    scores(0, 0)

    def body(i, carry):
        two_steps(4 * i, False)
        two_steps(4 * i + 2, False)
        return carry

    lax.fori_loop(0, qi // 2, body, 0)

    @pl.when(qi % 2 == 1)
    def _():
        two_steps(2 * qi - 2, False)

    two_steps(2 * qi, True)
    pv(2 * qi + 1, 1, first_live_row=bk)
    return acc_scr[...] / jnp.sum(l_scr[...], axis=1, keepdims=True)


def _attn_scratch(rows, bq):
    bk = bq // 2
    stat = lambda: pltpu.VMEM((rows, V7X_LANES), F32)
    s_buf = lambda: pltpu.VMEM((rows, bk), F32)
    p_buf = lambda: pltpu.VMEM((rows, bk), BF16)
    return [s_buf(), s_buf(), p_buf(), p_buf()] + [stat() for _ in range(5)]


def _attn_vmem(seq, rows, bq, kv_width):
    resident = 2 * seq * kv_width * 2
    buffers = 2 * rows * (bq // 2) * (4 + 2) + 5 * rows * V7X_LANES * 4
    temps = 2 * rows * (bq // 2) * 4
    return resident + buffers + temps + (6 << 20)


def _diff_attn_kernel(slope_ref, q_ref, k_ref, v_ref, lq1_ref, lk1_ref, lq2_ref, lk2_ref,
                      subln_ref, o_ref, *scratch, bq, lambda_init):
    h, qi = pl.program_id(0), pl.program_id(1)
    q = q_ref[...]
    lane = lax.broadcasted_iota(jnp.int32, q.shape, 1)
    zero = jnp.zeros_like(q)
    qs = jnp.concatenate([jnp.where(lane < A_QK, q, zero), jnp.where(lane >= A_QK, q, zero)], axis=0)
    slope = slope_ref[h] * LOG2E
    t0 = qi * bq

    def bias_fn(start):
        kpos = lax.broadcasted_iota(jnp.int32, (1, bq // 2), 1) + (start - t0)
        return slope * kpos.astype(F32)

    o = _flash_attend(qs, k_ref, v_ref, qi, bias_fn, scratch, bq=bq)
    lam = (jnp.exp(jnp.sum(lq1_ref[...] * lk1_ref[...], axis=1, keepdims=True))
           - jnp.exp(jnp.sum(lq2_ref[...] * lk2_ref[...], axis=1, keepdims=True)) + lambda_init)
    d = o[:bq] - lam * o[bq:]
    o_ref[...] = (_rms(d, subln_ref[...], 1e-5) * (1.0 - lambda_init)).astype(o_ref.dtype)


def diff_attention(qkv, lq1, lk1, lq2, lk2, subln, lambda_init, *, bq=DIFF_ATTN_BLOCK):
    seq = qkv.shape[0]
    nq = seq // bq
    slopes = jnp.exp2(-8.0 * jnp.arange(1, A_HEADS + 1, dtype=F32) / A_HEADS)
    vec = lambda: pl.BlockSpec((1, A_QK), lambda h, i, s: (0, 0))
    grid_spec = pltpu.PrefetchScalarGridSpec(
        num_scalar_prefetch=1,
        grid=(A_HEADS, nq),
        in_specs=[
            pl.BlockSpec((bq, 128), lambda h, i, s: (i, h)),
            pl.BlockSpec((seq, 128), lambda h, i, s: (0, A_HEADS + h)),
            pl.BlockSpec((seq, 128), lambda h, i, s: (0, 2 * A_HEADS + h)),
            vec(), vec(), vec(), vec(),
            pl.BlockSpec((1, A_V), lambda h, i, s: (0, 0)),
        ],
        out_specs=pl.BlockSpec((bq, A_V), lambda h, i, s: (i, h)),
        scratch_shapes=_attn_scratch(2 * bq, bq),
    )
    return pl.pallas_call(
        functools.partial(_diff_attn_kernel, bq=bq, lambda_init=lambda_init),
        grid_spec=grid_spec,
        out_shape=jax.ShapeDtypeStruct((seq, A_HEADS * A_V), BF16),
        compiler_params=_cparams(("parallel", "arbitrary"), _attn_vmem(seq, 2 * bq, bq, 256)),
        name="diff_attention",
    )(slopes, qkv, qkv, qkv, lq1.reshape(1, -1), lk1.reshape(1, -1), lq2.reshape(1, -1),
      lk2.reshape(1, -1), subln.reshape(1, -1))


def _mla_prep_kernel(c_ref, qn_ref, kvn_ref, wq1_ref, wq2_ref, wkn_ref, wv_ref, cos_ref, sin_ref,
                     q_ref, k_ref, v_ref, *, scale):
    c = c_ref[...]
    cqn = _rms(c[:, :B_LORA], qn_ref[...], 1e-6).astype(BF16)
    ckvn = _rms(c[:, B_LORA:2 * B_LORA], kvn_ref[...], 1e-6).astype(BF16)
    kp = c[:, 2 * B_LORA:]
    cos, sin = cos_ref[...], sin_ref[...]
    qa = jnp.dot(cqn, wq1_ref[...], preferred_element_type=F32)
    qb = jnp.dot(cqn, wq2_ref[...], preferred_element_type=F32)
    kn = jnp.dot(ckvn, wkn_ref[...], preferred_element_type=F32)
    v_ref[...] = jnp.dot(ckvn, wv_ref[...], preferred_element_type=F32).astype(v_ref.dtype)
    kr = (kp * cos + pltpu.roll(kp, 64, axis=1) * sin).astype(k_ref.dtype)
    for h in range(B_HEADS):
        lo, hi = 256 * h, 256 * h + 128
        q_ref[:, lo:hi] = (qa[:, lo:hi] * scale).astype(q_ref.dtype)
        pe = qa[:, hi:hi + 128] * cos + qb[:, 128 * h:128 * (h + 1)] * sin
        q_ref[:, hi:hi + 128] = (pe * scale).astype(q_ref.dtype)
        k_ref[:, lo:hi] = kn[:, 128 * h:128 * (h + 1)].astype(k_ref.dtype)
        k_ref[:, hi:hi + 128] = kr


def mla_prep(c, q_norm, kv_norm, wq1, wq2, wkn, wv, cos_t, sin_t, *, tm=PREP_TM):
    seq = c.shape[0]
    scale = (B_NOPE + B_ROPE) ** -0.5 * LOG2E
    full = lambda a: pl.BlockSpec(a.shape, lambda i: (0, 0))
    row = lambda w: pl.BlockSpec((tm, w), lambda i: (i, 0))
    qn, kvn = q_norm.reshape(1, -1), kv_norm.reshape(1, -1)
    weights = 2 * 2 * (wq1.size + wq2.size + wkn.size + wv.size)
    vmem = weights + 2 * tm * (1152 * 4 + 256 * 4 + 5120 * 2) + tm * 5120 * 4 * 2 + (6 << 20)
    return pl.pallas_call(
        functools.partial(_mla_prep_kernel, scale=scale),
        grid=(seq // tm,),
        in_specs=[row(c.shape[1]), full(qn), full(kvn), full(wq1), full(wq2), full(wkn), full(wv),
                  row(128), row(128)],
        out_specs=[row(2048), row(2048), row(1024)],
        out_shape=[jax.ShapeDtypeStruct((seq, 2048), BF16), jax.ShapeDtypeStruct((seq, 2048), BF16),
                   jax.ShapeDtypeStruct((seq, 1024), BF16)],
        compiler_params=_cparams(("parallel",), vmem),
        name="mla_prep",
    )(c, qn, kvn, wq1, wq2, wkn, wv, cos_t, sin_t)


def _mla_attn_kernel(q_ref, k_ref, v_ref, o_ref, *scratch, bq):
    o = _flash_attend(q_ref[...], k_ref, v_ref, pl.program_id(1), None, scratch, bq=bq)
    o_ref[...] = o.astype(o_ref.dtype)


def mla_attention(q, k, v, *, bq=ATTN_BLOCK):
    seq = q.shape[0]
    return pl.pallas_call(
        functools.partial(_mla_attn_kernel, bq=bq),
        grid=(B_HEADS, seq // bq),
        in_specs=[
            pl.BlockSpec((bq, 256), lambda h, i: (i, h)),
            pl.BlockSpec((seq, 256), lambda h, i: (0, h)),
            pl.BlockSpec((seq, 128), lambda h, i: (0, h)),
        ],
        out_specs=pl.BlockSpec((bq, B_V), lambda h, i: (i, h)),
        out_shape=jax.ShapeDtypeStruct((seq, B_HEADS * B_V), BF16),
        scratch_shapes=_attn_scratch(bq, bq),
        compiler_params=_cparams(("parallel", "arbitrary"), _attn_vmem(seq, bq, bq, 384)),
        name="mla_attention",
    )(q, k, v)


def _forget_gate_kernel(h_ref, g_ref, wf_ref, b_ref, cum_ref, carry_scr):
    @pl.when(pl.program_id(0) == 0)
    def _():
        carry_scr[...] = jnp.zeros(carry_scr.shape, F32)

    xn = _rms(h_ref[...], g_ref[...], 1e-6).astype(BF16)
    f = lax.dot_general(wf_ref[...], xn, (((1,), (1,)), ((), ())), preferred_element_type=F32)
    z = f + b_ref[...]
    log_f = jnp.minimum(z, 0.0) - jnp.log(1.0 + jnp.exp(-jnp.abs(z)))
    r = lax.broadcasted_iota(jnp.int32, (V7X_LANES, V7X_LANES), 0)
    c = lax.broadcasted_iota(jnp.int32, (V7X_LANES, V7X_LANES), 1)
    tri = (r <= c).astype(F32)
    carry = carry_scr[...]
    for j in range(log_f.shape[1] // V7X_LANES):
        sl = slice(j * V7X_LANES, (j + 1) * V7X_LANES)
        loc = jnp.dot(log_f[:, sl], tri, preferred_element_type=F32,
                      precision=lax.Precision.HIGHEST) + carry
        cum_ref[:, sl] = loc
        carry = jnp.broadcast_to(loc[:, V7X_LANES - 1:], carry.shape)
    carry_scr[...] = carry


def forget_gate_cumsum(h, g, wf_t, bias, *, tm=GATE_TM):
    seq, d = h.shape
    nh = wf_t.shape[0]
    return pl.pallas_call(
        _forget_gate_kernel,
        grid=(seq // tm,),
        in_specs=[
            pl.BlockSpec((tm, d), lambda i: (i, 0)),
            pl.BlockSpec((1, d), lambda i: (0, 0)),
            pl.BlockSpec((nh, d), lambda i: (0, 0)),
            pl.BlockSpec((nh, 1), lambda i: (0, 0)),
        ],
        out_specs=pl.BlockSpec((nh, tm), lambda i: (0, i)),
        out_shape=jax.ShapeDtypeStruct((nh, seq), F32),
        scratch_shapes=[pltpu.VMEM((nh, V7X_LANES), F32)],
        compiler_params=_cparams(("arbitrary",), 2 * tm * d * 4 + 3 * tm * d * 4 + (4 << 20)),
        name="forget_gate_cumsum",
    )(h, g.reshape(1, d), wf_t, bias.reshape(nh, 1))


def _fox_attn_kernel(q_ref, k_ref, v_ref, cum_ref, o_ref, *scratch, bq):
    qi = pl.program_id(1)
    t0 = pl.multiple_of(qi * bq, bq)
    c0 = cum_ref[:, pl.ds(t0, V7X_LANES)][:, :1]

    def bias_fn(start):
        return (c0 - cum_ref[:, pl.ds(start, bq // 2)]) * LOG2E

    o = _flash_attend(q_ref[...], k_ref, v_ref, qi, bias_fn, scratch, bq=bq)
    o_ref[...] = o.astype(o_ref.dtype)


def fox_attention(qkv, cum, *, bq=ATTN_BLOCK):
    seq = qkv.shape[0]
    cum3 = cum.reshape(C_HEADS, 1, seq)
    return pl.pallas_call(
        functools.partial(_fox_attn_kernel, bq=bq),
        grid=(C_HEADS, seq // bq),
        in_specs=[
            pl.BlockSpec((bq, C_DIM), lambda h, i: (i, h)),
            pl.BlockSpec((seq, C_DIM), lambda h, i: (0, C_HEADS + h)),
            pl.BlockSpec((seq, C_DIM), lambda h, i: (0, 2 * C_HEADS + h)),
            pl.BlockSpec((None, 1, seq), lambda h, i: (h, 0, 0)),
        ],
        out_specs=pl.BlockSpec((bq, C_DIM), lambda h, i: (i, h)),
        out_shape=jax.ShapeDtypeStruct((seq, C_HEADS * C_DIM), BF16),
        scratch_shapes=_attn_scratch(bq, bq),
        compiler_params=_cparams(("parallel", "arbitrary"), _attn_vmem(seq, bq, bq, 256)),
        name="fox_attention",
    )(qkv, qkv, qkv, cum3)


def _swiglu_accumulate(xn, wg, wu, wd, acc_ref):
    gate = jnp.dot(xn, wg, preferred_element_type=F32)
    up = jnp.dot(xn, wu, preferred_element_type=F32)
    act = (gate * jax.nn.sigmoid(gate) * up).astype(BF16)
    acc_ref[...] += jnp.dot(act, wd, preferred_element_type=F32)


def _ffn_kernel(h_ref, g_ref, wg_ref, wu_ref, wd_ref, o_ref, xn_ref, acc_ref):
    f = pl.program_id(1)

    @pl.when(f == 0)
    def _():
        xn_ref[...] = _rms(h_ref[...], g_ref[...], 1e-6).astype(BF16)
        acc_ref[...] = jnp.zeros(acc_ref.shape, F32)

    _swiglu_accumulate(xn_ref[...], wg_ref[...], wu_ref[...], wd_ref[...], acc_ref)

    @pl.when(f == pl.num_programs(1) - 1)
    def _():
        o_ref[...] = h_ref[...] + acc_ref[...]


def _ffn_vmem(tm, d, tf):
    return 4 * tm * d * 4 + tm * d * 2 + tm * d * 4 + 3 * 2 * d * tf * 2 + 4 * tm * tf * 4 + (6 << 20)


def dense_ffn(h, g, wg, wu, wd, *, tm=FFN_TM, tf=FFN_TF):
    m, d = h.shape
    ff = wg.shape[1]
    assert m % tm == 0 and ff % tf == 0
    return pl.pallas_call(
        _ffn_kernel,
        grid=(m // tm, ff // tf),
        in_specs=[
            pl.BlockSpec((tm, d), lambda i, f: (i, 0)),
            pl.BlockSpec((1, d), lambda i, f: (0, 0)),
            pl.BlockSpec((d, tf), lambda i, f: (0, f)),
            pl.BlockSpec((d, tf), lambda i, f: (0, f)),
            pl.BlockSpec((tf, d), lambda i, f: (f, 0)),
        ],
        out_specs=pl.BlockSpec((tm, d), lambda i, f: (i, 0)),
        out_shape=jax.ShapeDtypeStruct((m, d), F32),
        scratch_shapes=[pltpu.VMEM((tm, d), BF16), pltpu.VMEM((tm, d), F32)],
        compiler_params=_cparams(("parallel", "arbitrary"), _ffn_vmem(tm, d, tf)),
        name="dense_ffn",
    )(h, g.reshape(1, d), wg, wu, wd)


def _router_kernel(h_ref, g_ref, wr_ref, idx_ref, gate_ref, rank_ref, cnt_ref, cnt_scr):
    @pl.when(pl.program_id(0) == 0)
    def _():
        cnt_scr[...] = jnp.zeros(cnt_scr.shape, F32)

    tm = h_ref.shape[0]
    xn = _rms(h_ref[...], g_ref[...], 1e-6)
    logits = jnp.dot(xn, wr_ref[...], preferred_element_type=F32, precision=lax.Precision.HIGHEST)
    lane = lax.broadcasted_iota(jnp.int32, logits.shape, 1)
    lane_f = lane.astype(F32)
    valid = lane < N_EXPERTS
    logits = jnp.where(valid, logits, MASK_VALUE)
    e = jnp.exp(logits - jnp.max(logits, axis=1, keepdims=True))
    probs = jnp.where(valid, e / jnp.sum(e, axis=1, keepdims=True), -1.0)
    p1 = jnp.max(probs, axis=1, keepdims=True)
    i1 = jnp.min(jnp.where(probs == p1, lane_f, float(V7X_LANES)), axis=1, keepdims=True)
    rest = jnp.where(lane_f == i1, -1.0, probs)
    p2 = jnp.max(rest, axis=1, keepdims=True)
    i2 = jnp.min(jnp.where(rest == p2, lane_f, float(V7X_LANES)), axis=1, keepdims=True)
    hot1 = (lane_f == i1).astype(F32)
    hot2 = (lane_f == i2).astype(F32)
    r = lax.broadcasted_iota(jnp.int32, (tm, tm), 0)
    c = lax.broadcasted_iota(jnp.int32, (tm, tm), 1)
    strict_lower = (c < r).astype(BF16)
    before = jnp.dot(strict_lower, (hot1 + hot2).astype(BF16), preferred_element_type=F32)
    before = before + cnt_scr[...]
    rank1 = jnp.sum(before * hot1, axis=1, keepdims=True)
    rank2 = jnp.sum(before * hot2, axis=1, keepdims=True)
    cnt_scr[...] += jnp.sum(hot1 + hot2, axis=0, keepdims=True)
    cnt_ref[...] = cnt_scr[...]
    denom = p1 + p2
    idx_ref[...] = jnp.where(lane == 0, i1, jnp.where(lane == 1, i2, 0.0)).astype(jnp.int32)
    gate_ref[...] = jnp.where(lane == 0, p1 / denom, jnp.where(lane == 1, p2 / denom, 0.0))
    rank_ref[...] = jnp.where(lane == 0, rank1, jnp.where(lane == 1, rank2, 0.0)).astype(jnp.int32)


def moe_router(h, g, w_router_padded, *, tm=ROUTER_TM):
    seq, d = h.shape
    out = lambda dt: jax.ShapeDtypeStruct((seq, V7X_LANES), dt)
    row = pl.BlockSpec((tm, V7X_LANES), lambda i: (i, 0))
    return pl.pallas_call(
        _router_kernel,
        grid=(seq // tm,),
        in_specs=[
            pl.BlockSpec((tm, d), lambda i: (i, 0)),
            pl.BlockSpec((1, d), lambda i: (0, 0)),
            pl.BlockSpec((d, V7X_LANES), lambda i: (0, 0)),
        ],
        out_specs=[row, row, row, pl.BlockSpec((1, V7X_LANES), lambda i: (0, 0))],
        out_shape=[out(jnp.int32), out(F32), out(jnp.int32),
                   jax.ShapeDtypeStruct((1, V7X_LANES), F32)],
        scratch_shapes=[pltpu.VMEM((1, V7X_LANES), F32)],
        compiler_params=_cparams(("arbitrary",), 5 * tm * d * 4 + 4 * tm * tm * 4 + (6 << 20)),
        name="moe_router",
    )(h, g.reshape(1, d), w_router_padded)


def _moe_ffn_kernel(te_ref, nv_ref, rows_ref, next_rows_ref, dest_ref, prev_dest_ref, h_hbm, g_ref,
                    wg_ref, wu_ref, wd_ref, y_hbm, x_buf, y_buf, xn_ref, acc_ref, in_sem, out_sem):
    i, f = pl.program_id(0), pl.program_id(1)
    n_tiles, n_f = pl.num_programs(0), pl.num_programs(1)
    live = i < nv_ref[0]
    tm = xn_ref.shape[0]
    n_rows = x_buf.shape[1]
    rows_per_step = n_rows // n_f
    slot = i % 2

    def gather_copy(idx_ref, r, dst_slot):
        return pltpu.make_async_copy(h_hbm.at[pl.ds(idx_ref[0, 0, r], 1)],
                                     x_buf.at[dst_slot, pl.ds(r, 1)], in_sem.at[dst_slot])

    def scatter_copy(idx_ref, r, src_slot):
        src_row = jnp.minimum(r, tm - 1)
        return pltpu.make_async_copy(y_buf.at[src_slot, pl.ds(src_row, 1)],
                                     y_hbm.at[pl.ds(idx_ref[0, 0, r], 1)], out_sem.at[src_slot])

    def wait_gather(dst_slot):
        pltpu.make_async_copy(h_hbm.at[pl.ds(0, n_rows)], x_buf.at[dst_slot], in_sem.at[dst_slot]).wait()

    def wait_scatter(src_slot):
        pltpu.make_async_copy(y_buf.at[src_slot], y_hbm.at[pl.ds(0, n_rows)], out_sem.at[src_slot]).wait()

    def step_rows():
        return [f * rows_per_step + k for k in range(rows_per_step)]

    def gather_next_tile_rows():
        for r in step_rows():
            gather_copy(next_rows_ref, r, 1 - slot).start()

    def scatter_prev_tile_rows():
        for r in step_rows():
            scatter_copy(prev_dest_ref, r, 1 - slot).start()

    @pl.when((f == 0) & (i == 0))
    def _():
        def body(r, carry):
            gather_copy(rows_ref, r, slot).start()
            return carry
        lax.fori_loop(0, n_rows, body, 0)

    @pl.when(f == 0)
    def _():
        wait_gather(slot)
        xn_ref[...] = _rms(x_buf[slot, :tm], g_ref[...], 1e-6).astype(BF16)
        acc_ref[...] = jnp.zeros(acc_ref.shape, F32)

    @pl.when(live & (i > 0))
    def _():
        _swiglu_accumulate(xn_ref[...], wg_ref[...], wu_ref[...], wd_ref[...], acc_ref)
        gather_next_tile_rows()
        scatter_prev_tile_rows()

    @pl.when(live & (i == 0))
    def _():
        _swiglu_accumulate(xn_ref[...], wg_ref[...], wu_ref[...], wd_ref[...], acc_ref)
        gather_next_tile_rows()

    @pl.when(jnp.logical_not(live))
    def _():
        gather_next_tile_rows()
        scatter_prev_tile_rows()

    @pl.when(f == n_f - 1)
    def _():
        @pl.when(i >= 2)
        def _():
            wait_scatter(slot)
        y_buf[slot, :tm] = acc_ref[...]

    @pl.when((f == n_f - 1) & (i == n_tiles - 1))
    def _():
        def body(r, carry):
            scatter_copy(dest_ref, r, slot).start()
            return carry
        lax.fori_loop(0, n_rows, body, 0)
        wait_gather(1 - slot)
        wait_scatter(1 - slot)
        wait_scatter(slot)


def _moe_tile_rows(tm, nf):
    return -(-tm // (8 * nf)) * 8 * nf


def moe_ffn(h, row_token, row_dest, n_y_rows, g, wg, wu, wd, tile_expert, n_live, *, tm=MOE_TM, tf=FFN_TF):
    d = h.shape[1]
    ff = wg.shape[2]
    nf = ff // tf
    n_tiles, _, n_rows = row_token.shape
    assert ff % tf == 0 and n_rows == _moe_tile_rows(tm, nf) and n_tiles >= 2

    def f_idx(i, f, nv):
        return jnp.where(i < nv[0], f, nf - 1)

    grid_spec = pltpu.PrefetchScalarGridSpec(
        num_scalar_prefetch=2,
        grid=(n_tiles, nf),
        in_specs=[
            pl.BlockSpec((1, 1, n_rows), lambda i, f, te, nv: (i, 0, 0), memory_space=pltpu.SMEM),
            pl.BlockSpec((1, 1, n_rows), lambda i, f, te, nv: (jnp.minimum(i + 1, n_tiles - 1), 0, 0),
                         memory_space=pltpu.SMEM),
            pl.BlockSpec((1, 1, n_rows), lambda i, f, te, nv: (i, 0, 0), memory_space=pltpu.SMEM),
            pl.BlockSpec((1, 1, n_rows), lambda i, f, te, nv: (jnp.maximum(i - 1, 0), 0, 0),
                         memory_space=pltpu.SMEM),
            pl.BlockSpec(memory_space=pl.ANY),
            pl.BlockSpec((1, d), lambda i, f, te, nv: (0, 0)),
            pl.BlockSpec((None, d, tf), lambda i, f, te, nv: (te[i], 0, f_idx(i, f, nv))),
            pl.BlockSpec((None, d, tf), lambda i, f, te, nv: (te[i], 0, f_idx(i, f, nv))),
            pl.BlockSpec((None, tf, d), lambda i, f, te, nv: (te[i], f_idx(i, f, nv), 0)),
        ],
        out_specs=pl.BlockSpec(memory_space=pl.ANY),
        scratch_shapes=[pltpu.VMEM((2, n_rows, d), F32), pltpu.VMEM((2, n_rows, d), F32),
                        pltpu.VMEM((tm, d), BF16), pltpu.VMEM((tm, d), F32),
                        pltpu.SemaphoreType.DMA((2,)), pltpu.SemaphoreType.DMA((2,))],
    )
    vmem = 4 * n_rows * d * 4 + tm * d * (2 + 4) + 3 * 2 * d * tf * 2 + 4 * tm * tf * 4 + (6 << 20)
    return pl.pallas_call(
        _moe_ffn_kernel,
        grid_spec=grid_spec,
        out_shape=jax.ShapeDtypeStruct((n_y_rows, d), F32),
        compiler_params=_cparams(("arbitrary", "arbitrary"), vmem),
        name="moe_ffn",
    )(tile_expert, n_live, row_token, row_token, row_dest, row_dest, h, g.reshape(1, d), wg, wu, wd)


def _combine_kernel(y0_ref, y1_ref, h_ref, gate_ref, fg_ref, o_ref):
    gate = gate_ref[...]
    out = h_ref[...] + gate[:, 0:1] * y0_ref[...] + gate[:, 1:2] * y1_ref[...]
    o_ref[...] = _rms(out, fg_ref[...], 1e-6)


def moe_combine(y, h, gates, final_gain, *, tc=COMBINE_ROWS):
    seq, d = h.shape
    n_tiles = seq // tc
    row = lambda w, off: pl.BlockSpec((tc, w), lambda i: (i + off, 0))
    return pl.pallas_call(
        _combine_kernel,
        grid=(n_tiles,),
        in_specs=[row(d, 0), row(d, n_tiles), row(d, 0), row(V7X_LANES, 0),
                  pl.BlockSpec((1, d), lambda i: (0, 0))],
        out_specs=row(d, 0),
        out_shape=jax.ShapeDtypeStruct((seq, d), F32),
        compiler_params=_cparams(("parallel",), 10 * tc * d * 4 + (4 << 20)),
        name="moe_combine",
    )(y, y, h, gates, final_gain.reshape(1, d))


def _swap_halves(w):
    half = w.shape[-1] // 2
    return jnp.concatenate([w[..., half:], w[..., :half]], axis=-1)


def _mla_weights(w_uq, w_ukv):
    lora = w_uq.shape[0]
    uq = w_uq.reshape(lora, B_HEADS, B_NOPE + B_ROPE)
    nope, pe = uq[..., :B_NOPE], uq[..., B_NOPE:]
    zpad = jnp.zeros((lora, B_HEADS, 128 - B_ROPE), w_uq.dtype)
    wq1 = jnp.concatenate([nope, pe, zpad], axis=-1).reshape(lora, B_HEADS * 256)
    wq2 = jnp.concatenate([_swap_halves(pe), zpad], axis=-1).reshape(lora, B_HEADS * 128)
    ukv = w_ukv.reshape(lora, B_HEADS, B_NOPE + B_V)
    wkn = ukv[..., :B_NOPE].reshape(lora, B_HEADS * B_NOPE)
    wv = ukv[..., B_NOPE:].reshape(lora, B_HEADS * B_V)
    return wq1.astype(BF16), wq2.astype(BF16), wkn.astype(BF16), wv.astype(BF16)


def _rope_tables(seq):
    inv = ROPE_THETA ** (-jnp.arange(0, B_ROPE, 2, dtype=F32) / B_ROPE)
    ang = jnp.arange(seq, dtype=F32)[:, None] * inv[None, :]
    cos, sin = jnp.cos(ang), jnp.sin(ang)
    zeros = jnp.zeros((seq, 128 - B_ROPE), F32)
    return (jnp.concatenate([cos, cos, zeros], axis=1), jnp.concatenate([-sin, sin, zeros], axis=1))


def _dispatch_plan(idx, rank, counts, tm, n_rows):
    seq = idx.shape[0]
    n_tiles = seq * TOP_K // tm + N_EXPERTS
    cnt = counts[0, :N_EXPERTS].astype(jnp.int32)
    tiles_per = (cnt + tm - 1) // tm
    tile_end = jnp.cumsum(tiles_per)
    row_start = (tile_end - tiles_per) * tm
    pos = (row_start[idx[:, :TOP_K]] + rank[:, :TOP_K]).reshape(-1)
    entry = (pos // tm) * n_rows + pos % tm
    tok = jnp.broadcast_to(jnp.arange(seq, dtype=jnp.int32)[:, None], (seq, TOP_K)).reshape(-1)
    slot = jnp.broadcast_to(jnp.arange(TOP_K, dtype=jnp.int32)[None, :], (seq, TOP_K)).reshape(-1)
    tile_id = jnp.arange(n_tiles, dtype=jnp.int32)
    spare = (TOP_K * seq + (tile_id[:, None] % 2) * n_rows
             + jnp.arange(n_rows, dtype=jnp.int32)[None, :]).reshape(-1)
    row_token = jnp.zeros((n_tiles * n_rows,), jnp.int32).at[entry].set(tok)
    row_dest = spare.at[entry].set(slot * seq + tok)
    n_live = tile_end[-1]
    tile_expert = jnp.sum((tile_id[:, None] >= tile_end[None, :]).astype(jnp.int32), axis=1)
    last_expert = jnp.sum((n_live - 1 >= tile_end).astype(jnp.int32))
    tile_expert = jnp.where(tile_id < n_live, tile_expert, last_expert).astype(jnp.int32)
    shape3 = (n_tiles, 1, n_rows)
    return (row_token.reshape(shape3), row_dest.reshape(shape3), tile_expert,
            n_live.reshape(1).astype(jnp.int32), TOP_K * seq + 2 * n_rows)


def kernel(x, ev_attn_norm, ev_w_in, ev_q_norm, ev_w_uq, ev_kv_norm, ev_w_ukv, ev_lambda_q1, ev_lambda_k1, ev_lambda_q2, ev_lambda_k2, ev_subln, ev_w_out, ev_ffn_norm, ev_ffn_w_gate, ev_ffn_w_up, ev_ffn_w_down, od_attn_norm, od_w_in, od_forget_bias, od_w_out, od_ffn_norm, od_router, od_moe_w_gate, od_moe_w_up, od_moe_w_down, final_norm):
    batch, seq, d = x.shape
    assert batch == 1
    h = x.reshape(seq, d)

    lambda_init = 0.8 - 0.6 * math.exp(-0.3 * 0)
    w_in = ev_w_in[0]
    n_a = 2 * A_HEADS * 2 * A_QK + A_HEADS * A_V
    w_a = w_in[:, :n_a].astype(BF16)
    scale_a = jnp.concatenate([jnp.full((A_HEADS * 2 * A_QK,), A_QK ** -0.5 * LOG2E, F32),
                               jnp.ones((n_a - A_HEADS * 2 * A_QK,), F32)])
    w_b = jnp.concatenate([w_in[:, n_a:], _swap_halves(w_in[:, -B_ROPE:])], axis=1).astype(BF16)
    qkv_a = rms_matmul(h, ev_attn_norm[0], w_a, scale_a, BF16)
    c_b = rms_matmul(h, ev_attn_norm[0], w_b, jnp.ones((w_b.shape[1],), F32), F32, tn=384)
    oa = diff_attention(qkv_a, ev_lambda_q1[0], ev_lambda_k1[0], ev_lambda_q2[0], ev_lambda_k2[0],
                        ev_subln[0], lambda_init)
    wq1, wq2, wkn, wv = _mla_weights(ev_w_uq[0], ev_w_ukv[0])
    cos_t, sin_t = _rope_tables(seq)
    q_b, k_b, v_b = mla_prep(c_b, ev_q_norm[0], ev_kv_norm[0], wq1, wq2, wkn, wv, cos_t, sin_t)
    ob = mla_attention(q_b, k_b, v_b)
    w_out = ev_w_out[0].astype(BF16)
    h = proj_residual([oa, ob], [w_out[:A_HEADS * A_V], w_out[A_HEADS * A_V:]], h)
    h = dense_ffn(h, ev_ffn_norm[0], ev_ffn_w_gate[0].astype(BF16), ev_ffn_w_up[0].astype(BF16),
                  ev_ffn_w_down[0].astype(BF16))

    width = C_HEADS * C_DIM
    w_qkv = od_w_in[0][:, :3 * width].astype(BF16)
    scale_c = jnp.concatenate([jnp.full((width,), C_DIM ** -0.5 * LOG2E, F32), jnp.ones((2 * width,), F32)])
    qkv_c = rms_matmul(h, od_attn_norm[0], w_qkv, scale_c, BF16)
    wf_t = od_w_in[0][:, 3 * width:].T.astype(BF16)
    cum = forget_gate_cumsum(h, od_attn_norm[0], wf_t, od_forget_bias[0])
    oc = fox_attention(qkv_c, cum)
    h = proj_residual([oc], [od_w_out[0].astype(BF16)], h)

    w_router = jnp.zeros((d, V7X_LANES), F32).at[:, :N_EXPERTS].set(od_router[0])
    idx, gates, rank, counts = moe_router(h, od_ffn_norm[0], w_router)
    n_rows = _moe_tile_rows(MOE_TM, od_moe_w_gate.shape[-1] // FFN_TF)
    row_token, row_dest, tile_expert, n_live, n_y_rows = _dispatch_plan(idx, rank, counts, MOE_TM, n_rows)
    y = moe_ffn(h, row_token, row_dest, n_y_rows, od_ffn_norm[0], od_moe_w_gate[0].astype(BF16),
                od_moe_w_up[0].astype(BF16), od_moe_w_down[0].astype(BF16), tile_expert, n_live)
    out = moe_combine(y, h, gates, final_norm)
    return out.reshape(batch, seq, d)
```

```python
import functools
import math

import jax
import jax.numpy as jnp
from jax import lax
from jax.experimental import pallas as pl
from jax.experimental.pallas import tpu as pltpu

F32 = jnp.float32
BF16 = jnp.bfloat16

V7X_LANES = 128
V7X_VMEM_REQUEST_CAP = 60000 * 1024

A_HEADS, A_QK, A_V = 8, 64, 128
B_HEADS, B_LORA, B_NOPE, B_ROPE, B_V = 8, 512, 128, 64, 128
C_HEADS, C_DIM = 16, 128
N_EXPERTS, TOP_K = 8, 2
ROPE_THETA = 10000.0
MASK_VALUE = -1e30
LOG2E = math.log2(math.e)

ATTN_BLOCK = 1024
DIFF_ATTN_BLOCK = 1024
HEADS_PER_STEP = 1
SOFTMAX_ROWS = 128
PROJ_TM, PROJ_TN = 1024, 1024
FFN_TM, FFN_TF = 512, 512
MOE_TM = 512
COMBINE_ROWS = 256
PREP_TM = 512
GATE_TM = 1024
ROUTER_TM = 512


def _cparams(semantics, vmem_bytes):
    return pltpu.CompilerParams(
        dimension_semantics=semantics,
        vmem_limit_bytes=int(min(V7X_VMEM_REQUEST_CAP, vmem_bytes)))


def _rms(x, g, eps):
    return x * lax.rsqrt(jnp.mean(x * x, axis=-1, keepdims=True) + eps) * g


def _rms_matmul_kernel(x_ref, g_ref, w_ref, cs_ref, o_ref, xn_ref, *, eps):
    @pl.when(pl.program_id(1) == 0)
    def _():
        xn_ref[...] = _rms(x_ref[...], g_ref[...], eps).astype(BF16)

    acc = jnp.dot(xn_ref[...], w_ref[...], preferred_element_type=F32)
    o_ref[...] = (acc * cs_ref[...]).astype(o_ref.dtype)


def rms_matmul(x, g, w, col_scale, out_dtype, *, eps=1e-6, tm=PROJ_TM, tn=PROJ_TN):
    m, k = x.shape
    n = w.shape[1]
    assert m % tm == 0 and n % tn == 0, (m, n, tm, tn)
    vmem = 2 * tm * k * 4 + tm * k * 2 + 2 * k * tn * 2 + 4 * tm * tn * 4 + (4 << 20)
    return pl.pallas_call(
        functools.partial(_rms_matmul_kernel, eps=eps),
        grid=(m // tm, n // tn),
        in_specs=[
            pl.BlockSpec((tm, k), lambda i, j: (i, 0)),
            pl.BlockSpec((1, k), lambda i, j: (0, 0)),
            pl.BlockSpec((k, tn), lambda i, j: (0, j)),
            pl.BlockSpec((1, tn), lambda i, j: (0, j)),
        ],
        out_specs=pl.BlockSpec((tm, tn), lambda i, j: (i, j)),
        out_shape=jax.ShapeDtypeStruct((m, n), out_dtype),
        scratch_shapes=[pltpu.VMEM((tm, k), BF16)],
        compiler_params=_cparams(("parallel", "arbitrary"), vmem),
        name="rms_matmul",
    )(x, g.reshape(1, k), w, col_scale.reshape(1, n))


def _proj_residual_kernel(*refs, n_in):
    a_refs, w_refs = refs[:n_in], refs[n_in:2 * n_in]
    h_ref, o_ref = refs[2 * n_in], refs[2 * n_in + 1]
    acc = h_ref[...]
    for a_ref, w_ref in zip(a_refs, w_refs):
        acc = acc + jnp.dot(a_ref[...], w_ref[...], preferred_element_type=F32)
    o_ref[...] = acc


def proj_residual(a_list, w_list, h, *, tm=PROJ_TM, tn=PROJ_TN):
    m, n = h.shape
    n_in = len(a_list)
    assert m % tm == 0 and n % tn == 0
    ks = [a.shape[1] for a in a_list]
    vmem = sum(2 * tm * k * 2 + 2 * k * tn * 2 for k in ks) + 6 * tm * tn * 4 + (4 << 20)
    in_specs = [pl.BlockSpec((tm, k), lambda i, j: (i, 0)) for k in ks]
    in_specs += [pl.BlockSpec((k, tn), lambda i, j: (0, j)) for k in ks]
    in_specs += [pl.BlockSpec((tm, tn), lambda i, j: (i, j))]
    return pl.pallas_call(
        functools.partial(_proj_residual_kernel, n_in=n_in),
        grid=(m // tm, n // tn),
        in_specs=in_specs,
        out_specs=pl.BlockSpec((tm, tn), lambda i, j: (i, j)),
        out_shape=jax.ShapeDtypeStruct((m, n), F32),
        compiler_params=_cparams(("parallel", "arbitrary"), vmem),
        name="proj_residual",
    )(*a_list, *w_list, h)


def _attn_stream(q, k_ref, v_ref, bias_fn, scratch, *, bq):
    s_bufs, p_bufs, a_bufs = scratch[0:2], scratch[2:4], scratch[4:6]
    m_scr, l_scr, acc_scr = scratch[6:9]
    rows, bk = s_bufs[0].shape
    assert bq == 2 * bk
    n_lane_chunks = bk // V7X_LANES
    m_scr[...] = jnp.full(m_scr.shape, MASK_VALUE, F32)
    l_scr[...] = jnp.zeros(l_scr.shape, F32)
    acc_scr[...] = jnp.zeros(acc_scr.shape, F32)
    a_bufs[1][...] = jnp.ones(a_bufs[1].shape, F32)
    p_bufs[1][...] = jnp.zeros(p_bufs[1].shape, BF16)

    def scores(j, slot):
        start = pl.multiple_of(j * bk, bk)
        s = lax.dot_general(q, k_ref[pl.ds(start, bk), :], (((1,), (1,)), ((), ())),
                            preferred_element_type=F32)
        s_bufs[slot][...] = s if bias_fn is None else s + bias_fn(start)

    def softmax(slot, diag_offset=None):
        s_buf, p_buf, a_buf = s_bufs[slot], p_bufs[slot], a_bufs[slot]
        for r0 in range(0, rows, SOFTMAX_ROWS):
            sl = slice(r0, r0 + SOFTMAX_ROWS)
            first_row = r0 % bq
            if diag_offset is not None and first_row + SOFTMAX_ROWS - 1 < diag_offset:
                continue
            s = s_buf[sl, :]
            if diag_offset is not None and first_row < diag_offset + bk - 1:
                row = lax.broadcasted_iota(jnp.int32, s.shape, 0) + first_row
                col = lax.broadcasted_iota(jnp.int32, s.shape, 1) + diag_offset
                s = jnp.where(col <= row, s, MASK_VALUE)
            chunks = [s[:, i * V7X_LANES:(i + 1) * V7X_LANES] for i in range(n_lane_chunks)]
            m_lane = chunks[0]
            for x in chunks[1:]:
                m_lane = jnp.maximum(m_lane, x)
            m_prev = m_scr[sl, :]
            m_next = jnp.maximum(m_prev, jnp.max(m_lane, axis=1, keepdims=True))
            alpha = jnp.exp2(m_prev - m_next)
            l_new = alpha * l_scr[sl, :]
            for i, x in enumerate(chunks):
                p = jnp.exp2(x - m_next)
                l_new = l_new + p
                p_buf[sl, i * V7X_LANES:(i + 1) * V7X_LANES] = p.astype(BF16)
            l_scr[sl, :] = l_new
            m_scr[sl, :] = m_next
            a_buf[sl, :] = alpha

    def pv(j, slot, first_live_row=0):
        start = pl.multiple_of(j * bk, bk)
        v = v_ref[pl.ds(start, bk), :]
        live = ([slice(0, rows)] if first_live_row == 0 else
                [slice(r0 + first_live_row, r0 + bq) for r0 in range(0, rows, bq)])
        for sl in live:
            acc_scr[sl, :] = acc_scr[sl, :] * a_bufs[slot][sl, :] + jnp.dot(
                p_bufs[slot][sl, :], v, preferred_element_type=F32)

    def result():
        return acc_scr[...] / jnp.sum(l_scr[...], axis=1, keepdims=True)

    return scores, softmax, pv, result


def _flash_attend(heads, qi, scratch, *, bq):
    bk = bq // 2
    per_head = len(scratch) // len(heads)
    streams = [_attn_stream(*head, scratch[t * per_head:(t + 1) * per_head], bq=bq)
               for t, head in enumerate(heads)]

    def each(stage, *args, **kwargs):
        for stream in streams:
            stream[stage](*args, **kwargs)

    SCORES, SOFTMAX, PV = 0, 1, 2

    def two_steps(j, diag):
        each(PV, jnp.maximum(j - 1, 0), 1)
        each(SOFTMAX, 0, 0 if diag else None)
        each(SCORES, j + 1, 1)
        each(PV, j, 0)
        each(SOFTMAX, 1, bk if diag else None)
        if not diag:
            each(SCORES, j + 2, 0)

    each(SCORES, 0, 0)

    def body(i, carry):
        two_steps(4 * i, False)
        two_steps(4 * i + 2, False)
        return carry

    lax.fori_loop(0, qi // 2, body, 0)

    @pl.when(qi % 2 == 1)
    def _():
        two_steps(2 * qi - 2, False)

    two_steps(2 * qi, True)
    each(PV, 2 * qi + 1, 1, first_live_row=bk)
    return [stream[3]() for stream in streams]


def _attn_scratch(rows, bq, n_heads=1):
    bk = bq // 2
    stat = lambda: pltpu.VMEM((rows, V7X_LANES), F32)
    s_buf = lambda: pltpu.VMEM((rows, bk), F32)
    p_buf = lambda: pltpu.VMEM((rows, bk), BF16)
    per_head = lambda: [s_buf(), s_buf(), p_buf(), p_buf()] + [stat() for _ in range(5)]
    return [buf for _ in range(n_heads) for buf in per_head()]


def _attn_vmem(seq, rows, bq, kv_width, n_heads=1, kv_buffers=2):
    resident = kv_buffers * seq * kv_width * 2 * n_heads
    buffers = n_heads * (2 * rows * (bq // 2) * (4 + 2) + 5 * rows * V7X_LANES * 4)
    temps = 2 * rows * (bq // 2) * 4
    return resident + buffers + temps + (6 << 20)


def _kv_buffers(n_heads):
    return 2 if n_heads == 1 else 1


def _resident(block_shape, index_map, n_heads):
    if _kv_buffers(n_heads) == 2:
        return pl.BlockSpec(block_shape, index_map)
    return pl.BlockSpec(block_shape, index_map, pipeline_mode=pl.Buffered(1))


def _diff_attn_kernel(slope_ref, q_ref, k_ref, v_ref, lq1_ref, lk1_ref, lq2_ref, lk2_ref,
                      subln_ref, o_ref, *scratch, bq, lambda_init):
    h, qi = pl.program_id(0), pl.program_id(1)
    q = q_ref[...]
    lane = lax.broadcasted_iota(jnp.int32, q.shape, 1)
    zero = jnp.zeros_like(q)
    qs = jnp.concatenate([jnp.where(lane < A_QK, q, zero), jnp.where(lane >= A_QK, q, zero)], axis=0)
    slope = slope_ref[h] * LOG2E
    t0 = qi * bq

    def bias_fn(start):
        kpos = lax.broadcasted_iota(jnp.int32, (1, bq // 2), 1) + (start - t0)
        return slope * kpos.astype(F32)

    o, = _flash_attend([(qs, k_ref, v_ref, bias_fn)], qi, scratch, bq=bq)
    lam = (jnp.exp(jnp.sum(lq1_ref[...] * lk1_ref[...], axis=1, keepdims=True))
           - jnp.exp(jnp.sum(lq2_ref[...] * lk2_ref[...], axis=1, keepdims=True)) + lambda_init)
    d = o[:bq] - lam * o[bq:]
    o_ref[...] = (_rms(d, subln_ref[...], 1e-5) * (1.0 - lambda_init)).astype(o_ref.dtype)


def diff_attention(qkv, lq1, lk1, lq2, lk2, subln, lambda_init, *, bq=DIFF_ATTN_BLOCK):
    seq = qkv.shape[0]
    nq = seq // bq
    slopes = jnp.exp2(-8.0 * jnp.arange(1, A_HEADS + 1, dtype=F32) / A_HEADS)
    vec = lambda: pl.BlockSpec((1, A_QK), lambda h, i, s: (0, 0))
    grid_spec = pltpu.PrefetchScalarGridSpec(
        num_scalar_prefetch=1,
        grid=(A_HEADS, nq),
        in_specs=[
            pl.BlockSpec((bq, 128), lambda h, i, s: (i, h)),
            pl.BlockSpec((seq, 128), lambda h, i, s: (0, A_HEADS + h)),
            pl.BlockSpec((seq, 128), lambda h, i, s: (0, 2 * A_HEADS + h)),
            vec(), vec(), vec(), vec(),
            pl.BlockSpec((1, A_V), lambda h, i, s: (0, 0)),
        ],
        out_specs=pl.BlockSpec((bq, A_V), lambda h, i, s: (i, h)),
        scratch_shapes=_attn_scratch(2 * bq, bq),
    )
    return pl.pallas_call(
        functools.partial(_diff_attn_kernel, bq=bq, lambda_init=lambda_init),
        grid_spec=grid_spec,
        out_shape=jax.ShapeDtypeStruct((seq, A_HEADS * A_V), BF16),
        compiler_params=_cparams(("parallel", "arbitrary"), _attn_vmem(seq, 2 * bq, bq, 256)),
        name="diff_attention",
    )(slopes, qkv, qkv, qkv, lq1.reshape(1, -1), lk1.reshape(1, -1), lq2.reshape(1, -1),
      lk2.reshape(1, -1), subln.reshape(1, -1))


def _mla_prep_kernel(c_ref, qn_ref, kvn_ref, wq1_ref, wq2_ref, wkn_ref, wv_ref, cos_ref, sin_ref,
                     q_ref, k_ref, v_ref, *, scale):
    c = c_ref[...]
    cqn = _rms(c[:, :B_LORA], qn_ref[...], 1e-6).astype(BF16)
    ckvn = _rms(c[:, B_LORA:2 * B_LORA], kvn_ref[...], 1e-6).astype(BF16)
    kp = c[:, 2 * B_LORA:]
    cos, sin = cos_ref[...], sin_ref[...]
    qa = jnp.dot(cqn, wq1_ref[...], preferred_element_type=F32)
    qb = jnp.dot(cqn, wq2_ref[...], preferred_element_type=F32)
    kn = jnp.dot(ckvn, wkn_ref[...], preferred_element_type=F32)
    v_ref[...] = jnp.dot(ckvn, wv_ref[...], preferred_element_type=F32).astype(v_ref.dtype)
    kr = (kp * cos + pltpu.roll(kp, 64, axis=1) * sin).astype(k_ref.dtype)
    for h in range(B_HEADS):
        lo, hi = 256 * h, 256 * h + 128
        q_ref[:, lo:hi] = (qa[:, lo:hi] * scale).astype(q_ref.dtype)
        pe = qa[:, hi:hi + 128] * cos + qb[:, 128 * h:128 * (h + 1)] * sin
        q_ref[:, hi:hi + 128] = (pe * scale).astype(q_ref.dtype)
        k_ref[:, lo:hi] = kn[:, 128 * h:128 * (h + 1)].astype(k_ref.dtype)
        k_ref[:, hi:hi + 128] = kr


def mla_prep(c, q_norm, kv_norm, wq1, wq2, wkn, wv, cos_t, sin_t, *, tm=PREP_TM):
    seq = c.shape[0]
    scale = (B_NOPE + B_ROPE) ** -0.5 * LOG2E
    full = lambda a: pl.BlockSpec(a.shape, lambda i: (0, 0))
    row = lambda w: pl.BlockSpec((tm, w), lambda i: (i, 0))
    qn, kvn = q_norm.reshape(1, -1), kv_norm.reshape(1, -1)
    weights = 2 * 2 * (wq1.size + wq2.size + wkn.size + wv.size)
    vmem = weights + 2 * tm * (1152 * 4 + 256 * 4 + 5120 * 2) + tm * 5120 * 4 * 2 + (6 << 20)
    return pl.pallas_call(
        functools.partial(_mla_prep_kernel, scale=scale),
        grid=(seq // tm,),
        in_specs=[row(c.shape[1]), full(qn), full(kvn), full(wq1), full(wq2), full(wkn), full(wv),
                  row(128), row(128)],
        out_specs=[row(2048), row(2048), row(1024)],
        out_shape=[jax.ShapeDtypeStruct((seq, 2048), BF16), jax.ShapeDtypeStruct((seq, 2048), BF16),
                   jax.ShapeDtypeStruct((seq, 1024), BF16)],
        compiler_params=_cparams(("parallel",), vmem),
        name="mla_prep",
    )(c, qn, kvn, wq1, wq2, wkn, wv, cos_t, sin_t)


def _mla_attn_kernel(q_ref, k_ref, v_ref, o_ref, *scratch, bq):
    heads = [(q_ref[:, 256 * t:256 * (t + 1)], k_ref.at[:, pl.ds(256 * t, 256)],
              v_ref.at[:, pl.ds(B_V * t, B_V)], None) for t in range(HEADS_PER_STEP)]
    for t, o in enumerate(_flash_attend(heads, pl.program_id(1), scratch, bq=bq)):
        o_ref[:, B_V * t:B_V * (t + 1)] = o.astype(o_ref.dtype)


def mla_attention(q, k, v, *, bq=ATTN_BLOCK, hp=HEADS_PER_STEP):
    seq = q.shape[0]
    return pl.pallas_call(
        functools.partial(_mla_attn_kernel, bq=bq),
        grid=(B_HEADS // hp, seq // bq),
        in_specs=[
            pl.BlockSpec((bq, 256 * hp), lambda h, i: (i, h)),
            _resident((seq, 256 * hp), lambda h, i: (0, h), hp),
            _resident((seq, B_V * hp), lambda h, i: (0, h), hp),
        ],
        out_specs=pl.BlockSpec((bq, B_V * hp), lambda h, i: (i, h)),
        out_shape=jax.ShapeDtypeStruct((seq, B_HEADS * B_V), BF16),
        scratch_shapes=_attn_scratch(bq, bq, hp),
        compiler_params=_cparams(("parallel", "arbitrary"), _attn_vmem(seq, bq, bq, 384, hp, _kv_buffers(hp))),
        name="mla_attention",
    )(q, k, v)


def _forget_gate_kernel(h_ref, g_ref, wf_ref, b_ref, cum_ref, carry_scr):
    @pl.when(pl.program_id(0) == 0)
    def _():
        carry_scr[...] = jnp.zeros(carry_scr.shape, F32)

    xn = _rms(h_ref[...], g_ref[...], 1e-6).astype(BF16)
    f = lax.dot_general(wf_ref[...], xn, (((1,), (1,)), ((), ())), preferred_element_type=F32)
    z = f + b_ref[...]
    log_f = jnp.minimum(z, 0.0) - jnp.log(1.0 + jnp.exp(-jnp.abs(z)))
    r = lax.broadcasted_iota(jnp.int32, (V7X_LANES, V7X_LANES), 0)
    c = lax.broadcasted_iota(jnp.int32, (V7X_LANES, V7X_LANES), 1)
    tri = (r <= c).astype(F32)
    carry = carry_scr[...]
    for j in range(log_f.shape[1] // V7X_LANES):
        sl = slice(j * V7X_LANES, (j + 1) * V7X_LANES)
        loc = jnp.dot(log_f[:, sl], tri, preferred_element_type=F32,
                      precision=lax.Precision.HIGHEST) + carry
        cum_ref[:, sl] = loc
        carry = jnp.broadcast_to(loc[:, V7X_LANES - 1:], carry.shape)
    carry_scr[...] = carry


def forget_gate_cumsum(h, g, wf_t, bias, *, tm=GATE_TM):
    seq, d = h.shape
    nh = wf_t.shape[0]
    return pl.pallas_call(
        _forget_gate_kernel,
        grid=(seq // tm,),
        in_specs=[
            pl.BlockSpec((tm, d), lambda i: (i, 0)),
            pl.BlockSpec((1, d), lambda i: (0, 0)),
            pl.BlockSpec((nh, d), lambda i: (0, 0)),
            pl.BlockSpec((nh, 1), lambda i: (0, 0)),
        ],
        out_specs=pl.BlockSpec((nh, tm), lambda i: (0, i)),
        out_shape=jax.ShapeDtypeStruct((nh, seq), F32),
        scratch_shapes=[pltpu.VMEM((nh, V7X_LANES), F32)],
        compiler_params=_cparams(("arbitrary",), 2 * tm * d * 4 + 3 * tm * d * 4 + (4 << 20)),
        name="forget_gate_cumsum",
    )(h, g.reshape(1, d), wf_t, bias.reshape(nh, 1))


def _fox_attn_kernel(q_ref, k_ref, v_ref, cum_ref, o_ref, *scratch, bq):
    qi = pl.program_id(1)
    t0 = pl.multiple_of(qi * bq, bq)

    def head(t):
        cum = cum_ref.at[t]
        c0 = cum[:, pl.ds(t0, V7X_LANES)][:, :1]

        def bias_fn(start):
            return (c0 - cum[:, pl.ds(start, bq // 2)]) * LOG2E

        lanes = pl.ds(C_DIM * t, C_DIM)
        return q_ref[:, C_DIM * t:C_DIM * (t + 1)], k_ref.at[:, lanes], v_ref.at[:, lanes], bias_fn

    heads = [head(t) for t in range(HEADS_PER_STEP)]
    for t, o in enumerate(_flash_attend(heads, qi, scratch, bq=bq)):
        o_ref[:, C_DIM * t:C_DIM * (t + 1)] = o.astype(o_ref.dtype)


def fox_attention(qkv, cum, *, bq=ATTN_BLOCK, hp=HEADS_PER_STEP):
    seq = qkv.shape[0]
    cum3 = cum.reshape(C_HEADS, 1, seq)
    n_groups = C_HEADS // hp
    return pl.pallas_call(
        functools.partial(_fox_attn_kernel, bq=bq),
        grid=(n_groups, seq // bq),
        in_specs=[
            pl.BlockSpec((bq, C_DIM * hp), lambda h, i: (i, h)),
            _resident((seq, C_DIM * hp), lambda h, i: (0, n_groups + h), hp),
            _resident((seq, C_DIM * hp), lambda h, i: (0, 2 * n_groups + h), hp),
            pl.BlockSpec((hp, 1, seq), lambda h, i: (h, 0, 0)),
        ],
        out_specs=pl.BlockSpec((bq, C_DIM * hp), lambda h, i: (i, h)),
        out_shape=jax.ShapeDtypeStruct((seq, C_HEADS * C_DIM), BF16),
        scratch_shapes=_attn_scratch(bq, bq, hp),
        compiler_params=_cparams(("parallel", "arbitrary"), _attn_vmem(seq, bq, bq, 256, hp, _kv_buffers(hp))),
        name="fox_attention",
    )(qkv, qkv, qkv, cum3)


def _swiglu_accumulate(xn, wg, wu, wd, acc_ref):
    gate = jnp.dot(xn, wg, preferred_element_type=F32)
    up = jnp.dot(xn, wu, preferred_element_type=F32)
    act = (gate * jax.nn.sigmoid(gate) * up).astype(BF16)
    acc_ref[...] += jnp.dot(act, wd, preferred_element_type=F32)


def _ffn_kernel(h_ref, g_ref, wg_ref, wu_ref, wd_ref, o_ref, xn_ref, acc_ref):
    f = pl.program_id(1)

    @pl.when(f == 0)
    def _():
        xn_ref[...] = _rms(h_ref[...], g_ref[...], 1e-6).astype(BF16)
        acc_ref[...] = jnp.zeros(acc_ref.shape, F32)

    _swiglu_accumulate(xn_ref[...], wg_ref[...], wu_ref[...], wd_ref[...], acc_ref)

    @pl.when(f == pl.num_programs(1) - 1)
    def _():
        o_ref[...] = h_ref[...] + acc_ref[...]


def _ffn_vmem(tm, d, tf):
    return 4 * tm * d * 4 + tm * d * 2 + tm * d * 4 + 3 * 2 * d * tf * 2 + 4 * tm * tf * 4 + (6 << 20)


def dense_ffn(h, g, wg, wu, wd, *, tm=FFN_TM, tf=FFN_TF):
    m, d = h.shape
    ff = wg.shape[1]
    assert m % tm == 0 and ff % tf == 0
    return pl.pallas_call(
        _ffn_kernel,
        grid=(m // tm, ff // tf),
        in_specs=[
            pl.BlockSpec((tm, d), lambda i, f: (i, 0)),
            pl.BlockSpec((1, d), lambda i, f: (0, 0)),
            pl.BlockSpec((d, tf), lambda i, f: (0, f)),
            pl.BlockSpec((d, tf), lambda i, f: (0, f)),
            pl.BlockSpec((tf, d), lambda i, f: (f, 0)),
        ],
        out_specs=pl.BlockSpec((tm, d), lambda i, f: (i, 0)),
        out_shape=jax.ShapeDtypeStruct((m, d), F32),
        scratch_shapes=[pltpu.VMEM((tm, d), BF16), pltpu.VMEM((tm, d), F32)],
        compiler_params=_cparams(("parallel", "arbitrary"), _ffn_vmem(tm, d, tf)),
        name="dense_ffn",
    )(h, g.reshape(1, d), wg, wu, wd)


def _router_kernel(h_ref, g_ref, wr_ref, idx_ref, gate_ref, rank_ref, cnt_ref, cnt_scr):
    @pl.when(pl.program_id(0) == 0)
    def _():
        cnt_scr[...] = jnp.zeros(cnt_scr.shape, F32)

    tm = h_ref.shape[0]
    xn = _rms(h_ref[...], g_ref[...], 1e-6)
    logits = jnp.dot(xn, wr_ref[...], preferred_element_type=F32, precision=lax.Precision.HIGHEST)
    lane = lax.broadcasted_iota(jnp.int32, logits.shape, 1)
    lane_f = lane.astype(F32)
    valid = lane < N_EXPERTS
    logits = jnp.where(valid, logits, MASK_VALUE)
    e = jnp.exp(logits - jnp.max(logits, axis=1, keepdims=True))
    probs = jnp.where(valid, e / jnp.sum(e, axis=1, keepdims=True), -1.0)
    p1 = jnp.max(probs, axis=1, keepdims=True)
    i1 = jnp.min(jnp.where(probs == p1, lane_f, float(V7X_LANES)), axis=1, keepdims=True)
    rest = jnp.where(lane_f == i1, -1.0, probs)
    p2 = jnp.max(rest, axis=1, keepdims=True)
    i2 = jnp.min(jnp.where(rest == p2, lane_f, float(V7X_LANES)), axis=1, keepdims=True)
    hot1 = (lane_f == i1).astype(F32)
    hot2 = (lane_f == i2).astype(F32)
    r = lax.broadcasted_iota(jnp.int32, (tm, tm), 0)
    c = lax.broadcasted_iota(jnp.int32, (tm, tm), 1)
    strict_lower = (c < r).astype(BF16)
    before = jnp.dot(strict_lower, (hot1 + hot2).astype(BF16), preferred_element_type=F32)
    before = before + cnt_scr[...]
    rank1 = jnp.sum(before * hot1, axis=1, keepdims=True)
    rank2 = jnp.sum(before * hot2, axis=1, keepdims=True)
    cnt_scr[...] += jnp.sum(hot1 + hot2, axis=0, keepdims=True)
    cnt_ref[...] = cnt_scr[...]
    denom = p1 + p2
    idx_ref[...] = jnp.where(lane == 0, i1, jnp.where(lane == 1, i2, 0.0)).astype(jnp.int32)
    gate_ref[...] = jnp.where(lane == 0, p1 / denom, jnp.where(lane == 1, p2 / denom, 0.0))
    rank_ref[...] = jnp.where(lane == 0, rank1, jnp.where(lane == 1, rank2, 0.0)).astype(jnp.int32)


def moe_router(h, g, w_router_padded, *, tm=ROUTER_TM):
    seq, d = h.shape
    out = lambda dt: jax.ShapeDtypeStruct((seq, V7X_LANES), dt)
    row = pl.BlockSpec((tm, V7X_LANES), lambda i: (i, 0))
    return pl.pallas_call(
        _router_kernel,
        grid=(seq // tm,),
        in_specs=[
            pl.BlockSpec((tm, d), lambda i: (i, 0)),
            pl.BlockSpec((1, d), lambda i: (0, 0)),
            pl.BlockSpec((d, V7X_LANES), lambda i: (0, 0)),
        ],
        out_specs=[row, row, row, pl.BlockSpec((1, V7X_LANES), lambda i: (0, 0))],
        out_shape=[out(jnp.int32), out(F32), out(jnp.int32),
                   jax.ShapeDtypeStruct((1, V7X_LANES), F32)],
        scratch_shapes=[pltpu.VMEM((1, V7X_LANES), F32)],
        compiler_params=_cparams(("arbitrary",), 5 * tm * d * 4 + 4 * tm * tm * 4 + (6 << 20)),
        name="moe_router",
    )(h, g.reshape(1, d), w_router_padded)


def _moe_ffn_kernel(te_ref, nv_ref, rows_ref, next_rows_ref, dest_ref, prev_dest_ref, h_hbm, g_ref,
                    wg_ref, wu_ref, wd_ref, y_hbm, x_buf, y_buf, xn_ref, acc_ref, in_sem, out_sem):
    i, f = pl.program_id(0), pl.program_id(1)
    n_tiles, n_f = pl.num_programs(0), pl.num_programs(1)
    live = i < nv_ref[0]
    tm = xn_ref.shape[0]
    n_rows = x_buf.shape[1]
    rows_per_step = n_rows // n_f
    slot = i % 2

    def gather_copy(idx_ref, r, dst_slot):
        return pltpu.make_async_copy(h_hbm.at[pl.ds(idx_ref[0, 0, r], 1)],
                                     x_buf.at[dst_slot, pl.ds(r, 1)], in_sem.at[dst_slot])

    def scatter_copy(idx_ref, r, src_slot):
        src_row = jnp.minimum(r, tm - 1)
        return pltpu.make_async_copy(y_buf.at[src_slot, pl.ds(src_row, 1)],
                                     y_hbm.at[pl.ds(idx_ref[0, 0, r], 1)], out_sem.at[src_slot])

    def wait_gather(dst_slot):
        pltpu.make_async_copy(h_hbm.at[pl.ds(0, n_rows)], x_buf.at[dst_slot], in_sem.at[dst_slot]).wait()

    def wait_scatter(src_slot):
        pltpu.make_async_copy(y_buf.at[src_slot], y_hbm.at[pl.ds(0, n_rows)], out_sem.at[src_slot]).wait()

    def step_rows():
        return [f * rows_per_step + k for k in range(rows_per_step)]

    def gather_next_tile_rows():
        for r in step_rows():
            gather_copy(next_rows_ref, r, 1 - slot).start()

    def scatter_prev_tile_rows():
        for r in step_rows():
            scatter_copy(prev_dest_ref, r, 1 - slot).start()

    @pl.when((f == 0) & (i == 0))
    def _():
        def body(r, carry):
            gather_copy(rows_ref, r, slot).start()
            return carry
        lax.fori_loop(0, n_rows, body, 0)

    @pl.when(f == 0)
    def _():
        wait_gather(slot)
        xn_ref[...] = _rms(x_buf[slot, :tm], g_ref[...], 1e-6).astype(BF16)
        acc_ref[...] = jnp.zeros(acc_ref.shape, F32)

    @pl.when(live & (i > 0))
    def _():
        _swiglu_accumulate(xn_ref[...], wg_ref[...], wu_ref[...], wd_ref[...], acc_ref)
        gather_next_tile_rows()
        scatter_prev_tile_rows()

    @pl.when(live & (i == 0))
    def _():
        _swiglu_accumulate(xn_ref[...], wg_ref[...], wu_ref[...], wd_ref[...], acc_ref)
        gather_next_tile_rows()

    @pl.when(jnp.logical_not(live))
    def _():
        gather_next_tile_rows()
        scatter_prev_tile_rows()

    @pl.when(f == n_f - 1)
    def _():
        @pl.when(i >= 2)
        def _():
            wait_scatter(slot)
        y_buf[slot, :tm] = acc_ref[...]

    @pl.when((f == n_f - 1) & (i == n_tiles - 1))
    def _():
        def body(r, carry):
            scatter_copy(dest_ref, r, slot).start()
            return carry
        lax.fori_loop(0, n_rows, body, 0)
        wait_gather(1 - slot)
        wait_scatter(1 - slot)
        wait_scatter(slot)


def _moe_tile_rows(tm, nf):
    return -(-tm // (8 * nf)) * 8 * nf


def moe_ffn(h, row_token, row_dest, n_y_rows, g, wg, wu, wd, tile_expert, n_live, *, tm=MOE_TM, tf=FFN_TF):
    d = h.shape[1]
    ff = wg.shape[2]
    nf = ff // tf
    n_tiles, _, n_rows = row_token.shape
    assert ff % tf == 0 and n_rows == _moe_tile_rows(tm, nf) and n_tiles >= 2

    def f_idx(i, f, nv):
        return jnp.where(i < nv[0], f, nf - 1)

    grid_spec = pltpu.PrefetchScalarGridSpec(
        num_scalar_prefetch=2,
        grid=(n_tiles, nf),
        in_specs=[
            pl.BlockSpec((1, 1, n_rows), lambda i, f, te, nv: (i, 0, 0), memory_space=pltpu.SMEM),
            pl.BlockSpec((1, 1, n_rows), lambda i, f, te, nv: (jnp.minimum(i + 1, n_tiles - 1), 0, 0),
                         memory_space=pltpu.SMEM),
            pl.BlockSpec((1, 1, n_rows), lambda i, f, te, nv: (i, 0, 0), memory_space=pltpu.SMEM),
            pl.BlockSpec((1, 1, n_rows), lambda i, f, te, nv: (jnp.maximum(i - 1, 0), 0, 0),
                         memory_space=pltpu.SMEM),
            pl.BlockSpec(memory_space=pl.ANY),
            pl.BlockSpec((1, d), lambda i, f, te, nv: (0, 0)),
            pl.BlockSpec((None, d, tf), lambda i, f, te, nv: (te[i], 0, f_idx(i, f, nv))),
            pl.BlockSpec((None, d, tf), lambda i, f, te, nv: (te[i], 0, f_idx(i, f, nv))),
            pl.BlockSpec((None, tf, d), lambda i, f, te, nv: (te[i], f_idx(i, f, nv), 0)),
        ],
        out_specs=pl.BlockSpec(memory_space=pl.ANY),
        scratch_shapes=[pltpu.VMEM((2, n_rows, d), F32), pltpu.VMEM((2, n_rows, d), F32),
                        pltpu.VMEM((tm, d), BF16), pltpu.VMEM((tm, d), F32),
                        pltpu.SemaphoreType.DMA((2,)), pltpu.SemaphoreType.DMA((2,))],
    )
    vmem = 4 * n_rows * d * 4 + tm * d * (2 + 4) + 3 * 2 * d * tf * 2 + 4 * tm * tf * 4 + (6 << 20)
    return pl.pallas_call(
        _moe_ffn_kernel,
        grid_spec=grid_spec,
        out_shape=jax.ShapeDtypeStruct((n_y_rows, d), F32),
        compiler_params=_cparams(("arbitrary", "arbitrary"), vmem),
        name="moe_ffn",
    )(tile_expert, n_live, row_token, row_token, row_dest, row_dest, h, g.reshape(1, d), wg, wu, wd)


def _combine_kernel(y0_ref, y1_ref, h_ref, gate_ref, fg_ref, o_ref):
    gate = gate_ref[...]
    out = h_ref[...] + gate[:, 0:1] * y0_ref[...] + gate[:, 1:2] * y1_ref[...]
    o_ref[...] = _rms(out, fg_ref[...], 1e-6)


def moe_combine(y, h, gates, final_gain, *, tc=COMBINE_ROWS):
    seq, d = h.shape
    n_tiles = seq // tc
    row = lambda w, off: pl.BlockSpec((tc, w), lambda i: (i + off, 0))
    return pl.pallas_call(
        _combine_kernel,
        grid=(n_tiles,),
        in_specs=[row(d, 0), row(d, n_tiles), row(d, 0), row(V7X_LANES, 0),
                  pl.BlockSpec((1, d), lambda i: (0, 0))],
        out_specs=row(d, 0),
        out_shape=jax.ShapeDtypeStruct((seq, d), F32),
        compiler_params=_cparams(("parallel",), 10 * tc * d * 4 + (4 << 20)),
        name="moe_combine",
    )(y, y, h, gates, final_gain.reshape(1, d))


def _swap_halves(w):
    half = w.shape[-1] // 2
    return jnp.concatenate([w[..., half:], w[..., :half]], axis=-1)


def _mla_weights(w_uq, w_ukv):
    lora = w_uq.shape[0]
    uq = w_uq.reshape(lora, B_HEADS, B_NOPE + B_ROPE)
    nope, pe = uq[..., :B_NOPE], uq[..., B_NOPE:]
    zpad = jnp.zeros((lora, B_HEADS, 128 - B_ROPE), w_uq.dtype)
    wq1 = jnp.concatenate([nope, pe, zpad], axis=-1).reshape(lora, B_HEADS * 256)
    wq2 = jnp.concatenate([_swap_halves(pe), zpad], axis=-1).reshape(lora, B_HEADS * 128)
    ukv = w_ukv.reshape(lora, B_HEADS, B_NOPE + B_V)
    wkn = ukv[..., :B_NOPE].reshape(lora, B_HEADS * B_NOPE)
    wv = ukv[..., B_NOPE:].reshape(lora, B_HEADS * B_V)
    return wq1.astype(BF16), wq2.astype(BF16), wkn.astype(BF16), wv.astype(BF16)


def _rope_tables(seq):
    inv = ROPE_THETA ** (-jnp.arange(0, B_ROPE, 2, dtype=F32) / B_ROPE)
    ang = jnp.arange(seq, dtype=F32)[:, None] * inv[None, :]
    cos, sin = jnp.cos(ang), jnp.sin(ang)
    zeros = jnp.zeros((seq, 128 - B_ROPE), F32)
    return (jnp.concatenate([cos, cos, zeros], axis=1), jnp.concatenate([-sin, sin, zeros], axis=1))


def _dispatch_plan(idx, rank, counts, tm, n_rows):
    seq = idx.shape[0]
    n_tiles = seq * TOP_K // tm + N_EXPERTS
    cnt = counts[0, :N_EXPERTS].astype(jnp.int32)
    tiles_per = (cnt + tm - 1) // tm
    tile_end = jnp.cumsum(tiles_per)
    row_start = (tile_end - tiles_per) * tm
    pos = (row_start[idx[:, :TOP_K]] + rank[:, :TOP_K]).reshape(-1)
    entry = (pos // tm) * n_rows + pos % tm
    tile_id = jnp.arange(n_tiles, dtype=jnp.int32)
    spare = (TOP_K * seq + (tile_id[:, None] % 2) * n_rows
             + jnp.arange(n_rows, dtype=jnp.int32)[None, :]).reshape(-1)
    pair = jnp.full((n_tiles * n_rows,), -1, jnp.int32).at[entry].set(
        jnp.arange(seq * TOP_K, dtype=jnp.int32))
    tok, slot = pair // TOP_K, pair % TOP_K
    row_token = jnp.where(pair >= 0, tok, 0)
    row_dest = jnp.where(pair >= 0, slot * seq + tok, spare)
    n_live = tile_end[-1]
    tile_expert = jnp.sum((tile_id[:, None] >= tile_end[None, :]).astype(jnp.int32), axis=1)
    last_expert = jnp.sum((n_live - 1 >= tile_end).astype(jnp.int32))
    tile_expert = jnp.where(tile_id < n_live, tile_expert, last_expert).astype(jnp.int32)
    shape3 = (n_tiles, 1, n_rows)
    return (row_token.reshape(shape3), row_dest.reshape(shape3), tile_expert,
            n_live.reshape(1).astype(jnp.int32), TOP_K * seq + 2 * n_rows)


def kernel(x, ev_attn_norm, ev_w_in, ev_q_norm, ev_w_uq, ev_kv_norm, ev_w_ukv, ev_lambda_q1, ev_lambda_k1, ev_lambda_q2, ev_lambda_k2, ev_subln, ev_w_out, ev_ffn_norm, ev_ffn_w_gate, ev_ffn_w_up, ev_ffn_w_down, od_attn_norm, od_w_in, od_forget_bias, od_w_out, od_ffn_norm, od_router, od_moe_w_gate, od_moe_w_up, od_moe_w_down, final_norm):
    batch, seq, d = x.shape
    assert batch == 1
    h = x.reshape(seq, d)

    lambda_init = 0.8 - 0.6 * math.exp(-0.3 * 0)
    w_in = ev_w_in[0]
    n_a = 2 * A_HEADS * 2 * A_QK + A_HEADS * A_V
    w_a = w_in[:, :n_a].astype(BF16)
    scale_a = jnp.concatenate([jnp.full((A_HEADS * 2 * A_QK,), A_QK ** -0.5 * LOG2E, F32),
                               jnp.ones((n_a - A_HEADS * 2 * A_QK,), F32)])
    w_b = jnp.concatenate([w_in[:, n_a:], _swap_halves(w_in[:, -B_ROPE:])], axis=1).astype(BF16)
    qkv_a = rms_matmul(h, ev_attn_norm[0], w_a, scale_a, BF16)
    c_b = rms_matmul(h, ev_attn_norm[0], w_b, jnp.ones((w_b.shape[1],), F32), F32, tn=w_b.shape[1])
    oa = diff_attention(qkv_a, ev_lambda_q1[0], ev_lambda_k1[0], ev_lambda_q2[0], ev_lambda_k2[0],
                        ev_subln[0], lambda_init)
    wq1, wq2, wkn, wv = _mla_weights(ev_w_uq[0], ev_w_ukv[0])
    cos_t, sin_t = _rope_tables(seq)
    q_b, k_b, v_b = mla_prep(c_b, ev_q_norm[0], ev_kv_norm[0], wq1, wq2, wkn, wv, cos_t, sin_t)
    ob = mla_attention(q_b, k_b, v_b)
    w_out = ev_w_out[0].astype(BF16)
    h = proj_residual([oa, ob], [w_out[:A_HEADS * A_V], w_out[A_HEADS * A_V:]], h)
    h = dense_ffn(h, ev_ffn_norm[0], ev_ffn_w_gate[0].astype(BF16), ev_ffn_w_up[0].astype(BF16),
                  ev_ffn_w_down[0].astype(BF16))

    width = C_HEADS * C_DIM
    w_qkv = od_w_in[0][:, :3 * width].astype(BF16)
    scale_c = jnp.concatenate([jnp.full((width,), C_DIM ** -0.5 * LOG2E, F32), jnp.ones((2 * width,), F32)])
    qkv_c = rms_matmul(h, od_attn_norm[0], w_qkv, scale_c, BF16)
    wf_t = od_w_in[0][:, 3 * width:].T.astype(BF16)
    cum = forget_gate_cumsum(h, od_attn_norm[0], wf_t, od_forget_bias[0])
    oc = fox_attention(qkv_c, cum)
    h = proj_residual([oc], [od_w_out[0].astype(BF16)], h)

    w_router = jnp.zeros((d, V7X_LANES), F32).at[:, :N_EXPERTS].set(od_router[0])
    idx, gates, rank, counts = moe_router(h, od_ffn_norm[0], w_router)
    n_rows = _moe_tile_rows(MOE_TM, od_moe_w_gate.shape[-1] // FFN_TF)
    row_token, row_dest, tile_expert, n_live, n_y_rows = _dispatch_plan(idx, rank, counts, MOE_TM, n_rows)
    y = moe_ffn(h, row_token, row_dest, n_y_rows, od_ffn_norm[0], od_moe_w_gate[0].astype(BF16),
                od_moe_w_up[0].astype(BF16), od_moe_w_down[0].astype(BF16), tile_expert, n_live)
    out = moe_combine(y, h, gates, final_norm)
    return out.reshape(batch, seq, d)
```

```python
import functools
import math

import jax
import jax.numpy as jnp
from jax import lax
from jax.experimental import pallas as pl
from jax.experimental.pallas import tpu as pltpu

F32 = jnp.float32
BF16 = jnp.bfloat16

V7X_LANES = 128
V7X_VMEM_REQUEST_CAP = 60000 * 1024

A_HEADS, A_QK, A_V = 8, 64, 128
B_HEADS, B_LORA, B_NOPE, B_ROPE, B_V = 8, 512, 128, 64, 128
C_HEADS, C_DIM = 16, 128
N_EXPERTS, TOP_K = 8, 2
ROPE_THETA = 10000.0
MASK_VALUE = -1e30
LOG2E = math.log2(math.e)

ATTN_BLOCK = 1024
DIFF_ATTN_BLOCK = 1024
HEADS_PER_STEP = 1
SOFTMAX_ROWS = 128
PROJ_TM, PROJ_TN = 1024, 1024
FFN_TM, FFN_TF = 512, 512
MOE_TM = 512
COMBINE_ROWS = 256
PREP_TM = 512
GATE_TM = 1024
ROUTER_TM = 512


def _cparams(semantics, vmem_bytes):
    return pltpu.CompilerParams(
        dimension_semantics=semantics,
        vmem_limit_bytes=int(min(V7X_VMEM_REQUEST_CAP, vmem_bytes)))


def _rms(x, g, eps):
    return x * lax.rsqrt(jnp.mean(x * x, axis=-1, keepdims=True) + eps) * g


def _rms_matmul_kernel(x_ref, g_ref, w_ref, cs_ref, o_ref, xn_ref, *, eps):
    @pl.when(pl.program_id(1) == 0)
    def _():
        xn_ref[...] = _rms(x_ref[...], g_ref[...], eps).astype(BF16)

    acc = jnp.dot(xn_ref[...], w_ref[...], preferred_element_type=F32)
    o_ref[...] = (acc * cs_ref[...]).astype(o_ref.dtype)


def rms_matmul(x, g, w, col_scale, out_dtype, *, eps=1e-6, tm=PROJ_TM, tn=PROJ_TN):
    m, k = x.shape
    n = w.shape[1]
    assert m % tm == 0 and n % tn == 0, (m, n, tm, tn)
    vmem = 2 * tm * k * 4 + tm * k * 2 + 2 * k * tn * 2 + 4 * tm * tn * 4 + (4 << 20)
    return pl.pallas_call(
        functools.partial(_rms_matmul_kernel, eps=eps),
        grid=(m // tm, n // tn),
        in_specs=[
            pl.BlockSpec((tm, k), lambda i, j: (i, 0)),
            pl.BlockSpec((1, k), lambda i, j: (0, 0)),
            pl.BlockSpec((k, tn), lambda i, j: (0, j)),
            pl.BlockSpec((1, tn), lambda i, j: (0, j)),
        ],
        out_specs=pl.BlockSpec((tm, tn), lambda i, j: (i, j)),
        out_shape=jax.ShapeDtypeStruct((m, n), out_dtype),
        scratch_shapes=[pltpu.VMEM((tm, k), BF16)],
        compiler_params=_cparams(("parallel", "arbitrary"), vmem),
        name="rms_matmul",
    )(x, g.reshape(1, k), w, col_scale.reshape(1, n))


def _proj_residual_kernel(*refs, n_in):
    a_refs, w_refs = refs[:n_in], refs[n_in:2 * n_in]
    h_ref, o_ref = refs[2 * n_in], refs[2 * n_in + 1]
    acc = h_ref[...]
    for a_ref, w_ref in zip(a_refs, w_refs):
        acc = acc + jnp.dot(a_ref[...], w_ref[...], preferred_element_type=F32)
    o_ref[...] = acc


def proj_residual(a_list, w_list, h, *, tm=PROJ_TM, tn=PROJ_TN):
    m, n = h.shape
    n_in = len(a_list)
    assert m % tm == 0 and n % tn == 0
    ks = [a.shape[1] for a in a_list]
    vmem = sum(2 * tm * k * 2 + 2 * k * tn * 2 for k in ks) + 6 * tm * tn * 4 + (4 << 20)
    in_specs = [pl.BlockSpec((tm, k), lambda i, j: (i, 0)) for k in ks]
    in_specs += [pl.BlockSpec((k, tn), lambda i, j: (0, j)) for k in ks]
    in_specs += [pl.BlockSpec((tm, tn), lambda i, j: (i, j))]
    return pl.pallas_call(
        functools.partial(_proj_residual_kernel, n_in=n_in),
        grid=(m // tm, n // tn),
        in_specs=in_specs,
        out_specs=pl.BlockSpec((tm, tn), lambda i, j: (i, j)),
        out_shape=jax.ShapeDtypeStruct((m, n), F32),
        compiler_params=_cparams(("parallel", "arbitrary"), vmem),
        name="proj_residual",
    )(*a_list, *w_list, h)


def _attn_stream(q, k_ref, v_ref, bias_fn, scratch, *, bq):
    s_bufs, p_bufs, a_bufs = scratch[0:2], scratch[2:4], scratch[4:6]
    m_scr, l_scr, acc_scr = scratch[6:9]
    rows, bk = s_bufs[0].shape
    assert bq == 2 * bk
    n_lane_chunks = bk // V7X_LANES
    m_scr[...] = jnp.full(m_scr.shape, MASK_VALUE, F32)
    l_scr[...] = jnp.zeros(l_scr.shape, F32)
    acc_scr[...] = jnp.zeros(acc_scr.shape, F32)
    a_bufs[1][...] = jnp.ones(a_bufs[1].shape, F32)
    p_bufs[1][...] = jnp.zeros(p_bufs[1].shape, BF16)

    def scores(j, slot):
        start = pl.multiple_of(j * bk, bk)
        s = lax.dot_general(q, k_ref[pl.ds(start, bk), :], (((1,), (1,)), ((), ())),
                            preferred_element_type=F32)
        s_bufs[slot][...] = s if bias_fn is None else s + bias_fn(start)

    def softmax(slot, diag_offset=None):
        s_buf, p_buf, a_buf = s_bufs[slot], p_bufs[slot], a_bufs[slot]
        for r0 in range(0, rows, SOFTMAX_ROWS):
            sl = slice(r0, r0 + SOFTMAX_ROWS)
            first_row = r0 % bq
            if diag_offset is not None and first_row + SOFTMAX_ROWS - 1 < diag_offset:
                continue
            s = s_buf[sl, :]
            if diag_offset is not None and first_row < diag_offset + bk - 1:
                row = lax.broadcasted_iota(jnp.int32, s.shape, 0) + first_row
                col = lax.broadcasted_iota(jnp.int32, s.shape, 1) + diag_offset
                s = jnp.where(col <= row, s, MASK_VALUE)
            chunks = [s[:, i * V7X_LANES:(i + 1) * V7X_LANES] for i in range(n_lane_chunks)]
            m_lane = chunks[0]
            for x in chunks[1:]:
                m_lane = jnp.maximum(m_lane, x)
            m_prev = m_scr[sl, :]
            m_next = jnp.maximum(m_prev, jnp.max(m_lane, axis=1, keepdims=True))
            alpha = jnp.exp2(m_prev - m_next)
            l_new = alpha * l_scr[sl, :]
            for i, x in enumerate(chunks):
                p = jnp.exp2(x - m_next)
                l_new = l_new + p
                p_buf[sl, i * V7X_LANES:(i + 1) * V7X_LANES] = p.astype(BF16)
            l_scr[sl, :] = l_new
            m_scr[sl, :] = m_next
            a_buf[sl, :] = alpha

    def pv(j, slot, first_live_row=0):
        start = pl.multiple_of(j * bk, bk)
        v = v_ref[pl.ds(start, bk), :]
        live = ([slice(0, rows)] if first_live_row == 0 else
                [slice(r0 + first_live_row, r0 + bq) for r0 in range(0, rows, bq)])
        for sl in live:
            acc_scr[sl, :] = acc_scr[sl, :] * a_bufs[slot][sl, :] + jnp.dot(
                p_bufs[slot][sl, :], v, preferred_element_type=F32)

    def result():
        return acc_scr[...] / jnp.sum(l_scr[...], axis=1, keepdims=True)

    return scores, softmax, pv, result


def _flash_attend(heads, qi, scratch, *, bq):
    bk = bq // 2
    per_head = len(scratch) // len(heads)
    streams = [_attn_stream(*head, scratch[t * per_head:(t + 1) * per_head], bq=bq)
               for t, head in enumerate(heads)]

    def each(stage, *args, **kwargs):
        for stream in streams:
            stream[stage](*args, **kwargs)

    SCORES, SOFTMAX, PV = 0, 1, 2

    def two_steps(j, diag):
        each(PV, jnp.maximum(j - 1, 0), 1)
        each(SOFTMAX, 0, 0 if diag else None)
        each(SCORES, j + 1, 1)
        each(PV, j, 0)
        each(SOFTMAX, 1, bk if diag else None)
        if not diag:
            each(SCORES, j + 2, 0)

    each(SCORES, 0, 0)

    def body(i, carry):
        two_steps(4 * i, False)
        two_steps(4 * i + 2, False)
        return carry

    lax.fori_loop(0, qi // 2, body, 0)

    @pl.when(qi % 2 == 1)
    def _():
        two_steps(2 * qi - 2, False)

    two_steps(2 * qi, True)
    each(PV, 2 * qi + 1, 1, first_live_row=bk)
    return [stream[3]() for stream in streams]


def _attn_scratch(rows, bq, n_heads=1):
    bk = bq // 2
    stat = lambda: pltpu.VMEM((rows, V7X_LANES), F32)
    s_buf = lambda: pltpu.VMEM((rows, bk), F32)
    p_buf = lambda: pltpu.VMEM((rows, bk), BF16)
    per_head = lambda: [s_buf(), s_buf(), p_buf(), p_buf()] + [stat() for _ in range(5)]
    return [buf for _ in range(n_heads) for buf in per_head()]


def _attn_vmem(seq, rows, bq, kv_width, n_heads=1, kv_buffers=2):
    resident = kv_buffers * seq * kv_width * 2 * n_heads
    buffers = n_heads * (2 * rows * (bq // 2) * (4 + 2) + 5 * rows * V7X_LANES * 4)
    temps = 2 * rows * (bq // 2) * 4
    return resident + buffers + temps + (6 << 20)


def _kv_buffers(n_heads):
    return 2 if n_heads == 1 else 1


def _resident(block_shape, index_map, n_heads):
    if _kv_buffers(n_heads) == 2:
        return pl.BlockSpec(block_shape, index_map)
    return pl.BlockSpec(block_shape, index_map, pipeline_mode=pl.Buffered(1))


def _side_cast_specs(weights, n_steps, step_index):
    in_specs, out_specs, out_shapes, vmem = [], [], [], 0
    for w in weights:
        rows, cols = w.shape
        block = next(b for b in range(16, rows + 1, 16) if rows % b == 0 and b * n_steps >= rows)
        last = rows // block - 1

        def index_map(*grid_idx, last=last):
            return (jnp.minimum(step_index(*grid_idx), last), 0)

        in_specs.append(pl.BlockSpec((block, cols), index_map))
        out_specs.append(pl.BlockSpec((block, cols), index_map))
        out_shapes.append(jax.ShapeDtypeStruct((rows, cols), BF16))
        vmem += 2 * block * cols * (4 + 2)
    return in_specs, out_specs, out_shapes, vmem


def _side_cast(in_refs, out_refs):
    for src, dst in zip(in_refs, out_refs):
        dst[...] = src[...].astype(dst.dtype)


def _diff_attn_kernel(slope_ref, q_ref, k_ref, v_ref, lq1_ref, lk1_ref, lq2_ref, lk2_ref,
                      subln_ref, *rest, bq, lambda_init, n_side):
    o_ref, scratch = rest[n_side], rest[2 * n_side + 1:]
    _side_cast(rest[:n_side], rest[n_side + 1:2 * n_side + 1])
    h, qi = pl.program_id(0), pl.program_id(1)
    q = q_ref[...]
    lane = lax.broadcasted_iota(jnp.int32, q.shape, 1)
    zero = jnp.zeros_like(q)
    qs = jnp.concatenate([jnp.where(lane < A_QK, q, zero), jnp.where(lane >= A_QK, q, zero)], axis=0)
    slope = slope_ref[h] * LOG2E
    t0 = qi * bq

    def bias_fn(start):
        kpos = lax.broadcasted_iota(jnp.int32, (1, bq // 2), 1) + (start - t0)
        return slope * kpos.astype(F32)

    o, = _flash_attend([(qs, k_ref, v_ref, bias_fn)], qi, scratch, bq=bq)
    lam = (jnp.exp(jnp.sum(lq1_ref[...] * lk1_ref[...], axis=1, keepdims=True))
           - jnp.exp(jnp.sum(lq2_ref[...] * lk2_ref[...], axis=1, keepdims=True)) + lambda_init)
    d = o[:bq] - lam * o[bq:]
    o_ref[...] = (_rms(d, subln_ref[...], 1e-5) * (1.0 - lambda_init)).astype(o_ref.dtype)


def diff_attention(qkv, lq1, lk1, lq2, lk2, subln, lambda_init, side_weights, *, bq=DIFF_ATTN_BLOCK):
    seq = qkv.shape[0]
    nq = seq // bq
    slopes = jnp.exp2(-8.0 * jnp.arange(1, A_HEADS + 1, dtype=F32) / A_HEADS)
    vec = lambda: pl.BlockSpec((1, A_QK), lambda h, i, s: (0, 0))
    side_in, side_out, side_shapes, side_vmem = _side_cast_specs(
        side_weights, A_HEADS * nq, lambda h, i, s: h * nq + i)
    grid_spec = pltpu.PrefetchScalarGridSpec(
        num_scalar_prefetch=1,
        grid=(A_HEADS, nq),
        in_specs=[
            pl.BlockSpec((bq, 128), lambda h, i, s: (i, h)),
            pl.BlockSpec((seq, 128), lambda h, i, s: (0, A_HEADS + h)),
            pl.BlockSpec((seq, 128), lambda h, i, s: (0, 2 * A_HEADS + h)),
            vec(), vec(), vec(), vec(),
            pl.BlockSpec((1, A_V), lambda h, i, s: (0, 0)),
        ] + side_in,
        out_specs=[pl.BlockSpec((bq, A_V), lambda h, i, s: (i, h))] + side_out,
        scratch_shapes=_attn_scratch(2 * bq, bq),
    )
    out = pl.pallas_call(
        functools.partial(_diff_attn_kernel, bq=bq, lambda_init=lambda_init, n_side=len(side_weights)),
        grid_spec=grid_spec,
        out_shape=[jax.ShapeDtypeStruct((seq, A_HEADS * A_V), BF16)] + side_shapes,
        compiler_params=_cparams(("arbitrary", "arbitrary"), _attn_vmem(seq, 2 * bq, bq, 256) + side_vmem),
        name="diff_attention",
    )(slopes, qkv, qkv, qkv, lq1.reshape(1, -1), lk1.reshape(1, -1), lq2.reshape(1, -1),
      lk2.reshape(1, -1), subln.reshape(1, -1), *side_weights)
    return out[0], out[1:]


def _mla_prep_kernel(c_ref, qn_ref, kvn_ref, wq1_ref, wq2_ref, wkn_ref, wv_ref, cos_ref, sin_ref,
                     q_ref, k_ref, v_ref, *, scale):
    c = c_ref[...]
    cqn = _rms(c[:, :B_LORA], qn_ref[...], 1e-6).astype(BF16)
    ckvn = _rms(c[:, B_LORA:2 * B_LORA], kvn_ref[...], 1e-6).astype(BF16)
    kp = c[:, 2 * B_LORA:]
    cos, sin = cos_ref[...], sin_ref[...]
    qa = jnp.dot(cqn, wq1_ref[...], preferred_element_type=F32)
    qb = jnp.dot(cqn, wq2_ref[...], preferred_element_type=F32)
    kn = jnp.dot(ckvn, wkn_ref[...], preferred_element_type=F32)
    v_ref[...] = jnp.dot(ckvn, wv_ref[...], preferred_element_type=F32).astype(v_ref.dtype)
    kr = (kp * cos + pltpu.roll(kp, 64, axis=1) * sin).astype(k_ref.dtype)
    for h in range(B_HEADS):
        lo, hi = 256 * h, 256 * h + 128
        q_ref[:, lo:hi] = (qa[:, lo:hi] * scale).astype(q_ref.dtype)
        pe = qa[:, hi:hi + 128] * cos + qb[:, 128 * h:128 * (h + 1)] * sin
        q_ref[:, hi:hi + 128] = (pe * scale).astype(q_ref.dtype)
        k_ref[:, lo:hi] = kn[:, 128 * h:128 * (h + 1)].astype(k_ref.dtype)
        k_ref[:, hi:hi + 128] = kr


def mla_prep(c, q_norm, kv_norm, wq1, wq2, wkn, wv, cos_t, sin_t, *, tm=PREP_TM):
    seq = c.shape[0]
    scale = (B_NOPE + B_ROPE) ** -0.5 * LOG2E
    full = lambda a: pl.BlockSpec(a.shape, lambda i: (0, 0))
    row = lambda w: pl.BlockSpec((tm, w), lambda i: (i, 0))
    qn, kvn = q_norm.reshape(1, -1), kv_norm.reshape(1, -1)
    weights = 2 * 2 * (wq1.size + wq2.size + wkn.size + wv.size)
    vmem = weights + 2 * tm * (1152 * 4 + 256 * 4 + 5120 * 2) + tm * 5120 * 4 * 2 + (6 << 20)
    return pl.pallas_call(
        functools.partial(_mla_prep_kernel, scale=scale),
        grid=(seq // tm,),
        in_specs=[row(c.shape[1]), full(qn), full(kvn), full(wq1), full(wq2), full(wkn), full(wv),
                  row(128), row(128)],
        out_specs=[row(2048), row(2048), row(1024)],
        out_shape=[jax.ShapeDtypeStruct((seq, 2048), BF16), jax.ShapeDtypeStruct((seq, 2048), BF16),
                   jax.ShapeDtypeStruct((seq, 1024), BF16)],
        compiler_params=_cparams(("parallel",), vmem),
        name="mla_prep",
    )(c, qn, kvn, wq1, wq2, wkn, wv, cos_t, sin_t)


def _mla_attn_kernel(q_ref, k_ref, v_ref, o_ref, *scratch, bq):
    heads = [(q_ref[:, 256 * t:256 * (t + 1)], k_ref.at[:, pl.ds(256 * t, 256)],
              v_ref.at[:, pl.ds(B_V * t, B_V)], None) for t in range(HEADS_PER_STEP)]
    for t, o in enumerate(_flash_attend(heads, pl.program_id(1), scratch, bq=bq)):
        o_ref[:, B_V * t:B_V * (t + 1)] = o.astype(o_ref.dtype)


def mla_attention(q, k, v, *, bq=ATTN_BLOCK, hp=HEADS_PER_STEP):
    seq = q.shape[0]
    return pl.pallas_call(
        functools.partial(_mla_attn_kernel, bq=bq),
        grid=(B_HEADS // hp, seq // bq),
        in_specs=[
            pl.BlockSpec((bq, 256 * hp), lambda h, i: (i, h)),
            _resident((seq, 256 * hp), lambda h, i: (0, h), hp),
            _resident((seq, B_V * hp), lambda h, i: (0, h), hp),
        ],
        out_specs=pl.BlockSpec((bq, B_V * hp), lambda h, i: (i, h)),
        out_shape=jax.ShapeDtypeStruct((seq, B_HEADS * B_V), BF16),
        scratch_shapes=_attn_scratch(bq, bq, hp),
        compiler_params=_cparams(("parallel", "arbitrary"), _attn_vmem(seq, bq, bq, 384, hp, _kv_buffers(hp))),
        name="mla_attention",
    )(q, k, v)


def _forget_gate_kernel(h_ref, g_ref, wf_ref, b_ref, cum_ref, carry_scr):
    @pl.when(pl.program_id(0) == 0)
    def _():
        carry_scr[...] = jnp.zeros(carry_scr.shape, F32)

    xn = _rms(h_ref[...], g_ref[...], 1e-6).astype(BF16)
    f = lax.dot_general(wf_ref[...], xn, (((1,), (1,)), ((), ())), preferred_element_type=F32)
    z = f + b_ref[...]
    log_f = jnp.minimum(z, 0.0) - jnp.log(1.0 + jnp.exp(-jnp.abs(z)))
    r = lax.broadcasted_iota(jnp.int32, (V7X_LANES, V7X_LANES), 0)
    c = lax.broadcasted_iota(jnp.int32, (V7X_LANES, V7X_LANES), 1)
    tri = (r <= c).astype(F32)
    carry = carry_scr[...]
    for j in range(log_f.shape[1] // V7X_LANES):
        sl = slice(j * V7X_LANES, (j + 1) * V7X_LANES)
        loc = jnp.dot(log_f[:, sl], tri, preferred_element_type=F32,
                      precision=lax.Precision.HIGHEST) + carry
        cum_ref[:, sl] = loc
        carry = jnp.broadcast_to(loc[:, V7X_LANES - 1:], carry.shape)
    carry_scr[...] = carry


def forget_gate_cumsum(h, g, wf_t, bias, *, tm=GATE_TM):
    seq, d = h.shape
    nh = wf_t.shape[0]
    return pl.pallas_call(
        _forget_gate_kernel,
        grid=(seq // tm,),
        in_specs=[
            pl.BlockSpec((tm, d), lambda i: (i, 0)),
            pl.BlockSpec((1, d), lambda i: (0, 0)),
            pl.BlockSpec((nh, d), lambda i: (0, 0)),
            pl.BlockSpec((nh, 1), lambda i: (0, 0)),
        ],
        out_specs=pl.BlockSpec((nh, tm), lambda i: (0, i)),
        out_shape=jax.ShapeDtypeStruct((nh, seq), F32),
        scratch_shapes=[pltpu.VMEM((nh, V7X_LANES), F32)],
        compiler_params=_cparams(("arbitrary",), 2 * tm * d * 4 + 3 * tm * d * 4 + (4 << 20)),
        name="forget_gate_cumsum",
    )(h, g.reshape(1, d), wf_t, bias.reshape(nh, 1))


def _fox_attn_kernel(q_ref, k_ref, v_ref, cum_ref, *rest, bq, n_side):
    o_ref, scratch = rest[n_side], rest[2 * n_side + 1:]
    _side_cast(rest[:n_side], rest[n_side + 1:2 * n_side + 1])
    qi = pl.program_id(1)
    t0 = pl.multiple_of(qi * bq, bq)

    def head(t):
        cum = cum_ref.at[t]
        c0 = cum[:, pl.ds(t0, V7X_LANES)][:, :1]

        def bias_fn(start):
            return (c0 - cum[:, pl.ds(start, bq // 2)]) * LOG2E

        lanes = pl.ds(C_DIM * t, C_DIM)
        return q_ref[:, C_DIM * t:C_DIM * (t + 1)], k_ref.at[:, lanes], v_ref.at[:, lanes], bias_fn

    heads = [head(t) for t in range(HEADS_PER_STEP)]
    for t, o in enumerate(_flash_attend(heads, qi, scratch, bq=bq)):
        o_ref[:, C_DIM * t:C_DIM * (t + 1)] = o.astype(o_ref.dtype)


def fox_attention(qkv, cum, side_weights, *, bq=ATTN_BLOCK, hp=HEADS_PER_STEP):
    seq = qkv.shape[0]
    nq = seq // bq
    cum3 = cum.reshape(C_HEADS, 1, seq)
    n_groups = C_HEADS // hp
    side_in, side_out, side_shapes, side_vmem = _side_cast_specs(
        side_weights, n_groups * nq, lambda h, i: h * nq + i)
    out = pl.pallas_call(
        functools.partial(_fox_attn_kernel, bq=bq, n_side=len(side_weights)),
        grid=(n_groups, nq),
        in_specs=[
            pl.BlockSpec((bq, C_DIM * hp), lambda h, i: (i, h)),
            _resident((seq, C_DIM * hp), lambda h, i: (0, n_groups + h), hp),
            _resident((seq, C_DIM * hp), lambda h, i: (0, 2 * n_groups + h), hp),
            pl.BlockSpec((hp, 1, seq), lambda h, i: (h, 0, 0)),
        ] + side_in,
        out_specs=[pl.BlockSpec((bq, C_DIM * hp), lambda h, i: (i, h))] + side_out,
        out_shape=[jax.ShapeDtypeStruct((seq, C_HEADS * C_DIM), BF16)] + side_shapes,
        scratch_shapes=_attn_scratch(bq, bq, hp),
        compiler_params=_cparams(("arbitrary", "arbitrary"),
                                 _attn_vmem(seq, bq, bq, 256, hp, _kv_buffers(hp)) + side_vmem),
        name="fox_attention",
    )(qkv, qkv, qkv, cum3, *side_weights)
    return out[0], out[1:]


def _swiglu_accumulate(xn, wg, wu, wd, acc_ref):
    gate = jnp.dot(xn, wg, preferred_element_type=F32)
    up = jnp.dot(xn, wu, preferred_element_type=F32)
    act = (gate * jax.nn.sigmoid(gate) * up).astype(BF16)
    acc_ref[...] += jnp.dot(act, wd, preferred_element_type=F32)


def _ffn_kernel(h_ref, g_ref, wg_ref, wu_ref, wd_ref, o_ref, xn_ref, acc_ref):
    f = pl.program_id(1)

    @pl.when(f == 0)
    def _():
        xn_ref[...] = _rms(h_ref[...], g_ref[...], 1e-6).astype(BF16)
        acc_ref[...] = jnp.zeros(acc_ref.shape, F32)

    _swiglu_accumulate(xn_ref[...], wg_ref[...], wu_ref[...], wd_ref[...], acc_ref)

    @pl.when(f == pl.num_programs(1) - 1)
    def _():
        o_ref[...] = h_ref[...] + acc_ref[...]


def _ffn_vmem(tm, d, tf):
    return 4 * tm * d * 4 + tm * d * 2 + tm * d * 4 + 3 * 2 * d * tf * 2 + 4 * tm * tf * 4 + (6 << 20)


def dense_ffn(h, g, wg, wu, wd, *, tm=FFN_TM, tf=FFN_TF):
    m, d = h.shape
    ff = wg.shape[1]
    assert m % tm == 0 and ff % tf == 0
    return pl.pallas_call(
        _ffn_kernel,
        grid=(m // tm, ff // tf),
        in_specs=[
            pl.BlockSpec((tm, d), lambda i, f: (i, 0)),
            pl.BlockSpec((1, d), lambda i, f: (0, 0)),
            pl.BlockSpec((d, tf), lambda i, f: (0, f)),
            pl.BlockSpec((d, tf), lambda i, f: (0, f)),
            pl.BlockSpec((tf, d), lambda i, f: (f, 0)),
        ],
        out_specs=pl.BlockSpec((tm, d), lambda i, f: (i, 0)),
        out_shape=jax.ShapeDtypeStruct((m, d), F32),
        scratch_shapes=[pltpu.VMEM((tm, d), BF16), pltpu.VMEM((tm, d), F32)],
        compiler_params=_cparams(("parallel", "arbitrary"), _ffn_vmem(tm, d, tf)),
        name="dense_ffn",
    )(h, g.reshape(1, d), wg, wu, wd)


def _router_kernel(h_ref, g_ref, wr_ref, idx_ref, gate_ref, rank_ref, cnt_ref, cnt_scr):
    @pl.when(pl.program_id(0) == 0)
    def _():
        cnt_scr[...] = jnp.zeros(cnt_scr.shape, F32)

    tm = h_ref.shape[0]
    xn = _rms(h_ref[...], g_ref[...], 1e-6)
    logits = jnp.dot(xn, wr_ref[...], preferred_element_type=F32, precision=lax.Precision.HIGHEST)
    lane = lax.broadcasted_iota(jnp.int32, logits.shape, 1)
    lane_f = lane.astype(F32)
    valid = lane < N_EXPERTS
    logits = jnp.where(valid, logits, MASK_VALUE)
    e = jnp.exp(logits - jnp.max(logits, axis=1, keepdims=True))
    probs = jnp.where(valid, e / jnp.sum(e, axis=1, keepdims=True), -1.0)
    p1 = jnp.max(probs, axis=1, keepdims=True)
    i1 = jnp.min(jnp.where(probs == p1, lane_f, float(V7X_LANES)), axis=1, keepdims=True)
    rest = jnp.where(lane_f == i1, -1.0, probs)
    p2 = jnp.max(rest, axis=1, keepdims=True)
    i2 = jnp.min(jnp.where(rest == p2, lane_f, float(V7X_LANES)), axis=1, keepdims=True)
    hot1 = (lane_f == i1).astype(F32)
    hot2 = (lane_f == i2).astype(F32)
    r = lax.broadcasted_iota(jnp.int32, (tm, tm), 0)
    c = lax.broadcasted_iota(jnp.int32, (tm, tm), 1)
    strict_lower = (c < r).astype(BF16)
    before = jnp.dot(strict_lower, (hot1 + hot2).astype(BF16), preferred_element_type=F32)
    before = before + cnt_scr[...]
    rank1 = jnp.sum(before * hot1, axis=1, keepdims=True)
    rank2 = jnp.sum(before * hot2, axis=1, keepdims=True)
    cnt_scr[...] += jnp.sum(hot1 + hot2, axis=0, keepdims=True)
    cnt_ref[...] = cnt_scr[...]
    denom = p1 + p2
    idx_ref[...] = jnp.where(lane == 0, i1, jnp.where(lane == 1, i2, 0.0)).astype(jnp.int32)
    gate_ref[...] = jnp.where(lane == 0, p1 / denom, jnp.where(lane == 1, p2 / denom, 0.0))
    rank_ref[...] = jnp.where(lane == 0, rank1, jnp.where(lane == 1, rank2, 0.0)).astype(jnp.int32)


def moe_router(h, g, w_router_padded, *, tm=ROUTER_TM):
    seq, d = h.shape
    out = lambda dt: jax.ShapeDtypeStruct((seq, V7X_LANES), dt)
    row = pl.BlockSpec((tm, V7X_LANES), lambda i: (i, 0))
    return pl.pallas_call(
        _router_kernel,
        grid=(seq // tm,),
        in_specs=[
            pl.BlockSpec((tm, d), lambda i: (i, 0)),
            pl.BlockSpec((1, d), lambda i: (0, 0)),
            pl.BlockSpec((d, V7X_LANES), lambda i: (0, 0)),
        ],
        out_specs=[row, row, row, pl.BlockSpec((1, V7X_LANES), lambda i: (0, 0))],
        out_shape=[out(jnp.int32), out(F32), out(jnp.int32),
                   jax.ShapeDtypeStruct((1, V7X_LANES), F32)],
        scratch_shapes=[pltpu.VMEM((1, V7X_LANES), F32)],
        compiler_params=_cparams(("arbitrary",), 5 * tm * d * 4 + 4 * tm * tm * 4 + (6 << 20)),
        name="moe_router",
    )(h, g.reshape(1, d), w_router_padded)


def _moe_ffn_kernel(te_ref, nv_ref, rows_ref, next_rows_ref, dest_ref, prev_dest_ref, h_hbm, g_ref,
                    wg_ref, wu_ref, wd_ref, y_hbm, x_buf, y_buf, xn_ref, acc_ref, in_sem, out_sem):
    i, f = pl.program_id(0), pl.program_id(1)
    n_tiles, n_f = pl.num_programs(0), pl.num_programs(1)
    live = i < nv_ref[0]
    tm = xn_ref.shape[0]
    n_rows = x_buf.shape[1]
    rows_per_step = n_rows // n_f
    slot = i % 2

    def gather_copy(idx_ref, r, dst_slot):
        return pltpu.make_async_copy(h_hbm.at[pl.ds(idx_ref[0, 0, r], 1)],
                                     x_buf.at[dst_slot, pl.ds(r, 1)], in_sem.at[dst_slot])

    def scatter_copy(idx_ref, r, src_slot):
        src_row = jnp.minimum(r, tm - 1)
        return pltpu.make_async_copy(y_buf.at[src_slot, pl.ds(src_row, 1)],
                                     y_hbm.at[pl.ds(idx_ref[0, 0, r], 1)], out_sem.at[src_slot])

    def wait_gather(dst_slot):
        pltpu.make_async_copy(h_hbm.at[pl.ds(0, n_rows)], x_buf.at[dst_slot], in_sem.at[dst_slot]).wait()

    def wait_scatter(src_slot):
        pltpu.make_async_copy(y_buf.at[src_slot], y_hbm.at[pl.ds(0, n_rows)], out_sem.at[src_slot]).wait()

    def step_rows():
        return [f * rows_per_step + k for k in range(rows_per_step)]

    def gather_next_tile_rows():
        for r in step_rows():
            gather_copy(next_rows_ref, r, 1 - slot).start()

    def scatter_prev_tile_rows():
        for r in step_rows():
            scatter_copy(prev_dest_ref, r, 1 - slot).start()

    @pl.when((f == 0) & (i == 0))
    def _():
        def body(r, carry):
            gather_copy(rows_ref, r, slot).start()
            return carry
        lax.fori_loop(0, n_rows, body, 0)

    @pl.when(f == 0)
    def _():
        wait_gather(slot)
        xn_ref[...] = _rms(x_buf[slot, :tm], g_ref[...], 1e-6).astype(BF16)
        acc_ref[...] = jnp.zeros(acc_ref.shape, F32)

    @pl.when(live & (i > 0))
    def _():
        _swiglu_accumulate(xn_ref[...], wg_ref[...], wu_ref[...], wd_ref[...], acc_ref)
        gather_next_tile_rows()
        scatter_prev_tile_rows()

    @pl.when(live & (i == 0))
    def _():
        _swiglu_accumulate(xn_ref[...], wg_ref[...], wu_ref[...], wd_ref[...], acc_ref)
        gather_next_tile_rows()

    @pl.when(jnp.logical_not(live))
    def _():
        gather_next_tile_rows()
        scatter_prev_tile_rows()

    @pl.when(f == n_f - 1)
    def _():
        @pl.when(i >= 2)
        def _():
            wait_scatter(slot)
        y_buf[slot, :tm] = acc_ref[...]

    @pl.when((f == n_f - 1) & (i == n_tiles - 1))
    def _():
        def body(r, carry):
            scatter_copy(dest_ref, r, slot).start()
            return carry
        lax.fori_loop(0, n_rows, body, 0)
        wait_gather(1 - slot)
        wait_scatter(1 - slot)
        wait_scatter(slot)


def _moe_tile_rows(tm, nf):
    return -(-tm // (8 * nf)) * 8 * nf


def moe_ffn(h, row_token, row_dest, n_y_rows, g, wg, wu, wd, tile_expert, n_live, *, tm=MOE_TM, tf=FFN_TF):
    d = h.shape[1]
    ff = wg.shape[2]
    nf = ff // tf
    n_tiles, _, n_rows = row_token.shape
    assert ff % tf == 0 and n_rows == _moe_tile_rows(tm, nf) and n_tiles >= 2

    def f_idx(i, f, nv):
        return jnp.where(i < nv[0], f, nf - 1)

    grid_spec = pltpu.PrefetchScalarGridSpec(
        num_scalar_prefetch=2,
        grid=(n_tiles, nf),
        in_specs=[
            pl.BlockSpec((1, 1, n_rows), lambda i, f, te, nv: (i, 0, 0), memory_space=pltpu.SMEM),
            pl.BlockSpec((1, 1, n_rows), lambda i, f, te, nv: (jnp.minimum(i + 1, n_tiles - 1), 0, 0),
                         memory_space=pltpu.SMEM),
            pl.BlockSpec((1, 1, n_rows), lambda i, f, te, nv: (i, 0, 0), memory_space=pltpu.SMEM),
            pl.BlockSpec((1, 1, n_rows), lambda i, f, te, nv: (jnp.maximum(i - 1, 0), 0, 0),
                         memory_space=pltpu.SMEM),
            pl.BlockSpec(memory_space=pl.ANY),
            pl.BlockSpec((1, d), lambda i, f, te, nv: (0, 0)),
            pl.BlockSpec((None, d, tf), lambda i, f, te, nv: (te[i], 0, f_idx(i, f, nv))),
            pl.BlockSpec((None, d, tf), lambda i, f, te, nv: (te[i], 0, f_idx(i, f, nv))),
            pl.BlockSpec((None, tf, d), lambda i, f, te, nv: (te[i], f_idx(i, f, nv), 0)),
        ],
        out_specs=pl.BlockSpec(memory_space=pl.ANY),
        scratch_shapes=[pltpu.VMEM((2, n_rows, d), F32), pltpu.VMEM((2, n_rows, d), F32),
                        pltpu.VMEM((tm, d), BF16), pltpu.VMEM((tm, d), F32),
                        pltpu.SemaphoreType.DMA((2,)), pltpu.SemaphoreType.DMA((2,))],
    )
    vmem = 4 * n_rows * d * 4 + tm * d * (2 + 4) + 3 * 2 * d * tf * 2 + 4 * tm * tf * 4 + (6 << 20)
    return pl.pallas_call(
        _moe_ffn_kernel,
        grid_spec=grid_spec,
        out_shape=jax.ShapeDtypeStruct((n_y_rows, d), F32),
        compiler_params=_cparams(("arbitrary", "arbitrary"), vmem),
        name="moe_ffn",
    )(tile_expert, n_live, row_token, row_token, row_dest, row_dest, h, g.reshape(1, d), wg, wu, wd)


def _combine_kernel(y0_ref, y1_ref, h_ref, gate_ref, fg_ref, o_ref):
    gate = gate_ref[...]
    out = h_ref[...] + gate[:, 0:1] * y0_ref[...] + gate[:, 1:2] * y1_ref[...]
    o_ref[...] = _rms(out, fg_ref[...], 1e-6)


def moe_combine(y, h, gates, final_gain, *, tc=COMBINE_ROWS):
    seq, d = h.shape
    n_tiles = seq // tc
    row = lambda w, off: pl.BlockSpec((tc, w), lambda i: (i + off, 0))
    return pl.pallas_call(
        _combine_kernel,
        grid=(n_tiles,),
        in_specs=[row(d, 0), row(d, n_tiles), row(d, 0), row(V7X_LANES, 0),
                  pl.BlockSpec((1, d), lambda i: (0, 0))],
        out_specs=row(d, 0),
        out_shape=jax.ShapeDtypeStruct((seq, d), F32),
        compiler_params=_cparams(("parallel",), 10 * tc * d * 4 + (4 << 20)),
        name="moe_combine",
    )(y, y, h, gates, final_gain.reshape(1, d))


def _swap_halves(w):
    half = w.shape[-1] // 2
    return jnp.concatenate([w[..., half:], w[..., :half]], axis=-1)


def _mla_weights(w_uq, w_ukv):
    lora = w_uq.shape[0]
    uq = w_uq.reshape(lora, B_HEADS, B_NOPE + B_ROPE)
    nope, pe = uq[..., :B_NOPE], uq[..., B_NOPE:]
    zpad = jnp.zeros((lora, B_HEADS, 128 - B_ROPE), w_uq.dtype)
    wq1 = jnp.concatenate([nope, pe, zpad], axis=-1).reshape(lora, B_HEADS * 256)
    wq2 = jnp.concatenate([_swap_halves(pe), zpad], axis=-1).reshape(lora, B_HEADS * 128)
    ukv = w_ukv.reshape(lora, B_HEADS, B_NOPE + B_V)
    wkn = ukv[..., :B_NOPE].reshape(lora, B_HEADS * B_NOPE)
    wv = ukv[..., B_NOPE:].reshape(lora, B_HEADS * B_V)
    return wq1.astype(BF16), wq2.astype(BF16), wkn.astype(BF16), wv.astype(BF16)


def _rope_tables(seq):
    inv = ROPE_THETA ** (-jnp.arange(0, B_ROPE, 2, dtype=F32) / B_ROPE)
    ang = jnp.arange(seq, dtype=F32)[:, None] * inv[None, :]
    cos, sin = jnp.cos(ang), jnp.sin(ang)
    zeros = jnp.zeros((seq, 128 - B_ROPE), F32)
    return (jnp.concatenate([cos, cos, zeros], axis=1), jnp.concatenate([-sin, sin, zeros], axis=1))


def _dispatch_plan(idx, rank, counts, tm, n_rows):
    seq = idx.shape[0]
    n_tiles = seq * TOP_K // tm + N_EXPERTS
    cnt = counts[0, :N_EXPERTS].astype(jnp.int32)
    tiles_per = (cnt + tm - 1) // tm
    tile_end = jnp.cumsum(tiles_per)
    row_start = (tile_end - tiles_per) * tm
    pos = (row_start[idx[:, :TOP_K]] + rank[:, :TOP_K]).reshape(-1)
    entry = (pos // tm) * n_rows + pos % tm
    tile_id = jnp.arange(n_tiles, dtype=jnp.int32)
    spare = (TOP_K * seq + (tile_id[:, None] % 2) * n_rows
             + jnp.arange(n_rows, dtype=jnp.int32)[None, :]).reshape(-1)
    pair = jnp.full((n_tiles * n_rows,), -1, jnp.int32).at[entry].set(
        jnp.arange(seq * TOP_K, dtype=jnp.int32))
    tok, slot = pair // TOP_K, pair % TOP_K
    row_token = jnp.where(pair >= 0, tok, 0)
    row_dest = jnp.where(pair >= 0, slot * seq + tok, spare)
    n_live = tile_end[-1]
    tile_expert = jnp.sum((tile_id[:, None] >= tile_end[None, :]).astype(jnp.int32), axis=1)
    last_expert = jnp.sum((n_live - 1 >= tile_end).astype(jnp.int32))
    tile_expert = jnp.where(tile_id < n_live, tile_expert, last_expert).astype(jnp.int32)
    shape3 = (n_tiles, 1, n_rows)
    return (row_token.reshape(shape3), row_dest.reshape(shape3), tile_expert,
            n_live.reshape(1).astype(jnp.int32), TOP_K * seq + 2 * n_rows)


def kernel(x, ev_attn_norm, ev_w_in, ev_q_norm, ev_w_uq, ev_kv_norm, ev_w_ukv, ev_lambda_q1, ev_lambda_k1, ev_lambda_q2, ev_lambda_k2, ev_subln, ev_w_out, ev_ffn_norm, ev_ffn_w_gate, ev_ffn_w_up, ev_ffn_w_down, od_attn_norm, od_w_in, od_forget_bias, od_w_out, od_ffn_norm, od_router, od_moe_w_gate, od_moe_w_up, od_moe_w_down, final_norm):
    batch, seq, d = x.shape
    assert batch == 1
    h = x.reshape(seq, d)

    lambda_init = 0.8 - 0.6 * math.exp(-0.3 * 0)
    w_in = ev_w_in[0]
    n_a = 2 * A_HEADS * 2 * A_QK + A_HEADS * A_V
    w_a = w_in[:, :n_a].astype(BF16)
    scale_a = jnp.concatenate([jnp.full((A_HEADS * 2 * A_QK,), A_QK ** -0.5 * LOG2E, F32),
                               jnp.ones((n_a - A_HEADS * 2 * A_QK,), F32)])
    w_b = jnp.concatenate([w_in[:, n_a:], _swap_halves(w_in[:, -B_ROPE:])], axis=1).astype(BF16)
    qkv_a = rms_matmul(h, ev_attn_norm[0], w_a, scale_a, BF16)
    c_b = rms_matmul(h, ev_attn_norm[0], w_b, jnp.ones((w_b.shape[1],), F32), F32, tn=w_b.shape[1])
    oa, (ffn_wg, ffn_wu, ffn_wd) = diff_attention(
        qkv_a, ev_lambda_q1[0], ev_lambda_k1[0], ev_lambda_q2[0], ev_lambda_k2[0], ev_subln[0],
        lambda_init, [ev_ffn_w_gate[0], ev_ffn_w_up[0], ev_ffn_w_down[0]])
    wq1, wq2, wkn, wv = _mla_weights(ev_w_uq[0], ev_w_ukv[0])
    cos_t, sin_t = _rope_tables(seq)
    q_b, k_b, v_b = mla_prep(c_b, ev_q_norm[0], ev_kv_norm[0], wq1, wq2, wkn, wv, cos_t, sin_t)
    ob = mla_attention(q_b, k_b, v_b)
    w_out = ev_w_out[0].astype(BF16)
    h = proj_residual([oa, ob], [w_out[:A_HEADS * A_V], w_out[A_HEADS * A_V:]], h)
    h = dense_ffn(h, ev_ffn_norm[0], ffn_wg, ffn_wu, ffn_wd)

    width = C_HEADS * C_DIM
    w_qkv = od_w_in[0][:, :3 * width].astype(BF16)
    scale_c = jnp.concatenate([jnp.full((width,), C_DIM ** -0.5 * LOG2E, F32), jnp.ones((2 * width,), F32)])
    qkv_c = rms_matmul(h, od_attn_norm[0], w_qkv, scale_c, BF16)
    wf_t = od_w_in[0][:, 3 * width:].T.astype(BF16)
    cum = forget_gate_cumsum(h, od_attn_norm[0], wf_t, od_forget_bias[0])
    n_e, _, ff = od_moe_w_gate[0].shape
    oc, (moe_wg, moe_wu, moe_wd) = fox_attention(qkv_c, cum, [
        od_moe_w_gate[0].reshape(n_e * d, ff), od_moe_w_up[0].reshape(n_e * d, ff),
        od_moe_w_down[0].reshape(n_e * ff, d)])
    h = proj_residual([oc], [od_w_out[0].astype(BF16)], h)

    w_router = jnp.zeros((d, V7X_LANES), F32).at[:, :N_EXPERTS].set(od_router[0])
    idx, gates, rank, counts = moe_router(h, od_ffn_norm[0], w_router)
    n_rows = _moe_tile_rows(MOE_TM, od_moe_w_gate.shape[-1] // FFN_TF)
    row_token, row_dest, tile_expert, n_live, n_y_rows = _dispatch_plan(idx, rank, counts, MOE_TM, n_rows)
    y = moe_ffn(h, row_token, row_dest, n_y_rows, od_ffn_norm[0], moe_wg.reshape(n_e, d, ff),
                moe_wu.reshape(n_e, d, ff), moe_wd.reshape(n_e, ff, d), tile_expert, n_live)
    out = moe_combine(y, h, gates, final_norm)
    return out.reshape(batch, seq, d)
```

```python
import functools
import math

import jax
import jax.numpy as jnp
from jax import lax
from jax.experimental import pallas as pl
from jax.experimental.pallas import tpu as pltpu

F32 = jnp.float32
BF16 = jnp.bfloat16

V7X_LANES = 128
V7X_VMEM_REQUEST_CAP = 60000 * 1024

A_HEADS, A_QK, A_V = 8, 64, 128
B_HEADS, B_LORA, B_NOPE, B_ROPE, B_V = 8, 512, 128, 64, 128
C_HEADS, C_DIM = 16, 128
N_EXPERTS, TOP_K = 8, 2
ROPE_THETA = 10000.0
MASK_VALUE = -1e30
LOG2E = math.log2(math.e)

ATTN_BLOCK = 1024
DIFF_ATTN_BLOCK = 1024
HEADS_PER_STEP = 1
SOFTMAX_ROWS = 128
PROJ_TM, PROJ_TN = 1024, 1024
FFN_TM, FFN_TF = 512, 512
MOE_TM = 512
COMBINE_ROWS = 256
PREP_TM = 512
GATE_TM = 1024
ROUTER_TM = 512


def _cparams(semantics, vmem_bytes):
    return pltpu.CompilerParams(
        dimension_semantics=semantics,
        vmem_limit_bytes=int(min(V7X_VMEM_REQUEST_CAP, vmem_bytes)))


def _rms(x, g, eps):
    return x * lax.rsqrt(jnp.mean(x * x, axis=-1, keepdims=True) + eps) * g


def _rms_matmul_kernel(x_ref, g_ref, w_ref, cs_ref, o_ref, xn_ref, *, eps):
    @pl.when(pl.program_id(1) == 0)
    def _():
        xn_ref[...] = _rms(x_ref[...], g_ref[...], eps).astype(BF16)

    acc = jnp.dot(xn_ref[...], w_ref[...], preferred_element_type=F32)
    o_ref[...] = (acc * cs_ref[...]).astype(o_ref.dtype)


def rms_matmul(x, g, w, col_scale, out_dtype, *, eps=1e-6, tm=PROJ_TM, tn=PROJ_TN):
    m, k = x.shape
    n = w.shape[1]
    assert m % tm == 0 and n % tn == 0, (m, n, tm, tn)
    vmem = 2 * tm * k * 4 + tm * k * 2 + 2 * k * tn * 2 + 4 * tm * tn * 4 + (4 << 20)
    return pl.pallas_call(
        functools.partial(_rms_matmul_kernel, eps=eps),
        grid=(m // tm, n // tn),
        in_specs=[
            pl.BlockSpec((tm, k), lambda i, j: (i, 0)),
            pl.BlockSpec((1, k), lambda i, j: (0, 0)),
            pl.BlockSpec((k, tn), lambda i, j: (0, j)),
            pl.BlockSpec((1, tn), lambda i, j: (0, j)),
        ],
        out_specs=pl.BlockSpec((tm, tn), lambda i, j: (i, j)),
        out_shape=jax.ShapeDtypeStruct((m, n), out_dtype),
        scratch_shapes=[pltpu.VMEM((tm, k), BF16)],
        compiler_params=_cparams(("parallel", "arbitrary"), vmem),
        name="rms_matmul",
    )(x, g.reshape(1, k), w, col_scale.reshape(1, n))


def _proj_residual_kernel(*refs, n_in):
    a_refs, w_refs = refs[:n_in], refs[n_in:2 * n_in]
    h_ref, o_ref = refs[2 * n_in], refs[2 * n_in + 1]
    acc = h_ref[...]
    for a_ref, w_ref in zip(a_refs, w_refs):
        acc = acc + jnp.dot(a_ref[...], w_ref[...], preferred_element_type=F32)
    o_ref[...] = acc


def proj_residual(a_list, w_list, h, *, tm=PROJ_TM, tn=PROJ_TN):
    m, n = h.shape
    n_in = len(a_list)
    assert m % tm == 0 and n % tn == 0
    ks = [a.shape[1] for a in a_list]
    vmem = sum(2 * tm * k * 2 + 2 * k * tn * 2 for k in ks) + 6 * tm * tn * 4 + (4 << 20)
    in_specs = [pl.BlockSpec((tm, k), lambda i, j: (i, 0)) for k in ks]
    in_specs += [pl.BlockSpec((k, tn), lambda i, j: (0, j)) for k in ks]
    in_specs += [pl.BlockSpec((tm, tn), lambda i, j: (i, j))]
    return pl.pallas_call(
        functools.partial(_proj_residual_kernel, n_in=n_in),
        grid=(m // tm, n // tn),
        in_specs=in_specs,
        out_specs=pl.BlockSpec((tm, tn), lambda i, j: (i, j)),
        out_shape=jax.ShapeDtypeStruct((m, n), F32),
        compiler_params=_cparams(("parallel", "arbitrary"), vmem),
        name="proj_residual",
    )(*a_list, *w_list, h)


def _attn_stream(q, k_ref, v_ref, bias_fn, scratch, *, bq):
    s_bufs, p_bufs, a_bufs = scratch[0:2], scratch[2:4], scratch[4:6]
    m_scr, l_scr, acc_scr = scratch[6:9]
    rows, bk = s_bufs[0].shape
    assert bq == 2 * bk
    n_lane_chunks = bk // V7X_LANES
    m_scr[...] = jnp.full(m_scr.shape, MASK_VALUE, F32)
    l_scr[...] = jnp.zeros(l_scr.shape, F32)
    acc_scr[...] = jnp.zeros(acc_scr.shape, F32)
    a_bufs[1][...] = jnp.ones(a_bufs[1].shape, F32)
    p_bufs[1][...] = jnp.zeros(p_bufs[1].shape, BF16)

    def scores(j, slot):
        start = pl.multiple_of(j * bk, bk)
        s = lax.dot_general(q, k_ref[pl.ds(start, bk), :], (((1,), (1,)), ((), ())),
                            preferred_element_type=F32)
        s_bufs[slot][...] = s if bias_fn is None else s + bias_fn(start)

    def softmax(slot, diag_offset=None):
        s_buf, p_buf, a_buf = s_bufs[slot], p_bufs[slot], a_bufs[slot]
        for r0 in range(0, rows, SOFTMAX_ROWS):
            sl = slice(r0, r0 + SOFTMAX_ROWS)
            first_row = r0 % bq
            if diag_offset is not None and first_row + SOFTMAX_ROWS - 1 < diag_offset:
                continue
            s = s_buf[sl, :]
            if diag_offset is not None and first_row < diag_offset + bk - 1:
                row = lax.broadcasted_iota(jnp.int32, s.shape, 0) + first_row
                col = lax.broadcasted_iota(jnp.int32, s.shape, 1) + diag_offset
                s = jnp.where(col <= row, s, MASK_VALUE)
            chunks = [s[:, i * V7X_LANES:(i + 1) * V7X_LANES] for i in range(n_lane_chunks)]
            m_lane = chunks[0]
            for x in chunks[1:]:
                m_lane = jnp.maximum(m_lane, x)
            m_prev = m_scr[sl, :]
            m_next = jnp.maximum(m_prev, jnp.max(m_lane, axis=1, keepdims=True))
            alpha = jnp.exp2(m_prev - m_next)
            l_new = alpha * l_scr[sl, :]
            for i, x in enumerate(chunks):
                p = jnp.exp2(x - m_next)
                l_new = l_new + p
                p_buf[sl, i * V7X_LANES:(i + 1) * V7X_LANES] = p.astype(BF16)
            l_scr[sl, :] = l_new
            m_scr[sl, :] = m_next
            a_buf[sl, :] = alpha

    def pv(j, slot, first_live_row=0):
        start = pl.multiple_of(j * bk, bk)
        v = v_ref[pl.ds(start, bk), :]
        live = ([slice(0, rows)] if first_live_row == 0 else
                [slice(r0 + first_live_row, r0 + bq) for r0 in range(0, rows, bq)])
        for sl in live:
            acc_scr[sl, :] = acc_scr[sl, :] * a_bufs[slot][sl, :] + jnp.dot(
                p_bufs[slot][sl, :], v, preferred_element_type=F32)

    def result():
        return acc_scr[...] / jnp.sum(l_scr[...], axis=1, keepdims=True)

    return scores, softmax, pv, result


def _flash_attend(heads, qi, scratch, *, bq):
    bk = bq // 2
    per_head = len(scratch) // len(heads)
    streams = [_attn_stream(*head, scratch[t * per_head:(t + 1) * per_head], bq=bq)
               for t, head in enumerate(heads)]

    def each(stage, *args, **kwargs):
        for stream in streams:
            stream[stage](*args, **kwargs)

    SCORES, SOFTMAX, PV = 0, 1, 2

    def two_steps(j, diag):
        each(PV, jnp.maximum(j - 1, 0), 1)
        each(SOFTMAX, 0, 0 if diag else None)
        each(SCORES, j + 1, 1)
        each(PV, j, 0)
        each(SOFTMAX, 1, bk if diag else None)
        if not diag:
            each(SCORES, j + 2, 0)

    each(SCORES, 0, 0)

    def body(i, carry):
        two_steps(4 * i, False)
        two_steps(4 * i + 2, False)
        return carry

    lax.fori_loop(0, qi // 2, body, 0)

    @pl.when(qi % 2 == 1)
    def _():
        two_steps(2 * qi - 2, False)

    two_steps(2 * qi, True)
    each(PV, 2 * qi + 1, 1, first_live_row=bk)
    return [stream[3]() for stream in streams]


def _attn_scratch(rows, bq, n_heads=1):
    bk = bq // 2
    stat = lambda: pltpu.VMEM((rows, V7X_LANES), F32)
    s_buf = lambda: pltpu.VMEM((rows, bk), F32)
    p_buf = lambda: pltpu.VMEM((rows, bk), BF16)
    per_head = lambda: [s_buf(), s_buf(), p_buf(), p_buf()] + [stat() for _ in range(5)]
    return [buf for _ in range(n_heads) for buf in per_head()]


def _attn_vmem(seq, rows, bq, kv_width, n_heads=1, kv_buffers=2):
    resident = kv_buffers * seq * kv_width * 2 * n_heads
    buffers = n_heads * (2 * rows * (bq // 2) * (4 + 2) + 5 * rows * V7X_LANES * 4)
    temps = 2 * rows * (bq // 2) * 4
    return resident + buffers + temps + (6 << 20)


def _kv_buffers(n_heads):
    return 2 if n_heads == 1 else 1


def _resident(block_shape, index_map, n_heads):
    if _kv_buffers(n_heads) == 2:
        return pl.BlockSpec(block_shape, index_map)
    return pl.BlockSpec(block_shape, index_map, pipeline_mode=pl.Buffered(1))


def _side_cast_specs(weights, n_steps, step_index):
    in_specs, out_specs, out_shapes, vmem = [], [], [], 0
    for w, col_block in weights:
        rows, cols = w.shape
        block = next(b for b in range(16, rows + 1, 16) if rows % b == 0 and b * n_steps >= rows)
        last = rows // block - 1

        def row_block(*grid_idx, last=last):
            return jnp.minimum(step_index(*grid_idx), last)

        in_specs.append(pl.BlockSpec((block, cols), lambda *g, rb=row_block: (rb(*g), 0)))
        if col_block is None:
            out_specs.append(pl.BlockSpec((block, cols), lambda *g, rb=row_block: (rb(*g), 0)))
            out_shapes.append(jax.ShapeDtypeStruct((rows, cols), BF16))
        else:
            out_specs.append(pl.BlockSpec((cols // col_block, block, col_block),
                                          lambda *g, rb=row_block: (0, rb(*g), 0)))
            out_shapes.append(jax.ShapeDtypeStruct((cols // col_block, rows, col_block), BF16))
        vmem += 2 * block * cols * (4 + 2)
    return in_specs, out_specs, out_shapes, vmem


def _side_cast(in_refs, out_refs):
    for src, dst in zip(in_refs, out_refs):
        if len(dst.shape) == 2:
            dst[...] = src[...].astype(dst.dtype)
        else:
            width = dst.shape[2]
            for c in range(dst.shape[0]):
                dst[c] = src[:, c * width:(c + 1) * width].astype(dst.dtype)


def _diff_attn_kernel(slope_ref, q_ref, k_ref, v_ref, lq1_ref, lk1_ref, lq2_ref, lk2_ref,
                      subln_ref, *rest, bq, lambda_init, n_side):
    o_ref, scratch = rest[n_side], rest[2 * n_side + 1:]
    _side_cast(rest[:n_side], rest[n_side + 1:2 * n_side + 1])
    h, qi = pl.program_id(0), pl.program_id(1)
    q = q_ref[...]
    lane = lax.broadcasted_iota(jnp.int32, q.shape, 1)
    zero = jnp.zeros_like(q)
    qs = jnp.concatenate([jnp.where(lane < A_QK, q, zero), jnp.where(lane >= A_QK, q, zero)], axis=0)
    slope = slope_ref[h] * LOG2E
    t0 = qi * bq

    def bias_fn(start):
        kpos = lax.broadcasted_iota(jnp.int32, (1, bq // 2), 1) + (start - t0)
        return slope * kpos.astype(F32)

    o, = _flash_attend([(qs, k_ref, v_ref, bias_fn)], qi, scratch, bq=bq)
    lam = (jnp.exp(jnp.sum(lq1_ref[...] * lk1_ref[...], axis=1, keepdims=True))
           - jnp.exp(jnp.sum(lq2_ref[...] * lk2_ref[...], axis=1, keepdims=True)) + lambda_init)
    d = o[:bq] - lam * o[bq:]
    o_ref[...] = (_rms(d, subln_ref[...], 1e-5) * (1.0 - lambda_init)).astype(o_ref.dtype)


def diff_attention(qkv, lq1, lk1, lq2, lk2, subln, lambda_init, side_weights, *, bq=DIFF_ATTN_BLOCK):
    seq = qkv.shape[0]
    nq = seq // bq
    slopes = jnp.exp2(-8.0 * jnp.arange(1, A_HEADS + 1, dtype=F32) / A_HEADS)
    vec = lambda: pl.BlockSpec((1, A_QK), lambda h, i, s: (0, 0))
    side_in, side_out, side_shapes, side_vmem = _side_cast_specs(
        side_weights, A_HEADS * nq, lambda h, i, s: h * nq + i)
    grid_spec = pltpu.PrefetchScalarGridSpec(
        num_scalar_prefetch=1,
        grid=(A_HEADS, nq),
        in_specs=[
            pl.BlockSpec((bq, 128), lambda h, i, s: (i, h)),
            pl.BlockSpec((seq, 128), lambda h, i, s: (0, A_HEADS + h)),
            pl.BlockSpec((seq, 128), lambda h, i, s: (0, 2 * A_HEADS + h)),
            vec(), vec(), vec(), vec(),
            pl.BlockSpec((1, A_V), lambda h, i, s: (0, 0)),
        ] + side_in,
        out_specs=[pl.BlockSpec((bq, A_V), lambda h, i, s: (i, h))] + side_out,
        scratch_shapes=_attn_scratch(2 * bq, bq),
    )
    out = pl.pallas_call(
        functools.partial(_diff_attn_kernel, bq=bq, lambda_init=lambda_init, n_side=len(side_weights)),
        grid_spec=grid_spec,
        out_shape=[jax.ShapeDtypeStruct((seq, A_HEADS * A_V), BF16)] + side_shapes,
        compiler_params=_cparams(("arbitrary", "arbitrary"), _attn_vmem(seq, 2 * bq, bq, 256) + side_vmem),
        name="diff_attention",
    )(slopes, qkv, qkv, qkv, lq1.reshape(1, -1), lk1.reshape(1, -1), lq2.reshape(1, -1),
      lk2.reshape(1, -1), subln.reshape(1, -1), *[w for w, _ in side_weights])
    return out[0], out[1:]


def _mla_prep_kernel(c_ref, qn_ref, kvn_ref, wq1_ref, wq2_ref, wkn_ref, wv_ref, cos_ref, sin_ref,
                     q_ref, k_ref, v_ref, *, scale):
    c = c_ref[...]
    cqn = _rms(c[:, :B_LORA], qn_ref[...], 1e-6).astype(BF16)
    ckvn = _rms(c[:, B_LORA:2 * B_LORA], kvn_ref[...], 1e-6).astype(BF16)
    kp = c[:, 2 * B_LORA:]
    cos, sin = cos_ref[...], sin_ref[...]
    qa = jnp.dot(cqn, wq1_ref[...], preferred_element_type=F32)
    qb = jnp.dot(cqn, wq2_ref[...], preferred_element_type=F32)
    kn = jnp.dot(ckvn, wkn_ref[...], preferred_element_type=F32)
    v_ref[...] = jnp.dot(ckvn, wv_ref[...], preferred_element_type=F32).astype(v_ref.dtype)
    kr = (kp * cos + pltpu.roll(kp, 64, axis=1) * sin).astype(k_ref.dtype)
    for h in range(B_HEADS):
        lo, hi = 256 * h, 256 * h + 128
        q_ref[:, lo:hi] = (qa[:, lo:hi] * scale).astype(q_ref.dtype)
        pe = qa[:, hi:hi + 128] * cos + qb[:, 128 * h:128 * (h + 1)] * sin
        q_ref[:, hi:hi + 128] = (pe * scale).astype(q_ref.dtype)
        k_ref[:, lo:hi] = kn[:, 128 * h:128 * (h + 1)].astype(k_ref.dtype)
        k_ref[:, hi:hi + 128] = kr


def mla_prep(c, q_norm, kv_norm, wq1, wq2, wkn, wv, cos_t, sin_t, *, tm=PREP_TM):
    seq = c.shape[0]
    scale = (B_NOPE + B_ROPE) ** -0.5 * LOG2E
    full = lambda a: pl.BlockSpec(a.shape, lambda i: (0, 0))
    row = lambda w: pl.BlockSpec((tm, w), lambda i: (i, 0))
    qn, kvn = q_norm.reshape(1, -1), kv_norm.reshape(1, -1)
    weights = 2 * 2 * (wq1.size + wq2.size + wkn.size + wv.size)
    vmem = weights + 2 * tm * (1152 * 4 + 256 * 4 + 5120 * 2) + tm * 5120 * 4 * 2 + (6 << 20)
    return pl.pallas_call(
        functools.partial(_mla_prep_kernel, scale=scale),
        grid=(seq // tm,),
        in_specs=[row(c.shape[1]), full(qn), full(kvn), full(wq1), full(wq2), full(wkn), full(wv),
                  row(128), row(128)],
        out_specs=[row(2048), row(2048), row(1024)],
        out_shape=[jax.ShapeDtypeStruct((seq, 2048), BF16), jax.ShapeDtypeStruct((seq, 2048), BF16),
                   jax.ShapeDtypeStruct((seq, 1024), BF16)],
        compiler_params=_cparams(("parallel",), vmem),
        name="mla_prep",
    )(c, qn, kvn, wq1, wq2, wkn, wv, cos_t, sin_t)


def _mla_attn_kernel(q_ref, k_ref, v_ref, o_ref, *scratch, bq):
    heads = [(q_ref[:, 256 * t:256 * (t + 1)], k_ref.at[:, pl.ds(256 * t, 256)],
              v_ref.at[:, pl.ds(B_V * t, B_V)], None) for t in range(HEADS_PER_STEP)]
    for t, o in enumerate(_flash_attend(heads, pl.program_id(1), scratch, bq=bq)):
        o_ref[:, B_V * t:B_V * (t + 1)] = o.astype(o_ref.dtype)


def mla_attention(q, k, v, *, bq=ATTN_BLOCK, hp=HEADS_PER_STEP):
    seq = q.shape[0]
    return pl.pallas_call(
        functools.partial(_mla_attn_kernel, bq=bq),
        grid=(B_HEADS // hp, seq // bq),
        in_specs=[
            pl.BlockSpec((bq, 256 * hp), lambda h, i: (i, h)),
            _resident((seq, 256 * hp), lambda h, i: (0, h), hp),
            _resident((seq, B_V * hp), lambda h, i: (0, h), hp),
        ],
        out_specs=pl.BlockSpec((bq, B_V * hp), lambda h, i: (i, h)),
        out_shape=jax.ShapeDtypeStruct((seq, B_HEADS * B_V), BF16),
        scratch_shapes=_attn_scratch(bq, bq, hp),
        compiler_params=_cparams(("parallel", "arbitrary"), _attn_vmem(seq, bq, bq, 384, hp, _kv_buffers(hp))),
        name="mla_attention",
    )(q, k, v)


def _forget_gate_kernel(h_ref, g_ref, wf_ref, b_ref, cum_ref, carry_scr):
    @pl.when(pl.program_id(0) == 0)
    def _():
        carry_scr[...] = jnp.zeros(carry_scr.shape, F32)

    xn = _rms(h_ref[...], g_ref[...], 1e-6).astype(BF16)
    f = lax.dot_general(wf_ref[...], xn, (((1,), (1,)), ((), ())), preferred_element_type=F32)
    z = f + b_ref[...]
    log_f = jnp.minimum(z, 0.0) - jnp.log(1.0 + jnp.exp(-jnp.abs(z)))
    r = lax.broadcasted_iota(jnp.int32, (V7X_LANES, V7X_LANES), 0)
    c = lax.broadcasted_iota(jnp.int32, (V7X_LANES, V7X_LANES), 1)
    tri = (r <= c).astype(F32)
    carry = carry_scr[...]
    for j in range(log_f.shape[1] // V7X_LANES):
        sl = slice(j * V7X_LANES, (j + 1) * V7X_LANES)
        loc = jnp.dot(log_f[:, sl], tri, preferred_element_type=F32,
                      precision=lax.Precision.HIGHEST) + carry
        cum_ref[:, sl] = loc
        carry = jnp.broadcast_to(loc[:, V7X_LANES - 1:], carry.shape)
    carry_scr[...] = carry


def forget_gate_cumsum(h, g, wf_t, bias, *, tm=GATE_TM):
    seq, d = h.shape
    nh = wf_t.shape[0]
    return pl.pallas_call(
        _forget_gate_kernel,
        grid=(seq // tm,),
        in_specs=[
            pl.BlockSpec((tm, d), lambda i: (i, 0)),
            pl.BlockSpec((1, d), lambda i: (0, 0)),
            pl.BlockSpec((nh, d), lambda i: (0, 0)),
            pl.BlockSpec((nh, 1), lambda i: (0, 0)),
        ],
        out_specs=pl.BlockSpec((nh, tm), lambda i: (0, i)),
        out_shape=jax.ShapeDtypeStruct((nh, seq), F32),
        scratch_shapes=[pltpu.VMEM((nh, V7X_LANES), F32)],
        compiler_params=_cparams(("arbitrary",), 2 * tm * d * 4 + 3 * tm * d * 4 + (4 << 20)),
        name="forget_gate_cumsum",
    )(h, g.reshape(1, d), wf_t, bias.reshape(nh, 1))


def _fox_attn_kernel(q_ref, k_ref, v_ref, cum_ref, *rest, bq, n_side):
    o_ref, scratch = rest[n_side], rest[2 * n_side + 1:]
    _side_cast(rest[:n_side], rest[n_side + 1:2 * n_side + 1])
    qi = pl.program_id(1)
    t0 = pl.multiple_of(qi * bq, bq)

    def head(t):
        cum = cum_ref.at[t]
        c0 = cum[:, pl.ds(t0, V7X_LANES)][:, :1]

        def bias_fn(start):
            return (c0 - cum[:, pl.ds(start, bq // 2)]) * LOG2E

        lanes = pl.ds(C_DIM * t, C_DIM)
        return q_ref[:, C_DIM * t:C_DIM * (t + 1)], k_ref.at[:, lanes], v_ref.at[:, lanes], bias_fn

    heads = [head(t) for t in range(HEADS_PER_STEP)]
    for t, o in enumerate(_flash_attend(heads, qi, scratch, bq=bq)):
        o_ref[:, C_DIM * t:C_DIM * (t + 1)] = o.astype(o_ref.dtype)


def fox_attention(qkv, cum, side_weights, *, bq=ATTN_BLOCK, hp=HEADS_PER_STEP):
    seq = qkv.shape[0]
    nq = seq // bq
    cum3 = cum.reshape(C_HEADS, 1, seq)
    n_groups = C_HEADS // hp
    side_in, side_out, side_shapes, side_vmem = _side_cast_specs(
        side_weights, n_groups * nq, lambda h, i: h * nq + i)
    out = pl.pallas_call(
        functools.partial(_fox_attn_kernel, bq=bq, n_side=len(side_weights)),
        grid=(n_groups, nq),
        in_specs=[
            pl.BlockSpec((bq, C_DIM * hp), lambda h, i: (i, h)),
            _resident((seq, C_DIM * hp), lambda h, i: (0, n_groups + h), hp),
            _resident((seq, C_DIM * hp), lambda h, i: (0, 2 * n_groups + h), hp),
            pl.BlockSpec((hp, 1, seq), lambda h, i: (h, 0, 0)),
        ] + side_in,
        out_specs=[pl.BlockSpec((bq, C_DIM * hp), lambda h, i: (i, h))] + side_out,
        out_shape=[jax.ShapeDtypeStruct((seq, C_HEADS * C_DIM), BF16)] + side_shapes,
        scratch_shapes=_attn_scratch(bq, bq, hp),
        compiler_params=_cparams(("arbitrary", "arbitrary"),
                                 _attn_vmem(seq, bq, bq, 256, hp, _kv_buffers(hp)) + side_vmem),
        name="fox_attention",
    )(qkv, qkv, qkv, cum3, *[w for w, _ in side_weights])
    return out[0], out[1:]


def _swiglu_accumulate(xn, wg, wu, wd, acc_ref):
    gate = jnp.dot(xn, wg, preferred_element_type=F32)
    up = jnp.dot(xn, wu, preferred_element_type=F32)
    act = (gate * jax.nn.sigmoid(gate) * up).astype(BF16)
    acc_ref[...] += jnp.dot(act, wd, preferred_element_type=F32)


def _ffn_kernel(h_ref, g_ref, wg_ref, wu_ref, wd_ref, o_ref, xn_ref, acc_ref):
    f = pl.program_id(1)

    @pl.when(f == 0)
    def _():
        xn_ref[...] = _rms(h_ref[...], g_ref[...], 1e-6).astype(BF16)
        acc_ref[...] = jnp.zeros(acc_ref.shape, F32)

    _swiglu_accumulate(xn_ref[...], wg_ref[...], wu_ref[...], wd_ref[...], acc_ref)

    @pl.when(f == pl.num_programs(1) - 1)
    def _():
        o_ref[...] = h_ref[...] + acc_ref[...]


def _ffn_vmem(tm, d, tf):
    return 4 * tm * d * 4 + tm * d * 2 + tm * d * 4 + 3 * 2 * d * tf * 2 + 4 * tm * tf * 4 + (6 << 20)


def dense_ffn(h, g, wg, wu, wd, *, tm=FFN_TM):
    m, d = h.shape
    n_f, _, tf = wg.shape
    assert m % tm == 0 and wd.shape[0] == n_f * tf
    return pl.pallas_call(
        _ffn_kernel,
        grid=(m // tm, n_f),
        in_specs=[
            pl.BlockSpec((tm, d), lambda i, f: (i, 0)),
            pl.BlockSpec((1, d), lambda i, f: (0, 0)),
            pl.BlockSpec((None, d, tf), lambda i, f: (f, 0, 0)),
            pl.BlockSpec((None, d, tf), lambda i, f: (f, 0, 0)),
            pl.BlockSpec((tf, d), lambda i, f: (f, 0)),
        ],
        out_specs=pl.BlockSpec((tm, d), lambda i, f: (i, 0)),
        out_shape=jax.ShapeDtypeStruct((m, d), F32),
        scratch_shapes=[pltpu.VMEM((tm, d), BF16), pltpu.VMEM((tm, d), F32)],
        compiler_params=_cparams(("parallel", "arbitrary"), _ffn_vmem(tm, d, tf)),
        name="dense_ffn",
    )(h, g.reshape(1, d), wg, wu, wd)


def _router_kernel(h_ref, g_ref, wr_ref, idx_ref, gate_ref, rank_ref, cnt_ref, cnt_scr):
    @pl.when(pl.program_id(0) == 0)
    def _():
        cnt_scr[...] = jnp.zeros(cnt_scr.shape, F32)

    tm = h_ref.shape[0]
    xn = _rms(h_ref[...], g_ref[...], 1e-6)
    logits = jnp.dot(xn, wr_ref[...], preferred_element_type=F32, precision=lax.Precision.HIGHEST)
    lane = lax.broadcasted_iota(jnp.int32, logits.shape, 1)
    lane_f = lane.astype(F32)
    valid = lane < N_EXPERTS
    logits = jnp.where(valid, logits, MASK_VALUE)
    e = jnp.exp(logits - jnp.max(logits, axis=1, keepdims=True))
    probs = jnp.where(valid, e / jnp.sum(e, axis=1, keepdims=True), -1.0)
    p1 = jnp.max(probs, axis=1, keepdims=True)
    i1 = jnp.min(jnp.where(probs == p1, lane_f, float(V7X_LANES)), axis=1, keepdims=True)
    rest = jnp.where(lane_f == i1, -1.0, probs)
    p2 = jnp.max(rest, axis=1, keepdims=True)
    i2 = jnp.min(jnp.where(rest == p2, lane_f, float(V7X_LANES)), axis=1, keepdims=True)
    hot1 = (lane_f == i1).astype(F32)
    hot2 = (lane_f == i2).astype(F32)
    r = lax.broadcasted_iota(jnp.int32, (tm, tm), 0)
    c = lax.broadcasted_iota(jnp.int32, (tm, tm), 1)
    strict_lower = (c < r).astype(BF16)
    before = jnp.dot(strict_lower, (hot1 + hot2).astype(BF16), preferred_element_type=F32)
    before = before + cnt_scr[...]
    rank1 = jnp.sum(before * hot1, axis=1, keepdims=True)
    rank2 = jnp.sum(before * hot2, axis=1, keepdims=True)
    cnt_scr[...] += jnp.sum(hot1 + hot2, axis=0, keepdims=True)
    cnt_ref[...] = cnt_scr[...]
    denom = p1 + p2
    idx_ref[...] = jnp.where(lane == 0, i1, jnp.where(lane == 1, i2, 0.0)).astype(jnp.int32)
    gate_ref[...] = jnp.where(lane == 0, p1 / denom, jnp.where(lane == 1, p2 / denom, 0.0))
    rank_ref[...] = jnp.where(lane == 0, rank1, jnp.where(lane == 1, rank2, 0.0)).astype(jnp.int32)


def moe_router(h, g, w_router_padded, *, tm=ROUTER_TM):
    seq, d = h.shape
    out = lambda dt: jax.ShapeDtypeStruct((seq, V7X_LANES), dt)
    row = pl.BlockSpec((tm, V7X_LANES), lambda i: (i, 0))
    return pl.pallas_call(
        _router_kernel,
        grid=(seq // tm,),
        in_specs=[
            pl.BlockSpec((tm, d), lambda i: (i, 0)),
            pl.BlockSpec((1, d), lambda i: (0, 0)),
            pl.BlockSpec((d, V7X_LANES), lambda i: (0, 0)),
        ],
        out_specs=[row, row, row, pl.BlockSpec((1, V7X_LANES), lambda i: (0, 0))],
        out_shape=[out(jnp.int32), out(F32), out(jnp.int32),
                   jax.ShapeDtypeStruct((1, V7X_LANES), F32)],
        scratch_shapes=[pltpu.VMEM((1, V7X_LANES), F32)],
        compiler_params=_cparams(("arbitrary",), 5 * tm * d * 4 + 4 * tm * tm * 4 + (6 << 20)),
        name="moe_router",
    )(h, g.reshape(1, d), w_router_padded)


def _moe_ffn_kernel(te_ref, nv_ref, rows_ref, next_rows_ref, dest_ref, prev_dest_ref, h_hbm, g_ref,
                    wg_ref, wu_ref, wd_ref, y_hbm, x_buf, y_buf, xn_ref, acc_ref, in_sem, out_sem):
    i, f = pl.program_id(0), pl.program_id(1)
    n_tiles, n_f = pl.num_programs(0), pl.num_programs(1)
    live = i < nv_ref[0]
    tm = xn_ref.shape[0]
    n_rows = x_buf.shape[1]
    rows_per_step = n_rows // n_f
    slot = i % 2

    def gather_copy(idx_ref, r, dst_slot):
        return pltpu.make_async_copy(h_hbm.at[pl.ds(idx_ref[0, 0, r], 1)],
                                     x_buf.at[dst_slot, pl.ds(r, 1)], in_sem.at[dst_slot])

    def scatter_copy(idx_ref, r, src_slot):
        src_row = jnp.minimum(r, tm - 1)
        return pltpu.make_async_copy(y_buf.at[src_slot, pl.ds(src_row, 1)],
                                     y_hbm.at[pl.ds(idx_ref[0, 0, r], 1)], out_sem.at[src_slot])

    def wait_gather(dst_slot):
        pltpu.make_async_copy(h_hbm.at[pl.ds(0, n_rows)], x_buf.at[dst_slot], in_sem.at[dst_slot]).wait()

    def wait_scatter(src_slot):
        pltpu.make_async_copy(y_buf.at[src_slot], y_hbm.at[pl.ds(0, n_rows)], out_sem.at[src_slot]).wait()

    def step_rows():
        return [f * rows_per_step + k for k in range(rows_per_step)]

    def gather_next_tile_rows():
        for r in step_rows():
            gather_copy(next_rows_ref, r, 1 - slot).start()

    def scatter_prev_tile_rows():
        for r in step_rows():
            scatter_copy(prev_dest_ref, r, 1 - slot).start()

    @pl.when((f == 0) & (i == 0))
    def _():
        def body(r, carry):
            gather_copy(rows_ref, r, slot).start()
            return carry
        lax.fori_loop(0, n_rows, body, 0)

    @pl.when(f == 0)
    def _():
        wait_gather(slot)
        xn_ref[...] = _rms(x_buf[slot, :tm], g_ref[...], 1e-6).astype(BF16)
        acc_ref[...] = jnp.zeros(acc_ref.shape, F32)

    @pl.when(live & (i > 0))
    def _():
        _swiglu_accumulate(xn_ref[...], wg_ref[...], wu_ref[...], wd_ref[...], acc_ref)
        gather_next_tile_rows()
        scatter_prev_tile_rows()

    @pl.when(live & (i == 0))
    def _():
        _swiglu_accumulate(xn_ref[...], wg_ref[...], wu_ref[...], wd_ref[...], acc_ref)
        gather_next_tile_rows()

    @pl.when(jnp.logical_not(live))
    def _():
        gather_next_tile_rows()
        scatter_prev_tile_rows()

    @pl.when(f == n_f - 1)
    def _():
        @pl.when(i >= 2)
        def _():
            wait_scatter(slot)
        y_buf[slot, :tm] = acc_ref[...]

    @pl.when((f == n_f - 1) & (i == n_tiles - 1))
    def _():
        def body(r, carry):
            scatter_copy(dest_ref, r, slot).start()
            return carry
        lax.fori_loop(0, n_rows, body, 0)
        wait_gather(1 - slot)
        wait_scatter(1 - slot)
        wait_scatter(slot)


def _moe_tile_rows(tm, nf):
    return -(-tm // (8 * nf)) * 8 * nf


def moe_ffn(h, row_token, row_dest, n_y_rows, g, wg, wu, wd, tile_expert, n_live, *, tm=MOE_TM):
    d = h.shape[1]
    nf, _, _, tf = wg.shape
    n_tiles, _, n_rows = row_token.shape
    assert wd.shape[1] == nf * tf and n_rows == _moe_tile_rows(tm, nf) and n_tiles >= 2

    def f_idx(i, f, nv):
        return jnp.where(i < nv[0], f, nf - 1)

    grid_spec = pltpu.PrefetchScalarGridSpec(
        num_scalar_prefetch=2,
        grid=(n_tiles, nf),
        in_specs=[
            pl.BlockSpec((1, 1, n_rows), lambda i, f, te, nv: (i, 0, 0), memory_space=pltpu.SMEM),
            pl.BlockSpec((1, 1, n_rows), lambda i, f, te, nv: (jnp.minimum(i + 1, n_tiles - 1), 0, 0),
                         memory_space=pltpu.SMEM),
            pl.BlockSpec((1, 1, n_rows), lambda i, f, te, nv: (i, 0, 0), memory_space=pltpu.SMEM),
            pl.BlockSpec((1, 1, n_rows), lambda i, f, te, nv: (jnp.maximum(i - 1, 0), 0, 0),
                         memory_space=pltpu.SMEM),
            pl.BlockSpec(memory_space=pl.ANY),
            pl.BlockSpec((1, d), lambda i, f, te, nv: (0, 0)),
            pl.BlockSpec((None, None, d, tf), lambda i, f, te, nv: (f_idx(i, f, nv), te[i], 0, 0)),
            pl.BlockSpec((None, None, d, tf), lambda i, f, te, nv: (f_idx(i, f, nv), te[i], 0, 0)),
            pl.BlockSpec((None, tf, d), lambda i, f, te, nv: (te[i], f_idx(i, f, nv), 0)),
        ],
        out_specs=pl.BlockSpec(memory_space=pl.ANY),
        scratch_shapes=[pltpu.VMEM((2, n_rows, d), F32), pltpu.VMEM((2, n_rows, d), F32),
                        pltpu.VMEM((tm, d), BF16), pltpu.VMEM((tm, d), F32),
                        pltpu.SemaphoreType.DMA((2,)), pltpu.SemaphoreType.DMA((2,))],
    )
    vmem = 4 * n_rows * d * 4 + tm * d * (2 + 4) + 3 * 2 * d * tf * 2 + 4 * tm * tf * 4 + (6 << 20)
    return pl.pallas_call(
        _moe_ffn_kernel,
        grid_spec=grid_spec,
        out_shape=jax.ShapeDtypeStruct((n_y_rows, d), F32),
        compiler_params=_cparams(("arbitrary", "arbitrary"), vmem),
        name="moe_ffn",
    )(tile_expert, n_live, row_token, row_token, row_dest, row_dest, h, g.reshape(1, d), wg, wu, wd)


def _combine_kernel(y0_ref, y1_ref, h_ref, gate_ref, fg_ref, o_ref):
    gate = gate_ref[...]
    out = h_ref[...] + gate[:, 0:1] * y0_ref[...] + gate[:, 1:2] * y1_ref[...]
    o_ref[...] = _rms(out, fg_ref[...], 1e-6)


def moe_combine(y, h, gates, final_gain, *, tc=COMBINE_ROWS):
    seq, d = h.shape
    n_tiles = seq // tc
    row = lambda w, off: pl.BlockSpec((tc, w), lambda i: (i + off, 0))
    return pl.pallas_call(
        _combine_kernel,
        grid=(n_tiles,),
        in_specs=[row(d, 0), row(d, n_tiles), row(d, 0), row(V7X_LANES, 0),
                  pl.BlockSpec((1, d), lambda i: (0, 0))],
        out_specs=row(d, 0),
        out_shape=jax.ShapeDtypeStruct((seq, d), F32),
        compiler_params=_cparams(("parallel",), 10 * tc * d * 4 + (4 << 20)),
        name="moe_combine",
    )(y, y, h, gates, final_gain.reshape(1, d))


def _swap_halves(w):
    half = w.shape[-1] // 2
    return jnp.concatenate([w[..., half:], w[..., :half]], axis=-1)


def _mla_weights(w_uq, w_ukv):
    lora = w_uq.shape[0]
    uq = w_uq.reshape(lora, B_HEADS, B_NOPE + B_ROPE)
    nope, pe = uq[..., :B_NOPE], uq[..., B_NOPE:]
    zpad = jnp.zeros((lora, B_HEADS, 128 - B_ROPE), w_uq.dtype)
    wq1 = jnp.concatenate([nope, pe, zpad], axis=-1).reshape(lora, B_HEADS * 256)
    wq2 = jnp.concatenate([_swap_halves(pe), zpad], axis=-1).reshape(lora, B_HEADS * 128)
    ukv = w_ukv.reshape(lora, B_HEADS, B_NOPE + B_V)
    wkn = ukv[..., :B_NOPE].reshape(lora, B_HEADS * B_NOPE)
    wv = ukv[..., B_NOPE:].reshape(lora, B_HEADS * B_V)
    return wq1.astype(BF16), wq2.astype(BF16), wkn.astype(BF16), wv.astype(BF16)


def _rope_tables(seq):
    inv = ROPE_THETA ** (-jnp.arange(0, B_ROPE, 2, dtype=F32) / B_ROPE)
    ang = jnp.arange(seq, dtype=F32)[:, None] * inv[None, :]
    cos, sin = jnp.cos(ang), jnp.sin(ang)
    zeros = jnp.zeros((seq, 128 - B_ROPE), F32)
    return (jnp.concatenate([cos, cos, zeros], axis=1), jnp.concatenate([-sin, sin, zeros], axis=1))


def _dispatch_plan(idx, rank, counts, tm, n_rows):
    seq = idx.shape[0]
    n_tiles = seq * TOP_K // tm + N_EXPERTS
    cnt = counts[0, :N_EXPERTS].astype(jnp.int32)
    tiles_per = (cnt + tm - 1) // tm
    tile_end = jnp.cumsum(tiles_per)
    row_start = (tile_end - tiles_per) * tm
    pos = (row_start[idx[:, :TOP_K]] + rank[:, :TOP_K]).reshape(-1)
    entry = (pos // tm) * n_rows + pos % tm
    tile_id = jnp.arange(n_tiles, dtype=jnp.int32)
    spare = (TOP_K * seq + (tile_id[:, None] % 2) * n_rows
             + jnp.arange(n_rows, dtype=jnp.int32)[None, :]).reshape(-1)
    pair = jnp.full((n_tiles * n_rows,), -1, jnp.int32).at[entry].set(
        jnp.arange(seq * TOP_K, dtype=jnp.int32))
    tok, slot = pair // TOP_K, pair % TOP_K
    row_token = jnp.where(pair >= 0, tok, 0)
    row_dest = jnp.where(pair >= 0, slot * seq + tok, spare)
    n_live = tile_end[-1]
    tile_expert = jnp.sum((tile_id[:, None] >= tile_end[None, :]).astype(jnp.int32), axis=1)
    last_expert = jnp.sum((n_live - 1 >= tile_end).astype(jnp.int32))
    tile_expert = jnp.where(tile_id < n_live, tile_expert, last_expert).astype(jnp.int32)
    shape3 = (n_tiles, 1, n_rows)
    return (row_token.reshape(shape3), row_dest.reshape(shape3), tile_expert,
            n_live.reshape(1).astype(jnp.int32), TOP_K * seq + 2 * n_rows)


def kernel(x, ev_attn_norm, ev_w_in, ev_q_norm, ev_w_uq, ev_kv_norm, ev_w_ukv, ev_lambda_q1, ev_lambda_k1, ev_lambda_q2, ev_lambda_k2, ev_subln, ev_w_out, ev_ffn_norm, ev_ffn_w_gate, ev_ffn_w_up, ev_ffn_w_down, od_attn_norm, od_w_in, od_forget_bias, od_w_out, od_ffn_norm, od_router, od_moe_w_gate, od_moe_w_up, od_moe_w_down, final_norm):
    batch, seq, d = x.shape
    assert batch == 1
    h = x.reshape(seq, d)

    lambda_init = 0.8 - 0.6 * math.exp(-0.3 * 0)
    w_in = ev_w_in[0]
    n_a = 2 * A_HEADS * 2 * A_QK + A_HEADS * A_V
    w_a = w_in[:, :n_a].astype(BF16)
    scale_a = jnp.concatenate([jnp.full((A_HEADS * 2 * A_QK,), A_QK ** -0.5 * LOG2E, F32),
                               jnp.ones((n_a - A_HEADS * 2 * A_QK,), F32)])
    w_b = jnp.concatenate([w_in[:, n_a:], _swap_halves(w_in[:, -B_ROPE:])], axis=1).astype(BF16)
    qkv_a = rms_matmul(h, ev_attn_norm[0], w_a, scale_a, BF16)
    c_b = rms_matmul(h, ev_attn_norm[0], w_b, jnp.ones((w_b.shape[1],), F32), F32, tn=w_b.shape[1])
    oa, (ffn_wg, ffn_wu, ffn_wd) = diff_attention(
        qkv_a, ev_lambda_q1[0], ev_lambda_k1[0], ev_lambda_q2[0], ev_lambda_k2[0], ev_subln[0],
        lambda_init, [(ev_ffn_w_gate[0], FFN_TF), (ev_ffn_w_up[0], FFN_TF), (ev_ffn_w_down[0], None)])
    wq1, wq2, wkn, wv = _mla_weights(ev_w_uq[0], ev_w_ukv[0])
    cos_t, sin_t = _rope_tables(seq)
    q_b, k_b, v_b = mla_prep(c_b, ev_q_norm[0], ev_kv_norm[0], wq1, wq2, wkn, wv, cos_t, sin_t)
    ob = mla_attention(q_b, k_b, v_b)
    w_out = ev_w_out[0].astype(BF16)
    h = proj_residual([oa, ob], [w_out[:A_HEADS * A_V], w_out[A_HEADS * A_V:]], h)
    h = dense_ffn(h, ev_ffn_norm[0], ffn_wg, ffn_wu, ffn_wd)

    width = C_HEADS * C_DIM
    w_qkv = od_w_in[0][:, :3 * width].astype(BF16)
    scale_c = jnp.concatenate([jnp.full((width,), C_DIM ** -0.5 * LOG2E, F32), jnp.ones((2 * width,), F32)])
    qkv_c = rms_matmul(h, od_attn_norm[0], w_qkv, scale_c, BF16)
    wf_t = od_w_in[0][:, 3 * width:].T.astype(BF16)
    cum = forget_gate_cumsum(h, od_attn_norm[0], wf_t, od_forget_bias[0])
    n_e, _, ff = od_moe_w_gate[0].shape
    oc, (moe_wg, moe_wu, moe_wd) = fox_attention(qkv_c, cum, [
        (od_moe_w_gate[0].reshape(n_e * d, ff), FFN_TF), (od_moe_w_up[0].reshape(n_e * d, ff), FFN_TF),
        (od_moe_w_down[0].reshape(n_e * ff, d), None)])
    h = proj_residual([oc], [od_w_out[0].astype(BF16)], h)

    w_router = jnp.zeros((d, V7X_LANES), F32).at[:, :N_EXPERTS].set(od_router[0])
    idx, gates, rank, counts = moe_router(h, od_ffn_norm[0], w_router)
    n_rows = _moe_tile_rows(MOE_TM, od_moe_w_gate.shape[-1] // FFN_TF)
    row_token, row_dest, tile_expert, n_live, n_y_rows = _dispatch_plan(idx, rank, counts, MOE_TM, n_rows)
    blocked = (ff // FFN_TF, n_e, d, FFN_TF)
    y = moe_ffn(h, row_token, row_dest, n_y_rows, od_ffn_norm[0], moe_wg.reshape(blocked),
                moe_wu.reshape(blocked), moe_wd.reshape(n_e, ff, d), tile_expert, n_live)
    out = moe_combine(y, h, gates, final_norm)
    return out.reshape(batch, seq, d)
```

```python
import functools
import math

import jax
import jax.numpy as jnp
from jax import lax
from jax.experimental import pallas as pl
from jax.experimental.pallas import tpu as pltpu

F32 = jnp.float32
BF16 = jnp.bfloat16

V7X_LANES = 128
V7X_VMEM_REQUEST_CAP = 60000 * 1024

A_HEADS, A_QK, A_V = 8, 64, 128
B_HEADS, B_LORA, B_NOPE, B_ROPE, B_V = 8, 512, 128, 64, 128
C_HEADS, C_DIM = 16, 128
N_EXPERTS, TOP_K = 8, 2
ROPE_THETA = 10000.0
MASK_VALUE = -1e30
LOG2E = math.log2(math.e)

ATTN_BLOCK = 1024
DIFF_ATTN_BLOCK = 1024
LONG_TRIP_PAIRS = 4
HEADS_PER_STEP = 1
SOFTMAX_ROWS = 128
PROJ_TM, PROJ_TN = 1024, 1024
FFN_TM, FFN_TF = 512, 512
MOE_TM = 512
COMBINE_ROWS = 256
PREP_TM = 512
GATE_TM = 1024
ROUTER_TM = 512


def _cparams(semantics, vmem_bytes):
    return pltpu.CompilerParams(
        dimension_semantics=semantics,
        vmem_limit_bytes=int(min(V7X_VMEM_REQUEST_CAP, vmem_bytes)))


def _rms(x, g, eps):
    return x * lax.rsqrt(jnp.mean(x * x, axis=-1, keepdims=True) + eps) * g


def _rms_matmul_kernel(x_ref, g_ref, w_ref, cs_ref, o_ref, xn_ref, *, eps):
    @pl.when(pl.program_id(1) == 0)
    def _():
        xn_ref[...] = _rms(x_ref[...], g_ref[...], eps).astype(BF16)

    acc = jnp.dot(xn_ref[...], w_ref[...], preferred_element_type=F32)
    o_ref[...] = (acc * cs_ref[...]).astype(o_ref.dtype)


def rms_matmul(x, g, w, col_scale, out_dtype, *, eps=1e-6, tm=PROJ_TM, tn=PROJ_TN):
    m, k = x.shape
    n = w.shape[1]
    assert m % tm == 0 and n % tn == 0, (m, n, tm, tn)
    vmem = 2 * tm * k * 4 + tm * k * 2 + 2 * k * tn * 2 + 4 * tm * tn * 4 + (4 << 20)
    return pl.pallas_call(
        functools.partial(_rms_matmul_kernel, eps=eps),
        grid=(m // tm, n // tn),
        in_specs=[
            pl.BlockSpec((tm, k), lambda i, j: (i, 0)),
            pl.BlockSpec((1, k), lambda i, j: (0, 0)),
            pl.BlockSpec((k, tn), lambda i, j: (0, j)),
            pl.BlockSpec((1, tn), lambda i, j: (0, j)),
        ],
        out_specs=pl.BlockSpec((tm, tn), lambda i, j: (i, j)),
        out_shape=jax.ShapeDtypeStruct((m, n), out_dtype),
        scratch_shapes=[pltpu.VMEM((tm, k), BF16)],
        compiler_params=_cparams(("parallel", "arbitrary"), vmem),
        name="rms_matmul",
    )(x, g.reshape(1, k), w, col_scale.reshape(1, n))


def _proj_residual_kernel(*refs, n_in):
    a_refs, w_refs = refs[:n_in], refs[n_in:2 * n_in]
    h_ref, o_ref = refs[2 * n_in], refs[2 * n_in + 1]
    acc = h_ref[...]
    for a_ref, w_ref in zip(a_refs, w_refs):
        acc = acc + jnp.dot(a_ref[...], w_ref[...], preferred_element_type=F32)
    o_ref[...] = acc


def proj_residual(a_list, w_list, h, *, tm=PROJ_TM, tn=PROJ_TN):
    m, n = h.shape
    n_in = len(a_list)
    assert m % tm == 0 and n % tn == 0
    ks = [a.shape[1] for a in a_list]
    vmem = sum(2 * tm * k * 2 + 2 * k * tn * 2 for k in ks) + 6 * tm * tn * 4 + (4 << 20)
    in_specs = [pl.BlockSpec((tm, k), lambda i, j: (i, 0)) for k in ks]
    in_specs += [pl.BlockSpec((k, tn), lambda i, j: (0, j)) for k in ks]
    in_specs += [pl.BlockSpec((tm, tn), lambda i, j: (i, j))]
    return pl.pallas_call(
        functools.partial(_proj_residual_kernel, n_in=n_in),
        grid=(m // tm, n // tn),
        in_specs=in_specs,
        out_specs=pl.BlockSpec((tm, tn), lambda i, j: (i, j)),
        out_shape=jax.ShapeDtypeStruct((m, n), F32),
        compiler_params=_cparams(("parallel", "arbitrary"), vmem),
        name="proj_residual",
    )(*a_list, *w_list, h)


def _attn_stream(q, k_ref, v_ref, bias_fn, scratch, *, bq):
    s_bufs, p_bufs, a_bufs = scratch[0:2], scratch[2:4], scratch[4:6]
    m_scr, l_scr, acc_scr = scratch[6:9]
    rows, bk = s_bufs[0].shape
    assert bq == 2 * bk
    n_lane_chunks = bk // V7X_LANES
    m_scr[...] = jnp.full(m_scr.shape, MASK_VALUE, F32)
    l_scr[...] = jnp.zeros(l_scr.shape, F32)
    acc_scr[...] = jnp.zeros(acc_scr.shape, F32)
    a_bufs[1][...] = jnp.ones(a_bufs[1].shape, F32)
    p_bufs[1][...] = jnp.zeros(p_bufs[1].shape, BF16)

    def scores(j, slot):
        start = pl.multiple_of(j * bk, bk)
        s = lax.dot_general(q, k_ref[pl.ds(start, bk), :], (((1,), (1,)), ((), ())),
                            preferred_element_type=F32)
        s_bufs[slot][...] = s if bias_fn is None else s + bias_fn(start)

    def softmax(slot, diag_offset=None):
        s_buf, p_buf, a_buf = s_bufs[slot], p_bufs[slot], a_bufs[slot]
        for r0 in range(0, rows, SOFTMAX_ROWS):
            sl = slice(r0, r0 + SOFTMAX_ROWS)
            first_row = r0 % bq
            if diag_offset is not None and first_row + SOFTMAX_ROWS - 1 < diag_offset:
                continue
            s = s_buf[sl, :]
            if diag_offset is not None and first_row < diag_offset + bk - 1:
                row = lax.broadcasted_iota(jnp.int32, s.shape, 0) + first_row
                col = lax.broadcasted_iota(jnp.int32, s.shape, 1) + diag_offset
                s = jnp.where(col <= row, s, MASK_VALUE)
            chunks = [s[:, i * V7X_LANES:(i + 1) * V7X_LANES] for i in range(n_lane_chunks)]
            m_lane = chunks[0]
            for x in chunks[1:]:
                m_lane = jnp.maximum(m_lane, x)
            m_prev = m_scr[sl, :]
            m_next = jnp.maximum(m_prev, jnp.max(m_lane, axis=1, keepdims=True))
            alpha = jnp.exp2(m_prev - m_next)
            l_new = alpha * l_scr[sl, :]
            for i, x in enumerate(chunks):
                p = jnp.exp2(x - m_next)
                l_new = l_new + p
                p_buf[sl, i * V7X_LANES:(i + 1) * V7X_LANES] = p.astype(BF16)
            l_scr[sl, :] = l_new
            m_scr[sl, :] = m_next
            a_buf[sl, :] = alpha

    def pv(j, slot, first_live_row=0):
        start = pl.multiple_of(j * bk, bk)
        v = v_ref[pl.ds(start, bk), :]
        live = ([slice(0, rows)] if first_live_row == 0 else
                [slice(r0 + first_live_row, r0 + bq) for r0 in range(0, rows, bq)])
        for sl in live:
            acc_scr[sl, :] = acc_scr[sl, :] * a_bufs[slot][sl, :] + jnp.dot(
                p_bufs[slot][sl, :], v, preferred_element_type=F32)

    def result():
        return acc_scr[...] / jnp.sum(l_scr[...], axis=1, keepdims=True)

    return scores, softmax, pv, result


def _flash_attend(heads, qi, scratch, *, bq, pairs_per_trip=2):
    bk = bq // 2
    per_head = len(scratch) // len(heads)
    streams = [_attn_stream(*head, scratch[t * per_head:(t + 1) * per_head], bq=bq)
               for t, head in enumerate(heads)]

    def each(stage, *args, **kwargs):
        for stream in streams:
            stream[stage](*args, **kwargs)

    SCORES, SOFTMAX, PV = 0, 1, 2

    def two_steps(j, diag):
        each(PV, jnp.maximum(j - 1, 0), 1)
        each(SOFTMAX, 0, 0 if diag else None)
        each(SCORES, j + 1, 1)
        each(PV, j, 0)
        each(SOFTMAX, 1, bk if diag else None)
        if not diag:
            each(SCORES, j + 2, 0)

    each(SCORES, 0, 0)

    done = 0
    pairs = pairs_per_trip
    while pairs >= 1:
        def body(i, carry, pairs=pairs, done=done):
            for u in range(pairs):
                two_steps(2 * (done + pairs * i + u), False)
            return carry

        n_trips = (qi - done) // pairs
        lax.fori_loop(0, n_trips, body, 0)
        done = done + n_trips * pairs
        pairs //= 2
    two_steps(2 * qi, True)
    each(PV, 2 * qi + 1, 1, first_live_row=bk)
    return [stream[3]() for stream in streams]


def _attn_scratch(rows, bq, n_heads=1):
    bk = bq // 2
    stat = lambda: pltpu.VMEM((rows, V7X_LANES), F32)
    s_buf = lambda: pltpu.VMEM((rows, bk), F32)
    p_buf = lambda: pltpu.VMEM((rows, bk), BF16)
    per_head = lambda: [s_buf(), s_buf(), p_buf(), p_buf()] + [stat() for _ in range(5)]
    return [buf for _ in range(n_heads) for buf in per_head()]


def _attn_vmem(seq, rows, bq, kv_width, n_heads=1, kv_buffers=2):
    resident = kv_buffers * seq * kv_width * 2 * n_heads
    buffers = n_heads * (2 * rows * (bq // 2) * (4 + 2) + 5 * rows * V7X_LANES * 4)
    temps = 2 * rows * (bq // 2) * 4
    return resident + buffers + temps + (6 << 20)


def _kv_buffers(n_heads):
    return 2 if n_heads == 1 else 1


def _resident(block_shape, index_map, n_heads):
    if _kv_buffers(n_heads) == 2:
        return pl.BlockSpec(block_shape, index_map)
    return pl.BlockSpec(block_shape, index_map, pipeline_mode=pl.Buffered(1))


def _side_cast_specs(weights, n_steps, step_index):
    in_specs, out_specs, out_shapes, vmem = [], [], [], 0
    for w, col_block in weights:
        rows, cols = w.shape
        block = next(b for b in range(16, rows + 1, 16) if rows % b == 0 and b * n_steps >= rows)
        last = rows // block - 1

        def row_block(*grid_idx, last=last):
            return jnp.minimum(step_index(*grid_idx), last)

        in_specs.append(pl.BlockSpec((block, cols), lambda *g, rb=row_block: (rb(*g), 0)))
        if col_block is None:
            out_specs.append(pl.BlockSpec((block, cols), lambda *g, rb=row_block: (rb(*g), 0)))
            out_shapes.append(jax.ShapeDtypeStruct((rows, cols), BF16))
        else:
            out_specs.append(pl.BlockSpec((cols // col_block, block, col_block),
                                          lambda *g, rb=row_block: (0, rb(*g), 0)))
            out_shapes.append(jax.ShapeDtypeStruct((cols // col_block, rows, col_block), BF16))
        vmem += 2 * block * cols * (4 + 2)
    return in_specs, out_specs, out_shapes, vmem


def _side_cast(in_refs, out_refs):
    for src, dst in zip(in_refs, out_refs):
        if len(dst.shape) == 2:
            dst[...] = src[...].astype(dst.dtype)
        else:
            width = dst.shape[2]
            for c in range(dst.shape[0]):
                dst[c] = src[:, c * width:(c + 1) * width].astype(dst.dtype)


def _diff_attn_kernel(slope_ref, q_ref, k_ref, v_ref, lq1_ref, lk1_ref, lq2_ref, lk2_ref,
                      subln_ref, *rest, bq, lambda_init, n_side):
    o_ref, scratch = rest[n_side], rest[2 * n_side + 1:]
    _side_cast(rest[:n_side], rest[n_side + 1:2 * n_side + 1])
    h, qi = pl.program_id(0), pl.program_id(1)
    q = q_ref[...]
    lane = lax.broadcasted_iota(jnp.int32, q.shape, 1)
    zero = jnp.zeros_like(q)
    qs = jnp.concatenate([jnp.where(lane < A_QK, q, zero), jnp.where(lane >= A_QK, q, zero)], axis=0)
    slope = slope_ref[h] * LOG2E
    t0 = qi * bq

    def bias_fn(start):
        kpos = lax.broadcasted_iota(jnp.int32, (1, bq // 2), 1) + (start - t0)
        return slope * kpos.astype(F32)

    o, = _flash_attend([(qs, k_ref, v_ref, bias_fn)], qi, scratch, bq=bq)
    lam = (jnp.exp(jnp.sum(lq1_ref[...] * lk1_ref[...], axis=1, keepdims=True))
           - jnp.exp(jnp.sum(lq2_ref[...] * lk2_ref[...], axis=1, keepdims=True)) + lambda_init)
    d = o[:bq] - lam * o[bq:]
    o_ref[...] = (_rms(d, subln_ref[...], 1e-5) * (1.0 - lambda_init)).astype(o_ref.dtype)


def diff_attention(qkv, lq1, lk1, lq2, lk2, subln, lambda_init, side_weights, *, bq=DIFF_ATTN_BLOCK):
    seq = qkv.shape[0]
    nq = seq // bq
    slopes = jnp.exp2(-8.0 * jnp.arange(1, A_HEADS + 1, dtype=F32) / A_HEADS)
    vec = lambda: pl.BlockSpec((1, A_QK), lambda h, i, s: (0, 0))
    side_in, side_out, side_shapes, side_vmem = _side_cast_specs(
        side_weights, A_HEADS * nq, lambda h, i, s: h * nq + i)
    grid_spec = pltpu.PrefetchScalarGridSpec(
        num_scalar_prefetch=1,
        grid=(A_HEADS, nq),
        in_specs=[
            pl.BlockSpec((bq, 128), lambda h, i, s: (i, h)),
            pl.BlockSpec((seq, 128), lambda h, i, s: (0, A_HEADS + h)),
            pl.BlockSpec((seq, 128), lambda h, i, s: (0, 2 * A_HEADS + h)),
            vec(), vec(), vec(), vec(),
            pl.BlockSpec((1, A_V), lambda h, i, s: (0, 0)),
        ] + side_in,
        out_specs=[pl.BlockSpec((bq, A_V), lambda h, i, s: (i, h))] + side_out,
        scratch_shapes=_attn_scratch(2 * bq, bq),
    )
    out = pl.pallas_call(
        functools.partial(_diff_attn_kernel, bq=bq, lambda_init=lambda_init, n_side=len(side_weights)),
        grid_spec=grid_spec,
        out_shape=[jax.ShapeDtypeStruct((seq, A_HEADS * A_V), BF16)] + side_shapes,
        compiler_params=_cparams(("arbitrary", "arbitrary"), _attn_vmem(seq, 2 * bq, bq, 256) + side_vmem),
        name="diff_attention",
    )(slopes, qkv, qkv, qkv, lq1.reshape(1, -1), lk1.reshape(1, -1), lq2.reshape(1, -1),
      lk2.reshape(1, -1), subln.reshape(1, -1), *[w for w, _ in side_weights])
    return out[0], out[1:]


def _mla_prep_kernel(c_ref, qn_ref, kvn_ref, wq1_ref, wq2_ref, wkn_ref, wv_ref, cos_ref, sin_ref,
                     q_ref, k_ref, v_ref, *, scale):
    c = c_ref[...]
    cqn = _rms(c[:, :B_LORA], qn_ref[...], 1e-6).astype(BF16)
    ckvn = _rms(c[:, B_LORA:2 * B_LORA], kvn_ref[...], 1e-6).astype(BF16)
    kp = c[:, 2 * B_LORA:]
    cos, sin = cos_ref[...], sin_ref[...]
    qa = jnp.dot(cqn, wq1_ref[...], preferred_element_type=F32)
    qb = jnp.dot(cqn, wq2_ref[...], preferred_element_type=F32)
    kn = jnp.dot(ckvn, wkn_ref[...], preferred_element_type=F32)
    v_ref[...] = jnp.dot(ckvn, wv_ref[...], preferred_element_type=F32).astype(v_ref.dtype)
    kr = (kp * cos + pltpu.roll(kp, 64, axis=1) * sin).astype(k_ref.dtype)
    for h in range(B_HEADS):
        lo, hi = 256 * h, 256 * h + 128
        q_ref[:, lo:hi] = (qa[:, lo:hi] * scale).astype(q_ref.dtype)
        pe = qa[:, hi:hi + 128] * cos + qb[:, 128 * h:128 * (h + 1)] * sin
        q_ref[:, hi:hi + 128] = (pe * scale).astype(q_ref.dtype)
        k_ref[:, lo:hi] = kn[:, 128 * h:128 * (h + 1)].astype(k_ref.dtype)
        k_ref[:, hi:hi + 128] = kr


def mla_prep(c, q_norm, kv_norm, wq1, wq2, wkn, wv, cos_t, sin_t, *, tm=PREP_TM):
    seq = c.shape[0]
    scale = (B_NOPE + B_ROPE) ** -0.5 * LOG2E
    full = lambda a: pl.BlockSpec(a.shape, lambda i: (0, 0))
    row = lambda w: pl.BlockSpec((tm, w), lambda i: (i, 0))
    qn, kvn = q_norm.reshape(1, -1), kv_norm.reshape(1, -1)
    weights = 2 * 2 * (wq1.size + wq2.size + wkn.size + wv.size)
    vmem = weights + 2 * tm * (1152 * 4 + 256 * 4 + 5120 * 2) + tm * 5120 * 4 * 2 + (6 << 20)
    return pl.pallas_call(
        functools.partial(_mla_prep_kernel, scale=scale),
        grid=(seq // tm,),
        in_specs=[row(c.shape[1]), full(qn), full(kvn), full(wq1), full(wq2), full(wkn), full(wv),
                  row(128), row(128)],
        out_specs=[row(2048), row(2048), row(1024)],
        out_shape=[jax.ShapeDtypeStruct((seq, 2048), BF16), jax.ShapeDtypeStruct((seq, 2048), BF16),
                   jax.ShapeDtypeStruct((seq, 1024), BF16)],
        compiler_params=_cparams(("parallel",), vmem),
        name="mla_prep",
    )(c, qn, kvn, wq1, wq2, wkn, wv, cos_t, sin_t)


def _mla_attn_kernel(q_ref, k_ref, v_ref, o_ref, *scratch, bq):
    heads = [(q_ref[:, 256 * t:256 * (t + 1)], k_ref.at[:, pl.ds(256 * t, 256)],
              v_ref.at[:, pl.ds(B_V * t, B_V)], None) for t in range(HEADS_PER_STEP)]
    for t, o in enumerate(_flash_attend(heads, pl.program_id(1), scratch, bq=bq,
                                        pairs_per_trip=LONG_TRIP_PAIRS)):
        o_ref[:, B_V * t:B_V * (t + 1)] = o.astype(o_ref.dtype)


def mla_attention(q, k, v, *, bq=ATTN_BLOCK, hp=HEADS_PER_STEP):
    seq = q.shape[0]
    return pl.pallas_call(
        functools.partial(_mla_attn_kernel, bq=bq),
        grid=(B_HEADS // hp, seq // bq),
        in_specs=[
            pl.BlockSpec((bq, 256 * hp), lambda h, i: (i, h)),
            _resident((seq, 256 * hp), lambda h, i: (0, h), hp),
            _resident((seq, B_V * hp), lambda h, i: (0, h), hp),
        ],
        out_specs=pl.BlockSpec((bq, B_V * hp), lambda h, i: (i, h)),
        out_shape=jax.ShapeDtypeStruct((seq, B_HEADS * B_V), BF16),
        scratch_shapes=_attn_scratch(bq, bq, hp),
        compiler_params=_cparams(("parallel", "arbitrary"), _attn_vmem(seq, bq, bq, 384, hp, _kv_buffers(hp))),
        name="mla_attention",
    )(q, k, v)


def _forget_gate_kernel(h_ref, g_ref, wf_ref, b_ref, cum_ref, carry_scr):
    @pl.when(pl.program_id(0) == 0)
    def _():
        carry_scr[...] = jnp.zeros(carry_scr.shape, F32)

    xn = _rms(h_ref[...], g_ref[...], 1e-6).astype(BF16)
    f = lax.dot_general(wf_ref[...], xn, (((1,), (1,)), ((), ())), preferred_element_type=F32)
    z = f + b_ref[...]
    log_f = jnp.minimum(z, 0.0) - jnp.log(1.0 + jnp.exp(-jnp.abs(z)))
    r = lax.broadcasted_iota(jnp.int32, (V7X_LANES, V7X_LANES), 0)
    c = lax.broadcasted_iota(jnp.int32, (V7X_LANES, V7X_LANES), 1)
    tri = (r <= c).astype(F32)
    carry = carry_scr[...]
    for j in range(log_f.shape[1] // V7X_LANES):
        sl = slice(j * V7X_LANES, (j + 1) * V7X_LANES)
        loc = jnp.dot(log_f[:, sl], tri, preferred_element_type=F32,
                      precision=lax.Precision.HIGHEST) + carry
        cum_ref[:, sl] = loc
        carry = jnp.broadcast_to(loc[:, V7X_LANES - 1:], carry.shape)
    carry_scr[...] = carry


def forget_gate_cumsum(h, g, wf_t, bias, *, tm=GATE_TM):
    seq, d = h.shape
    nh = wf_t.shape[0]
    return pl.pallas_call(
        _forget_gate_kernel,
        grid=(seq // tm,),
        in_specs=[
            pl.BlockSpec((tm, d), lambda i: (i, 0)),
            pl.BlockSpec((1, d), lambda i: (0, 0)),
            pl.BlockSpec((nh, d), lambda i: (0, 0)),
            pl.BlockSpec((nh, 1), lambda i: (0, 0)),
        ],
        out_specs=pl.BlockSpec((nh, tm), lambda i: (0, i)),
        out_shape=jax.ShapeDtypeStruct((nh, seq), F32),
        scratch_shapes=[pltpu.VMEM((nh, V7X_LANES), F32)],
        compiler_params=_cparams(("arbitrary",), 2 * tm * d * 4 + 3 * tm * d * 4 + (4 << 20)),
        name="forget_gate_cumsum",
    )(h, g.reshape(1, d), wf_t, bias.reshape(nh, 1))


def _fox_attn_kernel(q_ref, k_ref, v_ref, cum_ref, *rest, bq, n_side):
    o_ref, scratch = rest[n_side], rest[2 * n_side + 1:]
    _side_cast(rest[:n_side], rest[n_side + 1:2 * n_side + 1])
    qi = pl.program_id(1)
    t0 = pl.multiple_of(qi * bq, bq)

    def head(t):
        cum = cum_ref.at[t]
        c0 = cum[:, pl.ds(t0, V7X_LANES)][:, :1]

        def bias_fn(start):
            return (c0 - cum[:, pl.ds(start, bq // 2)]) * LOG2E

        lanes = pl.ds(C_DIM * t, C_DIM)
        return q_ref[:, C_DIM * t:C_DIM * (t + 1)], k_ref.at[:, lanes], v_ref.at[:, lanes], bias_fn

    heads = [head(t) for t in range(HEADS_PER_STEP)]
    for t, o in enumerate(_flash_attend(heads, qi, scratch, bq=bq, pairs_per_trip=LONG_TRIP_PAIRS)):
        o_ref[:, C_DIM * t:C_DIM * (t + 1)] = o.astype(o_ref.dtype)


def fox_attention(qkv, cum, side_weights, *, bq=ATTN_BLOCK, hp=HEADS_PER_STEP):
    seq = qkv.shape[0]
    nq = seq // bq
    cum3 = cum.reshape(C_HEADS, 1, seq)
    n_groups = C_HEADS // hp
    side_in, side_out, side_shapes, side_vmem = _side_cast_specs(
        side_weights, n_groups * nq, lambda h, i: h * nq + i)
    out = pl.pallas_call(
        functools.partial(_fox_attn_kernel, bq=bq, n_side=len(side_weights)),
        grid=(n_groups, nq),
        in_specs=[
            pl.BlockSpec((bq, C_DIM * hp), lambda h, i: (i, h)),
            _resident((seq, C_DIM * hp), lambda h, i: (0, n_groups + h), hp),
            _resident((seq, C_DIM * hp), lambda h, i: (0, 2 * n_groups + h), hp),
            pl.BlockSpec((hp, 1, seq), lambda h, i: (h, 0, 0)),
        ] + side_in,
        out_specs=[pl.BlockSpec((bq, C_DIM * hp), lambda h, i: (i, h))] + side_out,
        out_shape=[jax.ShapeDtypeStruct((seq, C_HEADS * C_DIM), BF16)] + side_shapes,
        scratch_shapes=_attn_scratch(bq, bq, hp),
        compiler_params=_cparams(("arbitrary", "arbitrary"),
                                 _attn_vmem(seq, bq, bq, 256, hp, _kv_buffers(hp)) + side_vmem),
        name="fox_attention",
    )(qkv, qkv, qkv, cum3, *[w for w, _ in side_weights])
    return out[0], out[1:]


def _swiglu_accumulate(xn, wg, wu, wd, acc_ref):
    gate = jnp.dot(xn, wg, preferred_element_type=F32)
    up = jnp.dot(xn, wu, preferred_element_type=F32)
    act = (gate * jax.nn.sigmoid(gate) * up).astype(BF16)
    acc_ref[...] += jnp.dot(act, wd, preferred_element_type=F32)


def _ffn_kernel(h_ref, g_ref, wg_ref, wu_ref, wd_ref, o_ref, xn_ref, acc_ref):
    f = pl.program_id(1)

    @pl.when(f == 0)
    def _():
        xn_ref[...] = _rms(h_ref[...], g_ref[...], 1e-6).astype(BF16)
        acc_ref[...] = jnp.zeros(acc_ref.shape, F32)

    _swiglu_accumulate(xn_ref[...], wg_ref[...], wu_ref[...], wd_ref[...], acc_ref)

    @pl.when(f == pl.num_programs(1) - 1)
    def _():
        o_ref[...] = h_ref[...] + acc_ref[...]


def _ffn_vmem(tm, d, tf):
    return 4 * tm * d * 4 + tm * d * 2 + tm * d * 4 + 3 * 2 * d * tf * 2 + 4 * tm * tf * 4 + (6 << 20)


def dense_ffn(h, g, wg, wu, wd, *, tm=FFN_TM):
    m, d = h.shape
    n_f, _, tf = wg.shape
    assert m % tm == 0 and wd.shape[0] == n_f * tf
    return pl.pallas_call(
        _ffn_kernel,
        grid=(m // tm, n_f),
        in_specs=[
            pl.BlockSpec((tm, d), lambda i, f: (i, 0)),
            pl.BlockSpec((1, d), lambda i, f: (0, 0)),
            pl.BlockSpec((None, d, tf), lambda i, f: (f, 0, 0)),
            pl.BlockSpec((None, d, tf), lambda i, f: (f, 0, 0)),
            pl.BlockSpec((tf, d), lambda i, f: (f, 0)),
        ],
        out_specs=pl.BlockSpec((tm, d), lambda i, f: (i, 0)),
        out_shape=jax.ShapeDtypeStruct((m, d), F32),
        scratch_shapes=[pltpu.VMEM((tm, d), BF16), pltpu.VMEM((tm, d), F32)],
        compiler_params=_cparams(("parallel", "arbitrary"), _ffn_vmem(tm, d, tf)),
        name="dense_ffn",
    )(h, g.reshape(1, d), wg, wu, wd)


def _router_kernel(h_ref, g_ref, wr_ref, idx_ref, gate_ref, rank_ref, cnt_ref, cnt_scr):
    @pl.when(pl.program_id(0) == 0)
    def _():
        cnt_scr[...] = jnp.zeros(cnt_scr.shape, F32)

    tm = h_ref.shape[0]
    xn = _rms(h_ref[...], g_ref[...], 1e-6)
    logits = jnp.dot(xn, wr_ref[...], preferred_element_type=F32, precision=lax.Precision.HIGHEST)
    lane = lax.broadcasted_iota(jnp.int32, logits.shape, 1)
    lane_f = lane.astype(F32)
    valid = lane < N_EXPERTS
    logits = jnp.where(valid, logits, MASK_VALUE)
    e = jnp.exp(logits - jnp.max(logits, axis=1, keepdims=True))
    probs = jnp.where(valid, e / jnp.sum(e, axis=1, keepdims=True), -1.0)
    p1 = jnp.max(probs, axis=1, keepdims=True)
    i1 = jnp.min(jnp.where(probs == p1, lane_f, float(V7X_LANES)), axis=1, keepdims=True)
    rest = jnp.where(lane_f == i1, -1.0, probs)
    p2 = jnp.max(rest, axis=1, keepdims=True)
    i2 = jnp.min(jnp.where(rest == p2, lane_f, float(V7X_LANES)), axis=1, keepdims=True)
    hot1 = (lane_f == i1).astype(F32)
    hot2 = (lane_f == i2).astype(F32)
    r = lax.broadcasted_iota(jnp.int32, (tm, tm), 0)
    c = lax.broadcasted_iota(jnp.int32, (tm, tm), 1)
    strict_lower = (c < r).astype(BF16)
    before = jnp.dot(strict_lower, (hot1 + hot2).astype(BF16), preferred_element_type=F32)
    before = before + cnt_scr[...]
    rank1 = jnp.sum(before * hot1, axis=1, keepdims=True)
    rank2 = jnp.sum(before * hot2, axis=1, keepdims=True)
    cnt_scr[...] += jnp.sum(hot1 + hot2, axis=0, keepdims=True)
    cnt_ref[...] = cnt_scr[...]
    denom = p1 + p2
    idx_ref[...] = jnp.where(lane == 0, i1, jnp.where(lane == 1, i2, 0.0)).astype(jnp.int32)
    gate_ref[...] = jnp.where(lane == 0, p1 / denom, jnp.where(lane == 1, p2 / denom, 0.0))
    rank_ref[...] = jnp.where(lane == 0, rank1, jnp.where(lane == 1, rank2, 0.0)).astype(jnp.int32)


def moe_router(h, g, w_router_padded, *, tm=ROUTER_TM):
    seq, d = h.shape
    out = lambda dt: jax.ShapeDtypeStruct((seq, V7X_LANES), dt)
    row = pl.BlockSpec((tm, V7X_LANES), lambda i: (i, 0))
    return pl.pallas_call(
        _router_kernel,
        grid=(seq // tm,),
        in_specs=[
            pl.BlockSpec((tm, d), lambda i: (i, 0)),
            pl.BlockSpec((1, d), lambda i: (0, 0)),
            pl.BlockSpec((d, V7X_LANES), lambda i: (0, 0)),
        ],
        out_specs=[row, row, row, pl.BlockSpec((1, V7X_LANES), lambda i: (0, 0))],
        out_shape=[out(jnp.int32), out(F32), out(jnp.int32),
                   jax.ShapeDtypeStruct((1, V7X_LANES), F32)],
        scratch_shapes=[pltpu.VMEM((1, V7X_LANES), F32)],
        compiler_params=_cparams(("arbitrary",), 5 * tm * d * 4 + 4 * tm * tm * 4 + (6 << 20)),
        name="moe_router",
    )(h, g.reshape(1, d), w_router_padded)


def _moe_ffn_kernel(te_ref, nv_ref, rows_ref, next_rows_ref, dest_ref, prev_dest_ref, h_hbm, g_ref,
                    wg_ref, wu_ref, wd_ref, y_hbm, x_buf, y_buf, xn_ref, acc_ref, in_sem, out_sem):
    i, f = pl.program_id(0), pl.program_id(1)
    n_tiles, n_f = pl.num_programs(0), pl.num_programs(1)
    live = i < nv_ref[0]
    tm = xn_ref.shape[0]
    n_rows = x_buf.shape[1]
    rows_per_step = n_rows // n_f
    slot = i % 2

    def gather_copy(idx_ref, r, dst_slot):
        return pltpu.make_async_copy(h_hbm.at[pl.ds(idx_ref[0, 0, r], 1)],
                                     x_buf.at[dst_slot, pl.ds(r, 1)], in_sem.at[dst_slot])

    def scatter_copy(idx_ref, r, src_slot):
        src_row = jnp.minimum(r, tm - 1)
        return pltpu.make_async_copy(y_buf.at[src_slot, pl.ds(src_row, 1)],
                                     y_hbm.at[pl.ds(idx_ref[0, 0, r], 1)], out_sem.at[src_slot])

    def wait_gather(dst_slot):
        pltpu.make_async_copy(h_hbm.at[pl.ds(0, n_rows)], x_buf.at[dst_slot], in_sem.at[dst_slot]).wait()

    def wait_scatter(src_slot):
        pltpu.make_async_copy(y_buf.at[src_slot], y_hbm.at[pl.ds(0, n_rows)], out_sem.at[src_slot]).wait()

    def step_rows():
        return [f * rows_per_step + k for k in range(rows_per_step)]

    def gather_next_tile_rows():
        for r in step_rows():
            gather_copy(next_rows_ref, r, 1 - slot).start()

    def scatter_prev_tile_rows():
        for r in step_rows():
            scatter_copy(prev_dest_ref, r, 1 - slot).start()

    @pl.when((f == 0) & (i == 0))
    def _():
        def body(r, carry):
            gather_copy(rows_ref, r, slot).start()
            return carry
        lax.fori_loop(0, n_rows, body, 0)

    @pl.when(f == 0)
    def _():
        wait_gather(slot)
        xn_ref[...] = _rms(x_buf[slot, :tm], g_ref[...], 1e-6).astype(BF16)
        acc_ref[...] = jnp.zeros(acc_ref.shape, F32)

    @pl.when(live & (i > 0))
    def _():
        _swiglu_accumulate(xn_ref[...], wg_ref[...], wu_ref[...], wd_ref[...], acc_ref)
        gather_next_tile_rows()
        scatter_prev_tile_rows()

    @pl.when(live & (i == 0))
    def _():
        _swiglu_accumulate(xn_ref[...], wg_ref[...], wu_ref[...], wd_ref[...], acc_ref)
        gather_next_tile_rows()

    @pl.when(jnp.logical_not(live))
    def _():
        gather_next_tile_rows()
        scatter_prev_tile_rows()

    @pl.when(f == n_f - 1)
    def _():
        @pl.when(i >= 2)
        def _():
            wait_scatter(slot)
        y_buf[slot, :tm] = acc_ref[...]

    @pl.when((f == n_f - 1) & (i == n_tiles - 1))
    def _():
        def body(r, carry):
            scatter_copy(dest_ref, r, slot).start()
            return carry
        lax.fori_loop(0, n_rows, body, 0)
        wait_gather(1 - slot)
        wait_scatter(1 - slot)
        wait_scatter(slot)


def _moe_tile_rows(tm, nf):
    return -(-tm // (8 * nf)) * 8 * nf


def moe_ffn(h, row_token, row_dest, n_y_rows, g, wg, wu, wd, tile_expert, n_live, *, tm=MOE_TM):
    d = h.shape[1]
    nf, _, _, tf = wg.shape
    n_tiles, _, n_rows = row_token.shape
    assert wd.shape[1] == nf * tf and n_rows == _moe_tile_rows(tm, nf) and n_tiles >= 2

    def f_idx(i, f, nv):
        return jnp.where(i < nv[0], f, nf - 1)

    grid_spec = pltpu.PrefetchScalarGridSpec(
        num_scalar_prefetch=2,
        grid=(n_tiles, nf),
        in_specs=[
            pl.BlockSpec((1, 1, n_rows), lambda i, f, te, nv: (i, 0, 0), memory_space=pltpu.SMEM),
            pl.BlockSpec((1, 1, n_rows), lambda i, f, te, nv: (jnp.minimum(i + 1, n_tiles - 1), 0, 0),
                         memory_space=pltpu.SMEM),
            pl.BlockSpec((1, 1, n_rows), lambda i, f, te, nv: (i, 0, 0), memory_space=pltpu.SMEM),
            pl.BlockSpec((1, 1, n_rows), lambda i, f, te, nv: (jnp.maximum(i - 1, 0), 0, 0),
                         memory_space=pltpu.SMEM),
            pl.BlockSpec(memory_space=pl.ANY),
            pl.BlockSpec((1, d), lambda i, f, te, nv: (0, 0)),
            pl.BlockSpec((None, None, d, tf), lambda i, f, te, nv: (f_idx(i, f, nv), te[i], 0, 0)),
            pl.BlockSpec((None, None, d, tf), lambda i, f, te, nv: (f_idx(i, f, nv), te[i], 0, 0)),
            pl.BlockSpec((None, tf, d), lambda i, f, te, nv: (te[i], f_idx(i, f, nv), 0)),
        ],
        out_specs=pl.BlockSpec(memory_space=pl.ANY),
        scratch_shapes=[pltpu.VMEM((2, n_rows, d), F32), pltpu.VMEM((2, n_rows, d), F32),
                        pltpu.VMEM((tm, d), BF16), pltpu.VMEM((tm, d), F32),
                        pltpu.SemaphoreType.DMA((2,)), pltpu.SemaphoreType.DMA((2,))],
    )
    vmem = 4 * n_rows * d * 4 + tm * d * (2 + 4) + 3 * 2 * d * tf * 2 + 4 * tm * tf * 4 + (6 << 20)
    return pl.pallas_call(
        _moe_ffn_kernel,
        grid_spec=grid_spec,
        out_shape=jax.ShapeDtypeStruct((n_y_rows, d), F32),
        compiler_params=_cparams(("arbitrary", "arbitrary"), vmem),
        name="moe_ffn",
    )(tile_expert, n_live, row_token, row_token, row_dest, row_dest, h, g.reshape(1, d), wg, wu, wd)


def _combine_kernel(y0_ref, y1_ref, h_ref, gate_ref, fg_ref, o_ref):
    gate = gate_ref[...]
    out = h_ref[...] + gate[:, 0:1] * y0_ref[...] + gate[:, 1:2] * y1_ref[...]
    o_ref[...] = _rms(out, fg_ref[...], 1e-6)


def moe_combine(y, h, gates, final_gain, *, tc=COMBINE_ROWS):
    seq, d = h.shape
    n_tiles = seq // tc
    row = lambda w, off: pl.BlockSpec((tc, w), lambda i: (i + off, 0))
    return pl.pallas_call(
        _combine_kernel,
        grid=(n_tiles,),
        in_specs=[row(d, 0), row(d, n_tiles), row(d, 0), row(V7X_LANES, 0),
                  pl.BlockSpec((1, d), lambda i: (0, 0))],
        out_specs=row(d, 0),
        out_shape=jax.ShapeDtypeStruct((seq, d), F32),
        compiler_params=_cparams(("parallel",), 10 * tc * d * 4 + (4 << 20)),
        name="moe_combine",
    )(y, y, h, gates, final_gain.reshape(1, d))


def _swap_halves(w):
    half = w.shape[-1] // 2
    return jnp.concatenate([w[..., half:], w[..., :half]], axis=-1)


def _mla_weights(w_uq, w_ukv):
    lora = w_uq.shape[0]
    uq = w_uq.reshape(lora, B_HEADS, B_NOPE + B_ROPE)
    nope, pe = uq[..., :B_NOPE], uq[..., B_NOPE:]
    zpad = jnp.zeros((lora, B_HEADS, 128 - B_ROPE), w_uq.dtype)
    wq1 = jnp.concatenate([nope, pe, zpad], axis=-1).reshape(lora, B_HEADS * 256)
    wq2 = jnp.concatenate([_swap_halves(pe), zpad], axis=-1).reshape(lora, B_HEADS * 128)
    ukv = w_ukv.reshape(lora, B_HEADS, B_NOPE + B_V)
    wkn = ukv[..., :B_NOPE].reshape(lora, B_HEADS * B_NOPE)
    wv = ukv[..., B_NOPE:].reshape(lora, B_HEADS * B_V)
    return wq1.astype(BF16), wq2.astype(BF16), wkn.astype(BF16), wv.astype(BF16)


def _rope_tables(seq):
    inv = ROPE_THETA ** (-jnp.arange(0, B_ROPE, 2, dtype=F32) / B_ROPE)
    ang = jnp.arange(seq, dtype=F32)[:, None] * inv[None, :]
    cos, sin = jnp.cos(ang), jnp.sin(ang)
    zeros = jnp.zeros((seq, 128 - B_ROPE), F32)
    return (jnp.concatenate([cos, cos, zeros], axis=1), jnp.concatenate([-sin, sin, zeros], axis=1))


def _dispatch_plan(idx, rank, counts, tm, n_rows):
    seq = idx.shape[0]
    n_tiles = seq * TOP_K // tm + N_EXPERTS
    cnt = counts[0, :N_EXPERTS].astype(jnp.int32)
    tiles_per = (cnt + tm - 1) // tm
    tile_end = jnp.cumsum(tiles_per)
    row_start = (tile_end - tiles_per) * tm
    pos = (row_start[idx[:, :TOP_K]] + rank[:, :TOP_K]).reshape(-1)
    entry = (pos // tm) * n_rows + pos % tm
    tile_id = jnp.arange(n_tiles, dtype=jnp.int32)
    spare = (TOP_K * seq + (tile_id[:, None] % 2) * n_rows
             + jnp.arange(n_rows, dtype=jnp.int32)[None, :]).reshape(-1)
    pair = jnp.full((n_tiles * n_rows,), -1, jnp.int32).at[entry].set(
        jnp.arange(seq * TOP_K, dtype=jnp.int32))
    tok, slot = pair // TOP_K, pair % TOP_K
    row_token = jnp.where(pair >= 0, tok, 0)
    row_dest = jnp.where(pair >= 0, slot * seq + tok, spare)
    n_live = tile_end[-1]
    tile_expert = jnp.sum((tile_id[:, None] >= tile_end[None, :]).astype(jnp.int32), axis=1)
    last_expert = jnp.sum((n_live - 1 >= tile_end).astype(jnp.int32))
    tile_expert = jnp.where(tile_id < n_live, tile_expert, last_expert).astype(jnp.int32)
    shape3 = (n_tiles, 1, n_rows)
    return (row_token.reshape(shape3), row_dest.reshape(shape3), tile_expert,
            n_live.reshape(1).astype(jnp.int32), TOP_K * seq + 2 * n_rows)


def kernel(x, ev_attn_norm, ev_w_in, ev_q_norm, ev_w_uq, ev_kv_norm, ev_w_ukv, ev_lambda_q1, ev_lambda_k1, ev_lambda_q2, ev_lambda_k2, ev_subln, ev_w_out, ev_ffn_norm, ev_ffn_w_gate, ev_ffn_w_up, ev_ffn_w_down, od_attn_norm, od_w_in, od_forget_bias, od_w_out, od_ffn_norm, od_router, od_moe_w_gate, od_moe_w_up, od_moe_w_down, final_norm):
    batch, seq, d = x.shape
    assert batch == 1
    h = x.reshape(seq, d)

    lambda_init = 0.8 - 0.6 * math.exp(-0.3 * 0)
    w_in = ev_w_in[0]
    n_a = 2 * A_HEADS * 2 * A_QK + A_HEADS * A_V
    w_a = w_in[:, :n_a].astype(BF16)
    scale_a = jnp.concatenate([jnp.full((A_HEADS * 2 * A_QK,), A_QK ** -0.5 * LOG2E, F32),
                               jnp.ones((n_a - A_HEADS * 2 * A_QK,), F32)])
    w_b = jnp.concatenate([w_in[:, n_a:], _swap_halves(w_in[:, -B_ROPE:])], axis=1).astype(BF16)
    qkv_a = rms_matmul(h, ev_attn_norm[0], w_a, scale_a, BF16)
    c_b = rms_matmul(h, ev_attn_norm[0], w_b, jnp.ones((w_b.shape[1],), F32), F32, tn=w_b.shape[1])
    oa, (ffn_wg, ffn_wu, ffn_wd) = diff_attention(
        qkv_a, ev_lambda_q1[0], ev_lambda_k1[0], ev_lambda_q2[0], ev_lambda_k2[0], ev_subln[0],
        lambda_init, [(ev_ffn_w_gate[0], FFN_TF), (ev_ffn_w_up[0], FFN_TF), (ev_ffn_w_down[0], None)])
    wq1, wq2, wkn, wv = _mla_weights(ev_w_uq[0], ev_w_ukv[0])
    cos_t, sin_t = _rope_tables(seq)
    q_b, k_b, v_b = mla_prep(c_b, ev_q_norm[0], ev_kv_norm[0], wq1, wq2, wkn, wv, cos_t, sin_t)
    ob = mla_attention(q_b, k_b, v_b)
    w_out = ev_w_out[0].astype(BF16)
    h = proj_residual([oa, ob], [w_out[:A_HEADS * A_V], w_out[A_HEADS * A_V:]], h)
    h = dense_ffn(h, ev_ffn_norm[0], ffn_wg, ffn_wu, ffn_wd)

    width = C_HEADS * C_DIM
    w_qkv = od_w_in[0][:, :3 * width].astype(BF16)
    scale_c = jnp.concatenate([jnp.full((width,), C_DIM ** -0.5 * LOG2E, F32), jnp.ones((2 * width,), F32)])
    qkv_c = rms_matmul(h, od_attn_norm[0], w_qkv, scale_c, BF16)
    wf_t = od_w_in[0][:, 3 * width:].T.astype(BF16)
    cum = forget_gate_cumsum(h, od_attn_norm[0], wf_t, od_forget_bias[0])
    n_e, _, ff = od_moe_w_gate[0].shape
    oc, (moe_wg, moe_wu, moe_wd) = fox_attention(qkv_c, cum, [
        (od_moe_w_gate[0].reshape(n_e * d, ff), FFN_TF), (od_moe_w_up[0].reshape(n_e * d, ff), FFN_TF),
        (od_moe_w_down[0].reshape(n_e * ff, d), None)])
    h = proj_residual([oc], [od_w_out[0].astype(BF16)], h)

    w_router = jnp.zeros((d, V7X_LANES), F32).at[:, :N_EXPERTS].set(od_router[0])
    idx, gates, rank, counts = moe_router(h, od_ffn_norm[0], w_router)
    n_rows = _moe_tile_rows(MOE_TM, od_moe_w_gate.shape[-1] // FFN_TF)
    row_token, row_dest, tile_expert, n_live, n_y_rows = _dispatch_plan(idx, rank, counts, MOE_TM, n_rows)
    blocked = (ff // FFN_TF, n_e, d, FFN_TF)
    y = moe_ffn(h, row_token, row_dest, n_y_rows, od_ffn_norm[0], moe_wg.reshape(blocked),
                moe_wu.reshape(blocked), moe_wd.reshape(n_e, ff, d), tile_expert, n_live)
    out = moe_combine(y, h, gates, final_norm)
    return out.reshape(batch, seq, d)
```

```python
import functools
import math

import jax
import jax.numpy as jnp
from jax import lax
from jax.experimental import pallas as pl
from jax.experimental.pallas import tpu as pltpu

F32 = jnp.float32
BF16 = jnp.bfloat16

V7X_LANES = 128
V7X_VMEM_REQUEST_CAP = 60000 * 1024

A_HEADS, A_QK, A_V = 8, 64, 128
B_HEADS, B_LORA, B_NOPE, B_ROPE, B_V = 8, 512, 128, 64, 128
B_QK_PAD = 256
C_HEADS, C_DIM = 16, 128
N_EXPERTS, TOP_K = 8, 2
ROPE_THETA = 10000.0
MASK_VALUE = -1e30
LOG2E = math.log2(math.e)

ATTN_BLOCK = 1024
DIFF_ATTN_BLOCK = 1024
LONG_TRIP_PAIRS = 4
SOFTMAX_ROWS = 128
PROJ_TM, PROJ_TN = 1024, 1024
FFN_TM, FFN_TF = 512, 512
MOE_TM = 512
COMBINE_ROWS = 256
PREP_TM = 512
GATE_TM = 1024
ROUTER_TM = 512


def _cparams(semantics, vmem_bytes):
    return pltpu.CompilerParams(
        dimension_semantics=semantics,
        vmem_limit_bytes=int(min(V7X_VMEM_REQUEST_CAP, vmem_bytes)))


def _rms(x, g, eps):
    return x * lax.rsqrt(jnp.mean(x * x, axis=-1, keepdims=True) + eps) * g


def _rms_matmul_kernel(x_ref, g_ref, w_ref, cs_ref, o_ref, xn_ref, *, eps):
    @pl.when(pl.program_id(1) == 0)
    def _():
        xn_ref[...] = _rms(x_ref[...], g_ref[...], eps).astype(BF16)

    acc = jnp.dot(xn_ref[...], w_ref[...], preferred_element_type=F32)
    o_ref[...] = (acc * cs_ref[...]).astype(o_ref.dtype)


def rms_matmul(x, g, w, col_scale, out_dtype, *, eps=1e-6, tm=PROJ_TM, tn=PROJ_TN):
    m, k = x.shape
    n = w.shape[1]
    assert m % tm == 0 and n % tn == 0, (m, n, tm, tn)
    vmem = 2 * tm * k * 4 + tm * k * 2 + 2 * k * tn * 2 + 4 * tm * tn * 4 + (4 << 20)
    return pl.pallas_call(
        functools.partial(_rms_matmul_kernel, eps=eps),
        grid=(m // tm, n // tn),
        in_specs=[
            pl.BlockSpec((tm, k), lambda i, j: (i, 0)),
            pl.BlockSpec((1, k), lambda i, j: (0, 0)),
            pl.BlockSpec((k, tn), lambda i, j: (0, j)),
            pl.BlockSpec((1, tn), lambda i, j: (0, j)),
        ],
        out_specs=pl.BlockSpec((tm, tn), lambda i, j: (i, j)),
        out_shape=jax.ShapeDtypeStruct((m, n), out_dtype),
        scratch_shapes=[pltpu.VMEM((tm, k), BF16)],
        compiler_params=_cparams(("parallel", "arbitrary"), vmem),
        name="rms_matmul",
    )(x, g.reshape(1, k), w, col_scale.reshape(1, n))


def _proj_residual_kernel(*refs, n_in):
    a_refs, w_refs = refs[:n_in], refs[n_in:2 * n_in]
    h_ref, o_ref = refs[2 * n_in], refs[2 * n_in + 1]
    acc = h_ref[...]
    for a_ref, w_ref in zip(a_refs, w_refs):
        acc = acc + jnp.dot(a_ref[...], w_ref[...], preferred_element_type=F32)
    o_ref[...] = acc


def proj_residual(a_list, w_list, h, *, tm=PROJ_TM, tn=PROJ_TN):
    m, n = h.shape
    n_in = len(a_list)
    assert m % tm == 0 and n % tn == 0
    ks = [a.shape[1] for a in a_list]
    vmem = sum(2 * tm * k * 2 + 2 * k * tn * 2 for k in ks) + 6 * tm * tn * 4 + (4 << 20)
    in_specs = [pl.BlockSpec((tm, k), lambda i, j: (i, 0)) for k in ks]
    in_specs += [pl.BlockSpec((k, tn), lambda i, j: (0, j)) for k in ks]
    in_specs += [pl.BlockSpec((tm, tn), lambda i, j: (i, j))]
    return pl.pallas_call(
        functools.partial(_proj_residual_kernel, n_in=n_in),
        grid=(m // tm, n // tn),
        in_specs=in_specs,
        out_specs=pl.BlockSpec((tm, tn), lambda i, j: (i, j)),
        out_shape=jax.ShapeDtypeStruct((m, n), F32),
        compiler_params=_cparams(("parallel", "arbitrary"), vmem),
        name="proj_residual",
    )(*a_list, *w_list, h)


def _attn_stream(q, k_ref, v_ref, bias_fn, scratch, *, bq):
    s_bufs, p_bufs, a_bufs = scratch[0:2], scratch[2:4], scratch[4:6]
    m_scr, l_scr, acc_scr = scratch[6:9]
    rows, bk = s_bufs[0].shape
    assert bq == 2 * bk
    n_lane_chunks = bk // V7X_LANES
    m_scr[...] = jnp.full(m_scr.shape, MASK_VALUE, F32)
    l_scr[...] = jnp.zeros(l_scr.shape, F32)
    acc_scr[...] = jnp.zeros(acc_scr.shape, F32)
    a_bufs[1][...] = jnp.ones(a_bufs[1].shape, F32)
    p_bufs[1][...] = jnp.zeros(p_bufs[1].shape, BF16)

    def scores(j, slot):
        start = pl.multiple_of(j * bk, bk)
        s = lax.dot_general(q, k_ref[pl.ds(start, bk), :], (((1,), (1,)), ((), ())),
                            preferred_element_type=F32)
        s_bufs[slot][...] = s if bias_fn is None else s + bias_fn(start)

    def softmax(slot, diag_offset=None):
        s_buf, p_buf, a_buf = s_bufs[slot], p_bufs[slot], a_bufs[slot]
        for r0 in range(0, rows, SOFTMAX_ROWS):
            sl = slice(r0, r0 + SOFTMAX_ROWS)
            first_row = r0 % bq
            if diag_offset is not None and first_row + SOFTMAX_ROWS - 1 < diag_offset:
                continue
            s = s_buf[sl, :]
            if diag_offset is not None and first_row < diag_offset + bk - 1:
                row = lax.broadcasted_iota(jnp.int32, s.shape, 0) + first_row
                col = lax.broadcasted_iota(jnp.int32, s.shape, 1) + diag_offset
                s = jnp.where(col <= row, s, MASK_VALUE)
            chunks = [s[:, i * V7X_LANES:(i + 1) * V7X_LANES] for i in range(n_lane_chunks)]
            m_lane = chunks[0]
            for x in chunks[1:]:
                m_lane = jnp.maximum(m_lane, x)
            m_prev = m_scr[sl, :]
            m_next = jnp.maximum(m_prev, jnp.max(m_lane, axis=1, keepdims=True))
            alpha = jnp.exp2(m_prev - m_next)
            l_new = alpha * l_scr[sl, :]
            for i, x in enumerate(chunks):
                p = jnp.exp2(x - m_next)
                l_new = l_new + p
                p_buf[sl, i * V7X_LANES:(i + 1) * V7X_LANES] = p.astype(BF16)
            l_scr[sl, :] = l_new
            m_scr[sl, :] = m_next
            a_buf[sl, :] = alpha

    def pv(j, slot, first_live_row=0):
        start = pl.multiple_of(j * bk, bk)
        v = v_ref[pl.ds(start, bk), :]
        live = ([slice(0, rows)] if first_live_row == 0 else
                [slice(r0 + first_live_row, r0 + bq) for r0 in range(0, rows, bq)])
        for sl in live:
            acc_scr[sl, :] = acc_scr[sl, :] * a_bufs[slot][sl, :] + jnp.dot(
                p_bufs[slot][sl, :], v, preferred_element_type=F32)

    def result():
        return acc_scr[...] / jnp.sum(l_scr[...], axis=1, keepdims=True)

    return scores, softmax, pv, result


def _flash_attend(heads, qi, scratch, *, bq, pairs_per_trip=2):
    bk = bq // 2
    per_head = len(scratch) // len(heads)
    streams = [_attn_stream(*head, scratch[t * per_head:(t + 1) * per_head], bq=bq)
               for t, head in enumerate(heads)]

    def each(stage, *args, **kwargs):
        for stream in streams:
            stream[stage](*args, **kwargs)

    SCORES, SOFTMAX, PV = 0, 1, 2

    def two_steps(j, diag):
        each(PV, jnp.maximum(j - 1, 0), 1)
        each(SOFTMAX, 0, 0 if diag else None)
        each(SCORES, j + 1, 1)
        each(PV, j, 0)
        each(SOFTMAX, 1, bk if diag else None)
        if not diag:
            each(SCORES, j + 2, 0)

    each(SCORES, 0, 0)

    done = 0
    pairs = pairs_per_trip
    while pairs >= 1:
        def body(i, carry, pairs=pairs, done=done):
            for u in range(pairs):
                two_steps(2 * (done + pairs * i + u), False)
            return carry

        n_trips = (qi - done) // pairs
        lax.fori_loop(0, n_trips, body, 0)
        done = done + n_trips * pairs
        pairs //= 2
    two_steps(2 * qi, True)
    each(PV, 2 * qi + 1, 1, first_live_row=bk)
    return [stream[3]() for stream in streams]


def _attn_scratch(rows, bq):
    bk = bq // 2
    stat = lambda: pltpu.VMEM((rows, V7X_LANES), F32)
    s_buf = lambda: pltpu.VMEM((rows, bk), F32)
    p_buf = lambda: pltpu.VMEM((rows, bk), BF16)
    return [s_buf(), s_buf(), p_buf(), p_buf()] + [stat() for _ in range(5)]


def _attn_vmem(seq, rows, bq, kv_width):
    resident = 2 * seq * kv_width * 2
    buffers = 2 * rows * (bq // 2) * (4 + 2) + 5 * rows * V7X_LANES * 4
    temps = 2 * rows * (bq // 2) * 4
    return resident + buffers + temps + (6 << 20)


def _side_cast_specs(weights, n_steps, step_index):
    in_specs, out_specs, out_shapes, vmem = [], [], [], 0
    for w, col_block in weights:
        rows, cols = w.shape
        block = next(b for b in range(16, rows + 1, 16) if rows % b == 0 and b * n_steps >= rows)
        last = rows // block - 1

        def row_block(*grid_idx, last=last):
            return jnp.minimum(step_index(*grid_idx), last)

        in_specs.append(pl.BlockSpec((block, cols), lambda *g, rb=row_block: (rb(*g), 0)))
        if col_block is None:
            out_specs.append(pl.BlockSpec((block, cols), lambda *g, rb=row_block: (rb(*g), 0)))
            out_shapes.append(jax.ShapeDtypeStruct((rows, cols), BF16))
        else:
            out_specs.append(pl.BlockSpec((cols // col_block, block, col_block),
                                          lambda *g, rb=row_block: (0, rb(*g), 0)))
            out_shapes.append(jax.ShapeDtypeStruct((cols // col_block, rows, col_block), BF16))
        vmem += 2 * block * cols * (4 + 2)
    return in_specs, out_specs, out_shapes, vmem


def _side_cast(in_refs, out_refs):
    for src, dst in zip(in_refs, out_refs):
        if len(dst.shape) == 2:
            dst[...] = src[...].astype(dst.dtype)
        else:
            width = dst.shape[2]
            for c in range(dst.shape[0]):
                dst[c] = src[:, c * width:(c + 1) * width].astype(dst.dtype)


def _diff_attn_kernel(slope_ref, q_ref, k_ref, v_ref, lq1_ref, lk1_ref, lq2_ref, lk2_ref,
                      subln_ref, *rest, bq, lambda_init, n_side):
    o_ref, scratch = rest[n_side], rest[2 * n_side + 1:]
    _side_cast(rest[:n_side], rest[n_side + 1:2 * n_side + 1])
    h, qi = pl.program_id(0), pl.program_id(1)
    q = q_ref[...]
    lane = lax.broadcasted_iota(jnp.int32, q.shape, 1)
    zero = jnp.zeros_like(q)
    qs = jnp.concatenate([jnp.where(lane < A_QK, q, zero), jnp.where(lane >= A_QK, q, zero)], axis=0)
    slope = slope_ref[h] * LOG2E
    t0 = qi * bq

    def bias_fn(start):
        kpos = lax.broadcasted_iota(jnp.int32, (1, bq // 2), 1) + (start - t0)
        return slope * kpos.astype(F32)

    o, = _flash_attend([(qs, k_ref, v_ref, bias_fn)], qi, scratch, bq=bq)
    lam = (jnp.exp(jnp.sum(lq1_ref[...] * lk1_ref[...], axis=1, keepdims=True))
           - jnp.exp(jnp.sum(lq2_ref[...] * lk2_ref[...], axis=1, keepdims=True)) + lambda_init)
    d = o[:bq] - lam * o[bq:]
    o_ref[...] = (_rms(d, subln_ref[...], 1e-5) * (1.0 - lambda_init)).astype(o_ref.dtype)


def diff_attention(qkv, lq1, lk1, lq2, lk2, subln, lambda_init, side_weights, *, bq=DIFF_ATTN_BLOCK):
    seq = qkv.shape[0]
    nq = seq // bq
    slopes = jnp.exp2(-8.0 * jnp.arange(1, A_HEADS + 1, dtype=F32) / A_HEADS)
    vec = lambda: pl.BlockSpec((1, A_QK), lambda h, i, s: (0, 0))
    side_in, side_out, side_shapes, side_vmem = _side_cast_specs(
        side_weights, A_HEADS * nq, lambda h, i, s: h * nq + i)
    grid_spec = pltpu.PrefetchScalarGridSpec(
        num_scalar_prefetch=1,
        grid=(A_HEADS, nq),
        in_specs=[
            pl.BlockSpec((bq, 2 * A_QK), lambda h, i, s: (i, h)),
            pl.BlockSpec((seq, 2 * A_QK), lambda h, i, s: (0, A_HEADS + h)),
            pl.BlockSpec((seq, A_V), lambda h, i, s: (0, 2 * A_HEADS + h)),
            vec(), vec(), vec(), vec(),
            pl.BlockSpec((1, A_V), lambda h, i, s: (0, 0)),
        ] + side_in,
        out_specs=[pl.BlockSpec((bq, A_V), lambda h, i, s: (i, h))] + side_out,
        scratch_shapes=_attn_scratch(2 * bq, bq),
    )
    out = pl.pallas_call(
        functools.partial(_diff_attn_kernel, bq=bq, lambda_init=lambda_init, n_side=len(side_weights)),
        grid_spec=grid_spec,
        out_shape=[jax.ShapeDtypeStruct((seq, A_HEADS * A_V), BF16)] + side_shapes,
        compiler_params=_cparams(("arbitrary", "arbitrary"), _attn_vmem(seq, 2 * bq, bq, 2 * A_QK + A_V) + side_vmem),
        name="diff_attention",
    )(slopes, qkv, qkv, qkv, lq1.reshape(1, -1), lk1.reshape(1, -1), lq2.reshape(1, -1),
      lk2.reshape(1, -1), subln.reshape(1, -1), *[w for w, _ in side_weights])
    return out[0], out[1:]


def _mla_prep_kernel(c_ref, qn_ref, kvn_ref, wq1_ref, wq2_ref, wkn_ref, wv_ref, cos_ref, sin_ref,
                     q_ref, k_ref, v_ref, *, scale):
    c = c_ref[...]
    cqn = _rms(c[:, :B_LORA], qn_ref[...], 1e-6).astype(BF16)
    ckvn = _rms(c[:, B_LORA:2 * B_LORA], kvn_ref[...], 1e-6).astype(BF16)
    kp = c[:, 2 * B_LORA:]
    cos, sin = cos_ref[...], sin_ref[...]
    qa = jnp.dot(cqn, wq1_ref[...], preferred_element_type=F32)
    qb = jnp.dot(cqn, wq2_ref[...], preferred_element_type=F32)
    kn = jnp.dot(ckvn, wkn_ref[...], preferred_element_type=F32)
    v_ref[...] = jnp.dot(ckvn, wv_ref[...], preferred_element_type=F32).astype(v_ref.dtype)
    kr = (kp * cos + pltpu.roll(kp, B_ROPE, axis=1) * sin).astype(k_ref.dtype)
    for h in range(B_HEADS):
        lo = B_QK_PAD * h
        hi, end = lo + B_NOPE, lo + B_QK_PAD
        group = slice(V7X_LANES * h, V7X_LANES * (h + 1))
        q_ref[:, lo:hi] = (qa[:, lo:hi] * scale).astype(q_ref.dtype)
        pe = qa[:, hi:end] * cos + qb[:, group] * sin
        q_ref[:, hi:end] = (pe * scale).astype(q_ref.dtype)
        k_ref[:, lo:hi] = kn[:, group].astype(k_ref.dtype)
        k_ref[:, hi:end] = kr


def mla_prep(c, q_norm, kv_norm, wq1, wq2, wkn, wv, cos_t, sin_t, *, tm=PREP_TM):
    seq = c.shape[0]
    scale = (B_NOPE + B_ROPE) ** -0.5 * LOG2E
    full = lambda a: pl.BlockSpec(a.shape, lambda i: (0, 0))
    row = lambda w: pl.BlockSpec((tm, w), lambda i: (i, 0))
    qn, kvn = q_norm.reshape(1, -1), kv_norm.reshape(1, -1)
    qk_w, v_w = B_HEADS * B_QK_PAD, B_HEADS * B_V
    out_w = 2 * qk_w + v_w
    weights = 2 * 2 * (wq1.size + wq2.size + wkn.size + wv.size)
    vmem = (weights + 2 * tm * (c.shape[1] * 4 + 2 * V7X_LANES * 4 + out_w * 2)
            + 2 * tm * out_w * 4 + (6 << 20))
    return pl.pallas_call(
        functools.partial(_mla_prep_kernel, scale=scale),
        grid=(seq // tm,),
        in_specs=[row(c.shape[1]), full(qn), full(kvn), full(wq1), full(wq2), full(wkn), full(wv),
                  row(V7X_LANES), row(V7X_LANES)],
        out_specs=[row(qk_w), row(qk_w), row(v_w)],
        out_shape=[jax.ShapeDtypeStruct((seq, qk_w), BF16), jax.ShapeDtypeStruct((seq, qk_w), BF16),
                   jax.ShapeDtypeStruct((seq, v_w), BF16)],
        compiler_params=_cparams(("parallel",), vmem),
        name="mla_prep",
    )(c, qn, kvn, wq1, wq2, wkn, wv, cos_t, sin_t)


def _mla_attn_kernel(q_ref, k_ref, v_ref, o_ref, *scratch, bq):
    o, = _flash_attend([(q_ref[...], k_ref, v_ref, None)], pl.program_id(1), scratch, bq=bq,
                       pairs_per_trip=LONG_TRIP_PAIRS)
    o_ref[...] = o.astype(o_ref.dtype)


def mla_attention(q, k, v, *, bq=ATTN_BLOCK):
    seq = q.shape[0]
    return pl.pallas_call(
        functools.partial(_mla_attn_kernel, bq=bq),
        grid=(B_HEADS, seq // bq),
        in_specs=[
            pl.BlockSpec((bq, B_QK_PAD), lambda h, i: (i, h)),
            pl.BlockSpec((seq, B_QK_PAD), lambda h, i: (0, h)),
            pl.BlockSpec((seq, B_V), lambda h, i: (0, h)),
        ],
        out_specs=pl.BlockSpec((bq, B_V), lambda h, i: (i, h)),
        out_shape=jax.ShapeDtypeStruct((seq, B_HEADS * B_V), BF16),
        scratch_shapes=_attn_scratch(bq, bq),
        compiler_params=_cparams(("parallel", "arbitrary"), _attn_vmem(seq, bq, bq, B_QK_PAD + B_V)),
        name="mla_attention",
    )(q, k, v)


def _forget_gate_kernel(h_ref, g_ref, wf_ref, b_ref, cum_ref, carry_scr):
    @pl.when(pl.program_id(0) == 0)
    def _():
        carry_scr[...] = jnp.zeros(carry_scr.shape, F32)

    xn = _rms(h_ref[...], g_ref[...], 1e-6).astype(BF16)
    f = lax.dot_general(wf_ref[...], xn, (((1,), (1,)), ((), ())), preferred_element_type=F32)
    z = f + b_ref[...]
    log_f = jnp.minimum(z, 0.0) - jnp.log(1.0 + jnp.exp(-jnp.abs(z)))
    r = lax.broadcasted_iota(jnp.int32, (V7X_LANES, V7X_LANES), 0)
    c = lax.broadcasted_iota(jnp.int32, (V7X_LANES, V7X_LANES), 1)
    tri = (r <= c).astype(F32)
    carry = carry_scr[...]
    for j in range(log_f.shape[1] // V7X_LANES):
        sl = slice(j * V7X_LANES, (j + 1) * V7X_LANES)
        loc = jnp.dot(log_f[:, sl], tri, preferred_element_type=F32,
                      precision=lax.Precision.HIGHEST) + carry
        cum_ref[:, sl] = loc
        carry = jnp.broadcast_to(loc[:, V7X_LANES - 1:], carry.shape)
    carry_scr[...] = carry


def forget_gate_cumsum(h, g, wf_t, bias, *, tm=GATE_TM):
    seq, d = h.shape
    nh = wf_t.shape[0]
    return pl.pallas_call(
        _forget_gate_kernel,
        grid=(seq // tm,),
        in_specs=[
            pl.BlockSpec((tm, d), lambda i: (i, 0)),
            pl.BlockSpec((1, d), lambda i: (0, 0)),
            pl.BlockSpec((nh, d), lambda i: (0, 0)),
            pl.BlockSpec((nh, 1), lambda i: (0, 0)),
        ],
        out_specs=pl.BlockSpec((nh, tm), lambda i: (0, i)),
        out_shape=jax.ShapeDtypeStruct((nh, seq), F32),
        scratch_shapes=[pltpu.VMEM((nh, V7X_LANES), F32)],
        compiler_params=_cparams(("arbitrary",), 2 * tm * d * 4 + 3 * tm * d * 4 + (4 << 20)),
        name="forget_gate_cumsum",
    )(h, g.reshape(1, d), wf_t, bias.reshape(nh, 1))


def _fox_attn_kernel(q_ref, k_ref, v_ref, cum_ref, *rest, bq, n_side):
    o_ref, scratch = rest[n_side], rest[2 * n_side + 1:]
    _side_cast(rest[:n_side], rest[n_side + 1:2 * n_side + 1])
    qi = pl.program_id(1)
    t0 = pl.multiple_of(qi * bq, bq)
    c0 = cum_ref[:, pl.ds(t0, V7X_LANES)][:, :1]

    def bias_fn(start):
        return (c0 - cum_ref[:, pl.ds(start, bq // 2)]) * LOG2E

    o, = _flash_attend([(q_ref[...], k_ref, v_ref, bias_fn)], qi, scratch, bq=bq,
                       pairs_per_trip=LONG_TRIP_PAIRS)
    o_ref[...] = o.astype(o_ref.dtype)


def fox_attention(qkv, cum, side_weights, *, bq=ATTN_BLOCK):
    seq = qkv.shape[0]
    nq = seq // bq
    cum3 = cum.reshape(C_HEADS, 1, seq)
    side_in, side_out, side_shapes, side_vmem = _side_cast_specs(
        side_weights, C_HEADS * nq, lambda h, i: h * nq + i)
    out = pl.pallas_call(
        functools.partial(_fox_attn_kernel, bq=bq, n_side=len(side_weights)),
        grid=(C_HEADS, nq),
        in_specs=[
            pl.BlockSpec((bq, C_DIM), lambda h, i: (i, h)),
            pl.BlockSpec((seq, C_DIM), lambda h, i: (0, C_HEADS + h)),
            pl.BlockSpec((seq, C_DIM), lambda h, i: (0, 2 * C_HEADS + h)),
            pl.BlockSpec((None, 1, seq), lambda h, i: (h, 0, 0)),
        ] + side_in,
        out_specs=[pl.BlockSpec((bq, C_DIM), lambda h, i: (i, h))] + side_out,
        out_shape=[jax.ShapeDtypeStruct((seq, C_HEADS * C_DIM), BF16)] + side_shapes,
        scratch_shapes=_attn_scratch(bq, bq),
        compiler_params=_cparams(("arbitrary", "arbitrary"),
                                 _attn_vmem(seq, bq, bq, 2 * C_DIM) + side_vmem),
        name="fox_attention",
    )(qkv, qkv, qkv, cum3, *[w for w, _ in side_weights])
    return out[0], out[1:]


def _swiglu_accumulate(xn, wg, wu, wd, acc_ref):
    gate = jnp.dot(xn, wg, preferred_element_type=F32)
    up = jnp.dot(xn, wu, preferred_element_type=F32)
    act = (gate * jax.nn.sigmoid(gate) * up).astype(BF16)
    acc_ref[...] += jnp.dot(act, wd, preferred_element_type=F32)


def _ffn_kernel(h_ref, g_ref, wg_ref, wu_ref, wd_ref, o_ref, xn_ref, acc_ref):
    f = pl.program_id(1)

    @pl.when(f == 0)
    def _():
        xn_ref[...] = _rms(h_ref[...], g_ref[...], 1e-6).astype(BF16)
        acc_ref[...] = jnp.zeros(acc_ref.shape, F32)

    _swiglu_accumulate(xn_ref[...], wg_ref[...], wu_ref[...], wd_ref[...], acc_ref)

    @pl.when(f == pl.num_programs(1) - 1)
    def _():
        o_ref[...] = h_ref[...] + acc_ref[...]


def _ffn_vmem(tm, d, tf):
    return 4 * tm * d * 4 + tm * d * 2 + tm * d * 4 + 3 * 2 * d * tf * 2 + 4 * tm * tf * 4 + (6 << 20)


def dense_ffn(h, g, wg, wu, wd, *, tm=FFN_TM):
    m, d = h.shape
    n_f, _, tf = wg.shape
    assert m % tm == 0 and wd.shape[0] == n_f * tf
    return pl.pallas_call(
        _ffn_kernel,
        grid=(m // tm, n_f),
        in_specs=[
            pl.BlockSpec((tm, d), lambda i, f: (i, 0)),
            pl.BlockSpec((1, d), lambda i, f: (0, 0)),
            pl.BlockSpec((None, d, tf), lambda i, f: (f, 0, 0)),
            pl.BlockSpec((None, d, tf), lambda i, f: (f, 0, 0)),
            pl.BlockSpec((tf, d), lambda i, f: (f, 0)),
        ],
        out_specs=pl.BlockSpec((tm, d), lambda i, f: (i, 0)),
        out_shape=jax.ShapeDtypeStruct((m, d), F32),
        scratch_shapes=[pltpu.VMEM((tm, d), BF16), pltpu.VMEM((tm, d), F32)],
        compiler_params=_cparams(("parallel", "arbitrary"), _ffn_vmem(tm, d, tf)),
        name="dense_ffn",
    )(h, g.reshape(1, d), wg, wu, wd)


def _router_kernel(h_ref, g_ref, wr_ref, idx_ref, gate_ref, rank_ref, cnt_ref, cnt_scr):
    @pl.when(pl.program_id(0) == 0)
    def _():
        cnt_scr[...] = jnp.zeros(cnt_scr.shape, F32)

    tm = h_ref.shape[0]
    xn = _rms(h_ref[...], g_ref[...], 1e-6)
    logits = jnp.dot(xn, wr_ref[...], preferred_element_type=F32, precision=lax.Precision.HIGHEST)
    lane = lax.broadcasted_iota(jnp.int32, logits.shape, 1)
    lane_f = lane.astype(F32)
    valid = lane < N_EXPERTS
    logits = jnp.where(valid, logits, MASK_VALUE)
    e = jnp.exp(logits - jnp.max(logits, axis=1, keepdims=True))
    probs = jnp.where(valid, e / jnp.sum(e, axis=1, keepdims=True), -1.0)
    p1 = jnp.max(probs, axis=1, keepdims=True)
    i1 = jnp.min(jnp.where(probs == p1, lane_f, float(V7X_LANES)), axis=1, keepdims=True)
    rest = jnp.where(lane_f == i1, -1.0, probs)
    p2 = jnp.max(rest, axis=1, keepdims=True)
    i2 = jnp.min(jnp.where(rest == p2, lane_f, float(V7X_LANES)), axis=1, keepdims=True)
    hot1 = (lane_f == i1).astype(F32)
    hot2 = (lane_f == i2).astype(F32)
    r = lax.broadcasted_iota(jnp.int32, (tm, tm), 0)
    c = lax.broadcasted_iota(jnp.int32, (tm, tm), 1)
    strict_lower = (c < r).astype(BF16)
    before = jnp.dot(strict_lower, (hot1 + hot2).astype(BF16), preferred_element_type=F32)
    before = before + cnt_scr[...]
    rank1 = jnp.sum(before * hot1, axis=1, keepdims=True)
    rank2 = jnp.sum(before * hot2, axis=1, keepdims=True)
    cnt_scr[...] += jnp.sum(hot1 + hot2, axis=0, keepdims=True)
    cnt_ref[...] = cnt_scr[...]
    denom = p1 + p2
    idx_ref[...] = jnp.where(lane == 0, i1, jnp.where(lane == 1, i2, 0.0)).astype(jnp.int32)
    gate_ref[...] = jnp.where(lane == 0, p1 / denom, jnp.where(lane == 1, p2 / denom, 0.0))
    rank_ref[...] = jnp.where(lane == 0, rank1, jnp.where(lane == 1, rank2, 0.0)).astype(jnp.int32)


def moe_router(h, g, w_router_padded, *, tm=ROUTER_TM):
    seq, d = h.shape
    out = lambda dt: jax.ShapeDtypeStruct((seq, V7X_LANES), dt)
    row = pl.BlockSpec((tm, V7X_LANES), lambda i: (i, 0))
    return pl.pallas_call(
        _router_kernel,
        grid=(seq // tm,),
        in_specs=[
            pl.BlockSpec((tm, d), lambda i: (i, 0)),
            pl.BlockSpec((1, d), lambda i: (0, 0)),
            pl.BlockSpec((d, V7X_LANES), lambda i: (0, 0)),
        ],
        out_specs=[row, row, row, pl.BlockSpec((1, V7X_LANES), lambda i: (0, 0))],
        out_shape=[out(jnp.int32), out(F32), out(jnp.int32),
                   jax.ShapeDtypeStruct((1, V7X_LANES), F32)],
        scratch_shapes=[pltpu.VMEM((1, V7X_LANES), F32)],
        compiler_params=_cparams(("arbitrary",), 5 * tm * d * 4 + 4 * tm * tm * 4 + (6 << 20)),
        name="moe_router",
    )(h, g.reshape(1, d), w_router_padded)


def _moe_ffn_kernel(te_ref, nv_ref, rows_ref, next_rows_ref, dest_ref, prev_dest_ref, h_hbm, g_ref,
                    wg_ref, wu_ref, wd_ref, y_hbm, x_buf, y_buf, xn_ref, acc_ref, in_sem, out_sem):
    i, f = pl.program_id(0), pl.program_id(1)
    n_tiles, n_f = pl.num_programs(0), pl.num_programs(1)
    live = i < nv_ref[0]
    tm = xn_ref.shape[0]
    n_rows = x_buf.shape[1]
    rows_per_step = n_rows // n_f
    slot = i % 2

    def gather_copy(idx_ref, r, dst_slot):
        return pltpu.make_async_copy(h_hbm.at[pl.ds(idx_ref[0, 0, r], 1)],
                                     x_buf.at[dst_slot, pl.ds(r, 1)], in_sem.at[dst_slot])

    def scatter_copy(idx_ref, r, src_slot):
        src_row = jnp.minimum(r, tm - 1)
        return pltpu.make_async_copy(y_buf.at[src_slot, pl.ds(src_row, 1)],
                                     y_hbm.at[pl.ds(idx_ref[0, 0, r], 1)], out_sem.at[src_slot])

    def wait_gather(dst_slot):
        pltpu.make_async_copy(h_hbm.at[pl.ds(0, n_rows)], x_buf.at[dst_slot], in_sem.at[dst_slot]).wait()

    def wait_scatter(src_slot):
        pltpu.make_async_copy(y_buf.at[src_slot], y_hbm.at[pl.ds(0, n_rows)], out_sem.at[src_slot]).wait()

    def step_rows():
        return [f * rows_per_step + k for k in range(rows_per_step)]

    def gather_next_tile_rows():
        for r in step_rows():
            gather_copy(next_rows_ref, r, 1 - slot).start()

    def scatter_prev_tile_rows():
        for k, r in enumerate(step_rows()):
            scatter_copy(prev_dest_ref, r, 1 - slot).start(priority=k % 2)

    @pl.when((f == 0) & (i == 0))
    def _():
        def body(r, carry):
            gather_copy(rows_ref, r, slot).start()
            return carry
        lax.fori_loop(0, n_rows, body, 0)

    @pl.when(f == 0)
    def _():
        wait_gather(slot)
        xn_ref[...] = _rms(x_buf[slot, :tm], g_ref[...], 1e-6).astype(BF16)
        acc_ref[...] = jnp.zeros(acc_ref.shape, F32)

    @pl.when(live & (i > 0))
    def _():
        _swiglu_accumulate(xn_ref[...], wg_ref[...], wu_ref[...], wd_ref[...], acc_ref)
        gather_next_tile_rows()
        scatter_prev_tile_rows()

    @pl.when(live & (i == 0))
    def _():
        _swiglu_accumulate(xn_ref[...], wg_ref[...], wu_ref[...], wd_ref[...], acc_ref)
        gather_next_tile_rows()

    @pl.when(jnp.logical_not(live))
    def _():
        gather_next_tile_rows()
        scatter_prev_tile_rows()

    @pl.when(f == n_f - 1)
    def _():
        @pl.when(i >= 2)
        def _():
            wait_scatter(slot)
        y_buf[slot, :tm] = acc_ref[...]

    @pl.when((f == n_f - 1) & (i == n_tiles - 1))
    def _():
        def body(r, carry):
            scatter_copy(dest_ref, r, slot).start()
            return carry
        lax.fori_loop(0, n_rows, body, 0)
        wait_gather(1 - slot)
        wait_scatter(1 - slot)
        wait_scatter(slot)


def _moe_tile_rows(tm, nf):
    return -(-tm // (8 * nf)) * 8 * nf


def moe_ffn(h, row_token, row_dest, n_y_rows, g, wg, wu, wd, tile_expert, n_live, *, tm=MOE_TM):
    d = h.shape[1]
    nf, _, _, tf = wg.shape
    n_tiles, _, n_rows = row_token.shape
    assert wd.shape[1] == nf * tf and n_rows == _moe_tile_rows(tm, nf) and n_tiles >= 2

    def f_idx(i, f, nv):
        return jnp.where(i < nv[0], f, nf - 1)

    grid_spec = pltpu.PrefetchScalarGridSpec(
        num_scalar_prefetch=2,
        grid=(n_tiles, nf),
        in_specs=[
            pl.BlockSpec((1, 1, n_rows), lambda i, f, te, nv: (i, 0, 0), memory_space=pltpu.SMEM),
            pl.BlockSpec((1, 1, n_rows), lambda i, f, te, nv: (jnp.minimum(i + 1, n_tiles - 1), 0, 0),
                         memory_space=pltpu.SMEM),
            pl.BlockSpec((1, 1, n_rows), lambda i, f, te, nv: (i, 0, 0), memory_space=pltpu.SMEM),
            pl.BlockSpec((1, 1, n_rows), lambda i, f, te, nv: (jnp.maximum(i - 1, 0), 0, 0),
                         memory_space=pltpu.SMEM),
            pl.BlockSpec(memory_space=pl.ANY),
            pl.BlockSpec((1, d), lambda i, f, te, nv: (0, 0)),
            pl.BlockSpec((None, None, d, tf), lambda i, f, te, nv: (f_idx(i, f, nv), te[i], 0, 0)),
            pl.BlockSpec((None, None, d, tf), lambda i, f, te, nv: (f_idx(i, f, nv), te[i], 0, 0)),
            pl.BlockSpec((None, tf, d), lambda i, f, te, nv: (te[i], f_idx(i, f, nv), 0)),
        ],
        out_specs=pl.BlockSpec(memory_space=pl.ANY),
        scratch_shapes=[pltpu.VMEM((2, n_rows, d), F32), pltpu.VMEM((2, n_rows, d), F32),
                        pltpu.VMEM((tm, d), BF16), pltpu.VMEM((tm, d), F32),
                        pltpu.SemaphoreType.DMA((2,)), pltpu.SemaphoreType.DMA((2,))],
    )
    vmem = 4 * n_rows * d * 4 + tm * d * (2 + 4) + 3 * 2 * d * tf * 2 + 4 * tm * tf * 4 + (6 << 20)
    return pl.pallas_call(
        _moe_ffn_kernel,
        grid_spec=grid_spec,
        out_shape=jax.ShapeDtypeStruct((n_y_rows, d), F32),
        compiler_params=_cparams(("arbitrary", "arbitrary"), vmem),
        name="moe_ffn",
    )(tile_expert, n_live, row_token, row_token, row_dest, row_dest, h, g.reshape(1, d), wg, wu, wd)


def _combine_kernel(y0_ref, y1_ref, h_ref, gate_ref, fg_ref, o_ref):
    gate = gate_ref[...]
    out = h_ref[...] + gate[:, 0:1] * y0_ref[...] + gate[:, 1:2] * y1_ref[...]
    o_ref[...] = _rms(out, fg_ref[...], 1e-6)


def moe_combine(y, h, gates, final_gain, *, tc=COMBINE_ROWS):
    seq, d = h.shape
    n_tiles = seq // tc
    row = lambda w, off: pl.BlockSpec((tc, w), lambda i: (i + off, 0))
    return pl.pallas_call(
        _combine_kernel,
        grid=(n_tiles,),
        in_specs=[row(d, 0), row(d, n_tiles), row(d, 0), row(V7X_LANES, 0),
                  pl.BlockSpec((1, d), lambda i: (0, 0))],
        out_specs=row(d, 0),
        out_shape=jax.ShapeDtypeStruct((seq, d), F32),
        compiler_params=_cparams(("parallel",), 10 * tc * d * 4 + (4 << 20)),
        name="moe_combine",
    )(y, y, h, gates, final_gain.reshape(1, d))


def _swap_halves(w):
    half = w.shape[-1] // 2
    return jnp.concatenate([w[..., half:], w[..., :half]], axis=-1)


def _mla_weights(w_uq, w_ukv):
    lora = w_uq.shape[0]
    uq = w_uq.reshape(lora, B_HEADS, B_NOPE + B_ROPE)
    nope, pe = uq[..., :B_NOPE], uq[..., B_NOPE:]
    zpad = jnp.zeros((lora, B_HEADS, B_QK_PAD - B_NOPE - B_ROPE), w_uq.dtype)
    wq1 = jnp.concatenate([nope, pe, zpad], axis=-1).reshape(lora, B_HEADS * B_QK_PAD)
    wq2 = jnp.concatenate([_swap_halves(pe), zpad], axis=-1).reshape(lora, B_HEADS * V7X_LANES)
    ukv = w_ukv.reshape(lora, B_HEADS, B_NOPE + B_V)
    wkn = ukv[..., :B_NOPE].reshape(lora, B_HEADS * B_NOPE)
    wv = ukv[..., B_NOPE:].reshape(lora, B_HEADS * B_V)
    return wq1.astype(BF16), wq2.astype(BF16), wkn.astype(BF16), wv.astype(BF16)


def _rope_tables(seq):
    inv = ROPE_THETA ** (-jnp.arange(0, B_ROPE, 2, dtype=F32) / B_ROPE)
    ang = jnp.arange(seq, dtype=F32)[:, None] * inv[None, :]
    cos, sin = jnp.cos(ang), jnp.sin(ang)
    zeros = jnp.zeros((seq, 128 - B_ROPE), F32)
    return (jnp.concatenate([cos, cos, zeros], axis=1), jnp.concatenate([-sin, sin, zeros], axis=1))


def _dispatch_plan(idx, rank, counts, tm, n_rows):
    seq = idx.shape[0]
    n_tiles = seq * TOP_K // tm + N_EXPERTS
    cnt = counts[0, :N_EXPERTS].astype(jnp.int32)
    tiles_per = (cnt + tm - 1) // tm
    tile_end = jnp.cumsum(tiles_per)
    row_start = (tile_end - tiles_per) * tm
    pos = (row_start[idx[:, :TOP_K]] + rank[:, :TOP_K]).reshape(-1)
    entry = (pos // tm) * n_rows + pos % tm
    tile_id = jnp.arange(n_tiles, dtype=jnp.int32)
    spare = (TOP_K * seq + (tile_id[:, None] % 2) * n_rows
             + jnp.arange(n_rows, dtype=jnp.int32)[None, :]).reshape(-1)
    pair = jnp.full((n_tiles * n_rows,), -1, jnp.int32).at[entry].set(
        jnp.arange(seq * TOP_K, dtype=jnp.int32))
    tok, slot = pair // TOP_K, pair % TOP_K
    row_token = jnp.where(pair >= 0, tok, 0)
    row_dest = jnp.where(pair >= 0, slot * seq + tok, spare)
    n_live = tile_end[-1]
    tile_expert = jnp.sum((tile_id[:, None] >= tile_end[None, :]).astype(jnp.int32), axis=1)
    last_expert = jnp.sum((n_live - 1 >= tile_end).astype(jnp.int32))
    tile_expert = jnp.where(tile_id < n_live, tile_expert, last_expert).astype(jnp.int32)
    shape3 = (n_tiles, 1, n_rows)
    return (row_token.reshape(shape3), row_dest.reshape(shape3), tile_expert,
            n_live.reshape(1).astype(jnp.int32), TOP_K * seq + 2 * n_rows)


def kernel(x, ev_attn_norm, ev_w_in, ev_q_norm, ev_w_uq, ev_kv_norm, ev_w_ukv, ev_lambda_q1, ev_lambda_k1, ev_lambda_q2, ev_lambda_k2, ev_subln, ev_w_out, ev_ffn_norm, ev_ffn_w_gate, ev_ffn_w_up, ev_ffn_w_down, od_attn_norm, od_w_in, od_forget_bias, od_w_out, od_ffn_norm, od_router, od_moe_w_gate, od_moe_w_up, od_moe_w_down, final_norm):
    batch, seq, d = x.shape
    assert batch == 1
    h = x.reshape(seq, d)

    lambda_init = 0.8 - 0.6 * math.exp(-0.3 * 0)
    w_in = ev_w_in[0]
    n_a = 2 * A_HEADS * 2 * A_QK + A_HEADS * A_V
    w_a = w_in[:, :n_a].astype(BF16)
    scale_a = jnp.concatenate([jnp.full((A_HEADS * 2 * A_QK,), A_QK ** -0.5 * LOG2E, F32),
                               jnp.ones((n_a - A_HEADS * 2 * A_QK,), F32)])
    w_b = jnp.concatenate([w_in[:, n_a:], _swap_halves(w_in[:, -B_ROPE:])], axis=1).astype(BF16)
    qkv_a = rms_matmul(h, ev_attn_norm[0], w_a, scale_a, BF16)
    c_b = rms_matmul(h, ev_attn_norm[0], w_b, jnp.ones((w_b.shape[1],), F32), F32, tn=w_b.shape[1])
    oa, (ffn_wg, ffn_wu, ffn_wd) = diff_attention(
        qkv_a, ev_lambda_q1[0], ev_lambda_k1[0], ev_lambda_q2[0], ev_lambda_k2[0], ev_subln[0],
        lambda_init, [(ev_ffn_w_gate[0], FFN_TF), (ev_ffn_w_up[0], FFN_TF), (ev_ffn_w_down[0], None)])
    wq1, wq2, wkn, wv = _mla_weights(ev_w_uq[0], ev_w_ukv[0])
    cos_t, sin_t = _rope_tables(seq)
    q_b, k_b, v_b = mla_prep(c_b, ev_q_norm[0], ev_kv_norm[0], wq1, wq2, wkn, wv, cos_t, sin_t)
    ob = mla_attention(q_b, k_b, v_b)
    w_out = ev_w_out[0].astype(BF16)
    h = proj_residual([oa, ob], [w_out[:A_HEADS * A_V], w_out[A_HEADS * A_V:]], h)
    h = dense_ffn(h, ev_ffn_norm[0], ffn_wg, ffn_wu, ffn_wd)

    width = C_HEADS * C_DIM
    w_qkv = od_w_in[0][:, :3 * width].astype(BF16)
    scale_c = jnp.concatenate([jnp.full((width,), C_DIM ** -0.5 * LOG2E, F32), jnp.ones((2 * width,), F32)])
    qkv_c = rms_matmul(h, od_attn_norm[0], w_qkv, scale_c, BF16)
    wf_t = od_w_in[0][:, 3 * width:].T.astype(BF16)
    cum = forget_gate_cumsum(h, od_attn_norm[0], wf_t, od_forget_bias[0])
    n_e, _, ff = od_moe_w_gate[0].shape
    oc, (moe_wg, moe_wu, moe_wd) = fox_attention(qkv_c, cum, [
        (od_moe_w_gate[0].reshape(n_e * d, ff), FFN_TF), (od_moe_w_up[0].reshape(n_e * d, ff), FFN_TF),
        (od_moe_w_down[0].reshape(n_e * ff, d), None)])
    h = proj_residual([oc], [od_w_out[0].astype(BF16)], h)

    w_router = jnp.zeros((d, V7X_LANES), F32).at[:, :N_EXPERTS].set(od_router[0])
    idx, gates, rank, counts = moe_router(h, od_ffn_norm[0], w_router)
    n_rows = _moe_tile_rows(MOE_TM, od_moe_w_gate.shape[-1] // FFN_TF)
    row_token, row_dest, tile_expert, n_live, n_y_rows = _dispatch_plan(idx, rank, counts, MOE_TM, n_rows)
    blocked = (ff // FFN_TF, n_e, d, FFN_TF)
    y = moe_ffn(h, row_token, row_dest, n_y_rows, od_ffn_norm[0], moe_wg.reshape(blocked),
                moe_wu.reshape(blocked), moe_wd.reshape(n_e, ff, d), tile_expert, n_live)
    out = moe_combine(y, h, gates, final_norm)
    return out.reshape(batch, seq, d)
```

```python
import functools
import math

import jax
import jax.numpy as jnp
from jax import lax
from jax.experimental import pallas as pl
from jax.experimental.pallas import tpu as pltpu

F32 = jnp.float32
BF16 = jnp.bfloat16

V7X_LANES = 128
V7X_VMEM_REQUEST_CAP = 60000 * 1024

A_HEADS, A_QK, A_V = 8, 64, 128
B_HEADS, B_LORA, B_NOPE, B_ROPE, B_V = 8, 512, 128, 64, 128
B_QK_PAD = 256
C_HEADS, C_DIM = 16, 128
N_EXPERTS, TOP_K = 8, 2
ROPE_THETA = 10000.0
MASK_VALUE = -1e30
LOG2E = math.log2(math.e)

ATTN_BLOCK = 1024
DIFF_ATTN_BLOCK = 1024
LONG_TRIP_PAIRS = 4
SOFTMAX_ROWS = 128
PROJ_TM, PROJ_TN = 1024, 1024
FFN_TM, FFN_TF = 512, 512
MOE_TM = 512
COMBINE_ROWS = 256
PREP_TM = 512
GATE_TM = 1024
ROUTER_TM = 512


def _cparams(semantics, vmem_bytes):
    return pltpu.CompilerParams(
        dimension_semantics=semantics,
        vmem_limit_bytes=int(min(V7X_VMEM_REQUEST_CAP, vmem_bytes)))


def _rms(x, g, eps):
    return x * lax.rsqrt(jnp.mean(x * x, axis=-1, keepdims=True) + eps) * g


def _rms_matmul_kernel(x_ref, g_ref, w_ref, cs_ref, o_ref, xn_ref, *, eps):
    @pl.when(pl.program_id(1) == 0)
    def _():
        xn_ref[...] = _rms(x_ref[...], g_ref[...], eps).astype(BF16)

    acc = jnp.dot(xn_ref[...], w_ref[...], preferred_element_type=F32)
    o_ref[...] = (acc * cs_ref[...]).astype(o_ref.dtype)


def rms_matmul(x, g, w, col_scale, out_dtype, *, eps=1e-6, tm=PROJ_TM, tn=PROJ_TN):
    m, k = x.shape
    n = w.shape[1]
    assert m % tm == 0 and n % tn == 0, (m, n, tm, tn)
    vmem = 2 * tm * k * 4 + tm * k * 2 + 2 * k * tn * 2 + 4 * tm * tn * 4 + (4 << 20)
    return pl.pallas_call(
        functools.partial(_rms_matmul_kernel, eps=eps),
        grid=(m // tm, n // tn),
        in_specs=[
            pl.BlockSpec((tm, k), lambda i, j: (i, 0)),
            pl.BlockSpec((1, k), lambda i, j: (0, 0)),
            pl.BlockSpec((k, tn), lambda i, j: (0, j)),
            pl.BlockSpec((1, tn), lambda i, j: (0, j)),
        ],
        out_specs=pl.BlockSpec((tm, tn), lambda i, j: (i, j)),
        out_shape=jax.ShapeDtypeStruct((m, n), out_dtype),
        scratch_shapes=[pltpu.VMEM((tm, k), BF16)],
        compiler_params=_cparams(("parallel", "arbitrary"), vmem),
        name="rms_matmul",
    )(x, g.reshape(1, k), w, col_scale.reshape(1, n))


def _proj_residual_kernel(*refs, n_in):
    a_refs, w_refs = refs[:n_in], refs[n_in:2 * n_in]
    h_ref, o_ref = refs[2 * n_in], refs[2 * n_in + 1]
    acc = h_ref[...]
    for a_ref, w_ref in zip(a_refs, w_refs):
        acc = acc + jnp.dot(a_ref[...], w_ref[...], preferred_element_type=F32)
    o_ref[...] = acc


def proj_residual(a_list, w_list, h, *, tm=PROJ_TM, tn=PROJ_TN):
    m, n = h.shape
    n_in = len(a_list)
    assert m % tm == 0 and n % tn == 0
    ks = [a.shape[1] for a in a_list]
    vmem = sum(2 * tm * k * 2 + 2 * k * tn * 2 for k in ks) + 6 * tm * tn * 4 + (4 << 20)
    in_specs = [pl.BlockSpec((tm, k), lambda i, j: (i, 0)) for k in ks]
    in_specs += [pl.BlockSpec((k, tn), lambda i, j: (0, j)) for k in ks]
    in_specs += [pl.BlockSpec((tm, tn), lambda i, j: (i, j))]
    return pl.pallas_call(
        functools.partial(_proj_residual_kernel, n_in=n_in),
        grid=(m // tm, n // tn),
        in_specs=in_specs,
        out_specs=pl.BlockSpec((tm, tn), lambda i, j: (i, j)),
        out_shape=jax.ShapeDtypeStruct((m, n), F32),
        compiler_params=_cparams(("parallel", "arbitrary"), vmem),
        name="proj_residual",
    )(*a_list, *w_list, h)


def _attn_stream(q, k_ref, v_ref, bias_fn, scratch, *, bq):
    s_bufs, p_bufs, a_bufs = scratch[0:2], scratch[2:4], scratch[4:6]
    m_scr, l_scr, acc_scr = scratch[6:9]
    rows, bk = s_bufs[0].shape
    assert bq == 2 * bk
    n_lane_chunks = bk // V7X_LANES
    m_scr[...] = jnp.full(m_scr.shape, MASK_VALUE, F32)
    l_scr[...] = jnp.zeros(l_scr.shape, F32)
    acc_scr[...] = jnp.zeros(acc_scr.shape, F32)
    a_bufs[1][...] = jnp.ones(a_bufs[1].shape, F32)
    p_bufs[1][...] = jnp.zeros(p_bufs[1].shape, BF16)

    def scores(j, slot):
        start = pl.multiple_of(j * bk, bk)
        s = lax.dot_general(q, k_ref[pl.ds(start, bk), :], (((1,), (1,)), ((), ())),
                            preferred_element_type=F32)
        s_bufs[slot][...] = s if bias_fn is None else s + bias_fn(start)

    def softmax(slot, diag_offset=None):
        s_buf, p_buf, a_buf = s_bufs[slot], p_bufs[slot], a_bufs[slot]
        for r0 in range(0, rows, SOFTMAX_ROWS):
            sl = slice(r0, r0 + SOFTMAX_ROWS)
            first_row = r0 % bq
            if diag_offset is not None and first_row + SOFTMAX_ROWS - 1 < diag_offset:
                continue
            s = s_buf[sl, :]
            if diag_offset is not None and first_row < diag_offset + bk - 1:
                row = lax.broadcasted_iota(jnp.int32, s.shape, 0) + first_row
                col = lax.broadcasted_iota(jnp.int32, s.shape, 1) + diag_offset
                s = jnp.where(col <= row, s, MASK_VALUE)
            chunks = [s[:, i * V7X_LANES:(i + 1) * V7X_LANES] for i in range(n_lane_chunks)]
            m_lane = chunks[0]
            for x in chunks[1:]:
                m_lane = jnp.maximum(m_lane, x)
            m_prev = m_scr[sl, :]
            m_next = jnp.maximum(m_prev, jnp.max(m_lane, axis=1, keepdims=True))
            alpha = jnp.exp2(m_prev - m_next)
            l_new = alpha * l_scr[sl, :]
            for i, x in enumerate(chunks):
                p = jnp.exp2(x - m_next)
                l_new = l_new + p
                p_buf[sl, i * V7X_LANES:(i + 1) * V7X_LANES] = p.astype(BF16)
            l_scr[sl, :] = l_new
            m_scr[sl, :] = m_next
            a_buf[sl, :] = alpha

    def pv(j, slot, first_live_row=0):
        start = pl.multiple_of(j * bk, bk)
        v = v_ref[pl.ds(start, bk), :]
        live = ([slice(0, rows)] if first_live_row == 0 else
                [slice(r0 + first_live_row, r0 + bq) for r0 in range(0, rows, bq)])
        for sl in live:
            acc_scr[sl, :] = acc_scr[sl, :] * a_bufs[slot][sl, :] + jnp.dot(
                p_bufs[slot][sl, :], v, preferred_element_type=F32)

    def result():
        return acc_scr[...] / jnp.sum(l_scr[...], axis=1, keepdims=True)

    return scores, softmax, pv, result


def _flash_attend(heads, qi, scratch, *, bq, pairs_per_trip=2):
    bk = bq // 2
    per_head = len(scratch) // len(heads)
    streams = [_attn_stream(*head, scratch[t * per_head:(t + 1) * per_head], bq=bq)
               for t, head in enumerate(heads)]

    def each(stage, *args, **kwargs):
        for stream in streams:
            stream[stage](*args, **kwargs)

    SCORES, SOFTMAX, PV = 0, 1, 2

    def two_steps(j, diag):
        each(PV, jnp.maximum(j - 1, 0), 1)
        each(SOFTMAX, 0, 0 if diag else None)
        each(SCORES, j + 1, 1)
        each(PV, j, 0)
        each(SOFTMAX, 1, bk if diag else None)
        if not diag:
            each(SCORES, j + 2, 0)

    each(SCORES, 0, 0)

    done = 0
    pairs = pairs_per_trip
    while pairs >= 1:
        def body(i, carry, pairs=pairs, done=done):
            for u in range(pairs):
                two_steps(2 * (done + pairs * i + u), False)
            return carry

        n_trips = (qi - done) // pairs
        lax.fori_loop(0, n_trips, body, 0)
        done = done + n_trips * pairs
        pairs //= 2
    two_steps(2 * qi, True)
    each(PV, 2 * qi + 1, 1, first_live_row=bk)
    return [stream[3]() for stream in streams]


def _attn_scratch(rows, bq):
    bk = bq // 2
    stat = lambda: pltpu.VMEM((rows, V7X_LANES), F32)
    s_buf = lambda: pltpu.VMEM((rows, bk), F32)
    p_buf = lambda: pltpu.VMEM((rows, bk), BF16)
    return [s_buf(), s_buf(), p_buf(), p_buf()] + [stat() for _ in range(5)]


def _attn_vmem(seq, rows, bq, kv_width):
    resident = 2 * seq * kv_width * 2
    buffers = 2 * rows * (bq // 2) * (4 + 2) + 5 * rows * V7X_LANES * 4
    temps = 2 * rows * (bq // 2) * 4
    return resident + buffers + temps + (6 << 20)


def _side_cast_specs(weights, n_steps, step_index):
    in_specs, out_specs, out_shapes, vmem = [], [], [], 0
    for w, col_block in weights:
        rows, cols = w.shape
        block = next(b for b in range(16, rows + 1, 16) if rows % b == 0 and b * n_steps >= rows)
        last = rows // block - 1

        def row_block(*grid_idx, last=last):
            return jnp.minimum(step_index(*grid_idx), last)

        in_specs.append(pl.BlockSpec((block, cols), lambda *g, rb=row_block: (rb(*g), 0)))
        if col_block is None:
            out_specs.append(pl.BlockSpec((block, cols), lambda *g, rb=row_block: (rb(*g), 0)))
            out_shapes.append(jax.ShapeDtypeStruct((rows, cols), BF16))
        else:
            out_specs.append(pl.BlockSpec((cols // col_block, block, col_block),
                                          lambda *g, rb=row_block: (0, rb(*g), 0)))
            out_shapes.append(jax.ShapeDtypeStruct((cols // col_block, rows, col_block), BF16))
        vmem += 2 * block * cols * (4 + 2)
    return in_specs, out_specs, out_shapes, vmem


def _side_cast(in_refs, out_refs):
    for src, dst in zip(in_refs, out_refs):
        if len(dst.shape) == 2:
            dst[...] = src[...].astype(dst.dtype)
        else:
            width = dst.shape[2]
            for c in range(dst.shape[0]):
                dst[c] = src[:, c * width:(c + 1) * width].astype(dst.dtype)


def _diff_attn_kernel(slope_ref, q_ref, k_ref, v_ref, lq1_ref, lk1_ref, lq2_ref, lk2_ref,
                      subln_ref, *rest, bq, lambda_init, n_side):
    o_ref, scratch = rest[n_side], rest[2 * n_side + 1:]
    _side_cast(rest[:n_side], rest[n_side + 1:2 * n_side + 1])
    h, qi = pl.program_id(0), pl.program_id(1)
    q = q_ref[...]
    lane = lax.broadcasted_iota(jnp.int32, q.shape, 1)
    zero = jnp.zeros_like(q)
    qs = jnp.concatenate([jnp.where(lane < A_QK, q, zero), jnp.where(lane >= A_QK, q, zero)], axis=0)
    slope = slope_ref[h] * LOG2E
    t0 = qi * bq

    def bias_fn(start):
        kpos = lax.broadcasted_iota(jnp.int32, (1, bq // 2), 1) + (start - t0)
        return slope * kpos.astype(F32)

    o, = _flash_attend([(qs, k_ref, v_ref, bias_fn)], qi, scratch, bq=bq, pairs_per_trip=LONG_TRIP_PAIRS)
    lam = (jnp.exp(jnp.sum(lq1_ref[...] * lk1_ref[...], axis=1, keepdims=True))
           - jnp.exp(jnp.sum(lq2_ref[...] * lk2_ref[...], axis=1, keepdims=True)) + lambda_init)
    d = o[:bq] - lam * o[bq:]
    o_ref[...] = (_rms(d, subln_ref[...], 1e-5) * (1.0 - lambda_init)).astype(o_ref.dtype)


def diff_attention(qkv, lq1, lk1, lq2, lk2, subln, lambda_init, side_weights, *, bq=DIFF_ATTN_BLOCK):
    seq = qkv.shape[0]
    nq = seq // bq
    slopes = jnp.exp2(-8.0 * jnp.arange(1, A_HEADS + 1, dtype=F32) / A_HEADS)
    vec = lambda: pl.BlockSpec((1, A_QK), lambda h, i, s: (0, 0))
    side_in, side_out, side_shapes, side_vmem = _side_cast_specs(
        side_weights, A_HEADS * nq, lambda h, i, s: h * nq + i)
    grid_spec = pltpu.PrefetchScalarGridSpec(
        num_scalar_prefetch=1,
        grid=(A_HEADS, nq),
        in_specs=[
            pl.BlockSpec((bq, 2 * A_QK), lambda h, i, s: (i, h)),
            pl.BlockSpec((seq, 2 * A_QK), lambda h, i, s: (0, A_HEADS + h)),
            pl.BlockSpec((seq, A_V), lambda h, i, s: (0, 2 * A_HEADS + h)),
            vec(), vec(), vec(), vec(),
            pl.BlockSpec((1, A_V), lambda h, i, s: (0, 0)),
        ] + side_in,
        out_specs=[pl.BlockSpec((bq, A_V), lambda h, i, s: (i, h))] + side_out,
        scratch_shapes=_attn_scratch(2 * bq, bq),
    )
    out = pl.pallas_call(
        functools.partial(_diff_attn_kernel, bq=bq, lambda_init=lambda_init, n_side=len(side_weights)),
        grid_spec=grid_spec,
        out_shape=[jax.ShapeDtypeStruct((seq, A_HEADS * A_V), BF16)] + side_shapes,
        compiler_params=_cparams(("arbitrary", "arbitrary"), _attn_vmem(seq, 2 * bq, bq, 2 * A_QK + A_V) + side_vmem),
        name="diff_attention",
    )(slopes, qkv, qkv, qkv, lq1.reshape(1, -1), lk1.reshape(1, -1), lq2.reshape(1, -1),
      lk2.reshape(1, -1), subln.reshape(1, -1), *[w for w, _ in side_weights])
    return out[0], out[1:]


def _mla_prep_kernel(c_ref, qn_ref, kvn_ref, wq1_ref, wq2_ref, wkn_ref, wv_ref, cos_ref, sin_ref,
                     q_ref, k_ref, v_ref, *, scale):
    c = c_ref[...]
    cqn = _rms(c[:, :B_LORA], qn_ref[...], 1e-6).astype(BF16)
    ckvn = _rms(c[:, B_LORA:2 * B_LORA], kvn_ref[...], 1e-6).astype(BF16)
    kp = c[:, 2 * B_LORA:]
    cos, sin = cos_ref[...], sin_ref[...]
    qa = jnp.dot(cqn, wq1_ref[...], preferred_element_type=F32)
    qb = jnp.dot(cqn, wq2_ref[...], preferred_element_type=F32)
    kn = jnp.dot(ckvn, wkn_ref[...], preferred_element_type=F32)
    v_ref[...] = jnp.dot(ckvn, wv_ref[...], preferred_element_type=F32).astype(v_ref.dtype)
    kr = (kp * cos + pltpu.roll(kp, B_ROPE, axis=1) * sin).astype(k_ref.dtype)
    for h in range(B_HEADS):
        lo = B_QK_PAD * h
        hi, end = lo + B_NOPE, lo + B_QK_PAD
        group = slice(V7X_LANES * h, V7X_LANES * (h + 1))
        q_ref[:, lo:hi] = (qa[:, lo:hi] * scale).astype(q_ref.dtype)
        pe = qa[:, hi:end] * cos + qb[:, group] * sin
        q_ref[:, hi:end] = (pe * scale).astype(q_ref.dtype)
        k_ref[:, lo:hi] = kn[:, group].astype(k_ref.dtype)
        k_ref[:, hi:end] = kr


def mla_prep(c, q_norm, kv_norm, wq1, wq2, wkn, wv, cos_t, sin_t, *, tm=PREP_TM):
    seq = c.shape[0]
    scale = (B_NOPE + B_ROPE) ** -0.5 * LOG2E
    full = lambda a: pl.BlockSpec(a.shape, lambda i: (0, 0))
    row = lambda w: pl.BlockSpec((tm, w), lambda i: (i, 0))
    qn, kvn = q_norm.reshape(1, -1), kv_norm.reshape(1, -1)
    qk_w, v_w = B_HEADS * B_QK_PAD, B_HEADS * B_V
    out_w = 2 * qk_w + v_w
    weights = 2 * 2 * (wq1.size + wq2.size + wkn.size + wv.size)
    vmem = (weights + 2 * tm * (c.shape[1] * 4 + 2 * V7X_LANES * 4 + out_w * 2)
            + 2 * tm * out_w * 4 + (6 << 20))
    return pl.pallas_call(
        functools.partial(_mla_prep_kernel, scale=scale),
        grid=(seq // tm,),
        in_specs=[row(c.shape[1]), full(qn), full(kvn), full(wq1), full(wq2), full(wkn), full(wv),
                  row(V7X_LANES), row(V7X_LANES)],
        out_specs=[row(qk_w), row(qk_w), row(v_w)],
        out_shape=[jax.ShapeDtypeStruct((seq, qk_w), BF16), jax.ShapeDtypeStruct((seq, qk_w), BF16),
                   jax.ShapeDtypeStruct((seq, v_w), BF16)],
        compiler_params=_cparams(("parallel",), vmem),
        name="mla_prep",
    )(c, qn, kvn, wq1, wq2, wkn, wv, cos_t, sin_t)


def _mla_attn_kernel(q_ref, k_ref, v_ref, o_ref, *scratch, bq):
    o, = _flash_attend([(q_ref[...], k_ref, v_ref, None)], pl.program_id(1), scratch, bq=bq,
                       pairs_per_trip=LONG_TRIP_PAIRS)
    o_ref[...] = o.astype(o_ref.dtype)


def mla_attention(q, k, v, *, bq=ATTN_BLOCK):
    seq = q.shape[0]
    return pl.pallas_call(
        functools.partial(_mla_attn_kernel, bq=bq),
        grid=(B_HEADS, seq // bq),
        in_specs=[
            pl.BlockSpec((bq, B_QK_PAD), lambda h, i: (i, h)),
            pl.BlockSpec((seq, B_QK_PAD), lambda h, i: (0, h)),
            pl.BlockSpec((seq, B_V), lambda h, i: (0, h)),
        ],
        out_specs=pl.BlockSpec((bq, B_V), lambda h, i: (i, h)),
        out_shape=jax.ShapeDtypeStruct((seq, B_HEADS * B_V), BF16),
        scratch_shapes=_attn_scratch(bq, bq),
        compiler_params=_cparams(("parallel", "arbitrary"), _attn_vmem(seq, bq, bq, B_QK_PAD + B_V)),
        name="mla_attention",
    )(q, k, v)


def _forget_gate_kernel(h_ref, g_ref, wf_ref, b_ref, cum_ref, carry_scr):
    @pl.when(pl.program_id(0) == 0)
    def _():
        carry_scr[...] = jnp.zeros(carry_scr.shape, F32)

    xn = _rms(h_ref[...], g_ref[...], 1e-6).astype(BF16)
    f = lax.dot_general(wf_ref[...], xn, (((1,), (1,)), ((), ())), preferred_element_type=F32)
    z = f + b_ref[...]
    log_f = jnp.minimum(z, 0.0) - jnp.log(1.0 + jnp.exp(-jnp.abs(z)))
    r = lax.broadcasted_iota(jnp.int32, (V7X_LANES, V7X_LANES), 0)
    c = lax.broadcasted_iota(jnp.int32, (V7X_LANES, V7X_LANES), 1)
    tri = (r <= c).astype(F32)
    carry = carry_scr[...]
    for j in range(log_f.shape[1] // V7X_LANES):
        sl = slice(j * V7X_LANES, (j + 1) * V7X_LANES)
        loc = jnp.dot(log_f[:, sl], tri, preferred_element_type=F32,
                      precision=lax.Precision.HIGHEST) + carry
        cum_ref[:, sl] = loc
        carry = jnp.broadcast_to(loc[:, V7X_LANES - 1:], carry.shape)
    carry_scr[...] = carry


def forget_gate_cumsum(h, g, wf_t, bias, *, tm=GATE_TM):
    seq, d = h.shape
    nh = wf_t.shape[0]
    return pl.pallas_call(
        _forget_gate_kernel,
        grid=(seq // tm,),
        in_specs=[
            pl.BlockSpec((tm, d), lambda i: (i, 0)),
            pl.BlockSpec((1, d), lambda i: (0, 0)),
            pl.BlockSpec((nh, d), lambda i: (0, 0)),
            pl.BlockSpec((nh, 1), lambda i: (0, 0)),
        ],
        out_specs=pl.BlockSpec((nh, tm), lambda i: (0, i)),
        out_shape=jax.ShapeDtypeStruct((nh, seq), F32),
        scratch_shapes=[pltpu.VMEM((nh, V7X_LANES), F32)],
        compiler_params=_cparams(("arbitrary",), 2 * tm * d * 4 + 3 * tm * d * 4 + (4 << 20)),
        name="forget_gate_cumsum",
    )(h, g.reshape(1, d), wf_t, bias.reshape(nh, 1))


def _fox_attn_kernel(q_ref, k_ref, v_ref, cum_ref, *rest, bq, n_side):
    o_ref, scratch = rest[n_side], rest[2 * n_side + 1:]
    _side_cast(rest[:n_side], rest[n_side + 1:2 * n_side + 1])
    qi = pl.program_id(1)
    t0 = pl.multiple_of(qi * bq, bq)
    c0 = cum_ref[:, pl.ds(t0, V7X_LANES)][:, :1]

    def bias_fn(start):
        return (c0 - cum_ref[:, pl.ds(start, bq // 2)]) * LOG2E

    o, = _flash_attend([(q_ref[...], k_ref, v_ref, bias_fn)], qi, scratch, bq=bq,
                       pairs_per_trip=LONG_TRIP_PAIRS)
    o_ref[...] = o.astype(o_ref.dtype)


def fox_attention(qkv, cum, side_weights, *, bq=ATTN_BLOCK):
    seq = qkv.shape[0]
    nq = seq // bq
    cum3 = cum.reshape(C_HEADS, 1, seq)
    side_in, side_out, side_shapes, side_vmem = _side_cast_specs(
        side_weights, C_HEADS * nq, lambda h, i: h * nq + i)
    out = pl.pallas_call(
        functools.partial(_fox_attn_kernel, bq=bq, n_side=len(side_weights)),
        grid=(C_HEADS, nq),
        in_specs=[
            pl.BlockSpec((bq, C_DIM), lambda h, i: (i, h)),
            pl.BlockSpec((seq, C_DIM), lambda h, i: (0, C_HEADS + h)),
            pl.BlockSpec((seq, C_DIM), lambda h, i: (0, 2 * C_HEADS + h)),
            pl.BlockSpec((None, 1, seq), lambda h, i: (h, 0, 0)),
        ] + side_in,
        out_specs=[pl.BlockSpec((bq, C_DIM), lambda h, i: (i, h))] + side_out,
        out_shape=[jax.ShapeDtypeStruct((seq, C_HEADS * C_DIM), BF16)] + side_shapes,
        scratch_shapes=_attn_scratch(bq, bq),
        compiler_params=_cparams(("arbitrary", "arbitrary"),
                                 _attn_vmem(seq, bq, bq, 2 * C_DIM) + side_vmem),
        name="fox_attention",
    )(qkv, qkv, qkv, cum3, *[w for w, _ in side_weights])
    return out[0], out[1:]


def _swiglu_accumulate(xn, wg, wu, wd, acc_ref):
    gate = jnp.dot(xn, wg, preferred_element_type=F32)
    up = jnp.dot(xn, wu, preferred_element_type=F32)
    act = (gate * jax.nn.sigmoid(gate) * up).astype(BF16)
    acc_ref[...] += jnp.dot(act, wd, preferred_element_type=F32)


def _ffn_kernel(h_ref, g_ref, wg_ref, wu_ref, wd_ref, o_ref, xn_ref, acc_ref):
    f = pl.program_id(1)

    @pl.when(f == 0)
    def _():
        xn_ref[...] = _rms(h_ref[...], g_ref[...], 1e-6).astype(BF16)
        acc_ref[...] = jnp.zeros(acc_ref.shape, F32)

    _swiglu_accumulate(xn_ref[...], wg_ref[...], wu_ref[...], wd_ref[...], acc_ref)

    @pl.when(f == pl.num_programs(1) - 1)
    def _():
        o_ref[...] = h_ref[...] + acc_ref[...]


def _ffn_vmem(tm, d, tf):
    return 4 * tm * d * 4 + tm * d * 2 + tm * d * 4 + 3 * 2 * d * tf * 2 + 4 * tm * tf * 4 + (6 << 20)


def dense_ffn(h, g, wg, wu, wd, *, tm=FFN_TM):
    m, d = h.shape
    n_f, _, tf = wg.shape
    assert m % tm == 0 and wd.shape[0] == n_f * tf
    return pl.pallas_call(
        _ffn_kernel,
        grid=(m // tm, n_f),
        in_specs=[
            pl.BlockSpec((tm, d), lambda i, f: (i, 0)),
            pl.BlockSpec((1, d), lambda i, f: (0, 0)),
            pl.BlockSpec((None, d, tf), lambda i, f: (f, 0, 0)),
            pl.BlockSpec((None, d, tf), lambda i, f: (f, 0, 0)),
            pl.BlockSpec((tf, d), lambda i, f: (f, 0)),
        ],
        out_specs=pl.BlockSpec((tm, d), lambda i, f: (i, 0)),
        out_shape=jax.ShapeDtypeStruct((m, d), F32),
        scratch_shapes=[pltpu.VMEM((tm, d), BF16), pltpu.VMEM((tm, d), F32)],
        compiler_params=_cparams(("parallel", "arbitrary"), _ffn_vmem(tm, d, tf)),
        name="dense_ffn",
    )(h, g.reshape(1, d), wg, wu, wd)


def _router_kernel(h_ref, g_ref, wr_ref, idx_ref, gate_ref, rank_ref, cnt_ref, cnt_scr):
    @pl.when(pl.program_id(0) == 0)
    def _():
        cnt_scr[...] = jnp.zeros(cnt_scr.shape, F32)

    tm = h_ref.shape[0]
    xn = _rms(h_ref[...], g_ref[...], 1e-6)
    logits = jnp.dot(xn, wr_ref[...], preferred_element_type=F32, precision=lax.Precision.HIGHEST)
    lane = lax.broadcasted_iota(jnp.int32, logits.shape, 1)
    lane_f = lane.astype(F32)
    valid = lane < N_EXPERTS
    logits = jnp.where(valid, logits, MASK_VALUE)
    e = jnp.exp(logits - jnp.max(logits, axis=1, keepdims=True))
    probs = jnp.where(valid, e / jnp.sum(e, axis=1, keepdims=True), -1.0)
    p1 = jnp.max(probs, axis=1, keepdims=True)
    i1 = jnp.min(jnp.where(probs == p1, lane_f, float(V7X_LANES)), axis=1, keepdims=True)
    rest = jnp.where(lane_f == i1, -1.0, probs)
    p2 = jnp.max(rest, axis=1, keepdims=True)
    i2 = jnp.min(jnp.where(rest == p2, lane_f, float(V7X_LANES)), axis=1, keepdims=True)
    hot1 = (lane_f == i1).astype(F32)
    hot2 = (lane_f == i2).astype(F32)
    r = lax.broadcasted_iota(jnp.int32, (tm, tm), 0)
    c = lax.broadcasted_iota(jnp.int32, (tm, tm), 1)
    strict_lower = (c < r).astype(BF16)
    before = jnp.dot(strict_lower, (hot1 + hot2).astype(BF16), preferred_element_type=F32)
    before = before + cnt_scr[...]
    rank1 = jnp.sum(before * hot1, axis=1, keepdims=True)
    rank2 = jnp.sum(before * hot2, axis=1, keepdims=True)
    cnt_scr[...] += jnp.sum(hot1 + hot2, axis=0, keepdims=True)
    cnt_ref[...] = cnt_scr[...]
    denom = p1 + p2
    idx_ref[...] = jnp.where(lane == 0, i1, jnp.where(lane == 1, i2, 0.0)).astype(jnp.int32)
    gate_ref[...] = jnp.where(lane == 0, p1 / denom, jnp.where(lane == 1, p2 / denom, 0.0))
    rank_ref[...] = jnp.where(lane == 0, rank1, jnp.where(lane == 1, rank2, 0.0)).astype(jnp.int32)


def moe_router(h, g, w_router_padded, *, tm=ROUTER_TM):
    seq, d = h.shape
    out = lambda dt: jax.ShapeDtypeStruct((seq, V7X_LANES), dt)
    row = pl.BlockSpec((tm, V7X_LANES), lambda i: (i, 0))
    return pl.pallas_call(
        _router_kernel,
        grid=(seq // tm,),
        in_specs=[
            pl.BlockSpec((tm, d), lambda i: (i, 0)),
            pl.BlockSpec((1, d), lambda i: (0, 0)),
            pl.BlockSpec((d, V7X_LANES), lambda i: (0, 0)),
        ],
        out_specs=[row, row, row, pl.BlockSpec((1, V7X_LANES), lambda i: (0, 0))],
        out_shape=[out(jnp.int32), out(F32), out(jnp.int32),
                   jax.ShapeDtypeStruct((1, V7X_LANES), F32)],
        scratch_shapes=[pltpu.VMEM((1, V7X_LANES), F32)],
        compiler_params=_cparams(("arbitrary",), 5 * tm * d * 4 + 4 * tm * tm * 4 + (6 << 20)),
        name="moe_router",
    )(h, g.reshape(1, d), w_router_padded)


def _moe_ffn_kernel(te_ref, nv_ref, rows_ref, next_rows_ref, dest_ref, prev_dest_ref, h_hbm, g_ref,
                    wg_ref, wu_ref, wd_ref, y_hbm, x_buf, y_buf, xn_ref, acc_ref, in_sem, out_sem):
    i, f = pl.program_id(0), pl.program_id(1)
    n_tiles, n_f = pl.num_programs(0), pl.num_programs(1)
    live = i < nv_ref[0]
    tm = xn_ref.shape[0]
    n_rows = x_buf.shape[1]
    rows_per_step = n_rows // n_f
    slot = i % 2

    def gather_copy(idx_ref, r, dst_slot):
        return pltpu.make_async_copy(h_hbm.at[pl.ds(idx_ref[0, 0, r], 1)],
                                     x_buf.at[dst_slot, pl.ds(r, 1)], in_sem.at[dst_slot])

    def scatter_copy(idx_ref, r, src_slot):
        src_row = jnp.minimum(r, tm - 1)
        return pltpu.make_async_copy(y_buf.at[src_slot, pl.ds(src_row, 1)],
                                     y_hbm.at[pl.ds(idx_ref[0, 0, r], 1)], out_sem.at[src_slot])

    def wait_gather(dst_slot):
        pltpu.make_async_copy(h_hbm.at[pl.ds(0, n_rows)], x_buf.at[dst_slot], in_sem.at[dst_slot]).wait()

    def wait_scatter(src_slot):
        pltpu.make_async_copy(y_buf.at[src_slot], y_hbm.at[pl.ds(0, n_rows)], out_sem.at[src_slot]).wait()

    def step_rows():
        return [f * rows_per_step + k for k in range(rows_per_step)]

    def gather_next_tile_rows():
        for r in step_rows():
            gather_copy(next_rows_ref, r, 1 - slot).start()

    def scatter_prev_tile_rows():
        for r in step_rows():
            scatter_copy(prev_dest_ref, r, 1 - slot).start()

    @pl.when((f == 0) & (i == 0))
    def _():
        def body(r, carry):
            gather_copy(rows_ref, r, slot).start()
            return carry
        lax.fori_loop(0, n_rows, body, 0)

    @pl.when(f == 0)
    def _():
        wait_gather(slot)
        xn_ref[...] = _rms(x_buf[slot, :tm], g_ref[...], 1e-6).astype(BF16)
        acc_ref[...] = jnp.zeros(acc_ref.shape, F32)

    @pl.when(live & (i > 0))
    def _():
        _swiglu_accumulate(xn_ref[...], wg_ref[...], wu_ref[...], wd_ref[...], acc_ref)
        gather_next_tile_rows()
        scatter_prev_tile_rows()

    @pl.when(live & (i == 0))
    def _():
        _swiglu_accumulate(xn_ref[...], wg_ref[...], wu_ref[...], wd_ref[...], acc_ref)
        gather_next_tile_rows()

    @pl.when(jnp.logical_not(live))
    def _():
        gather_next_tile_rows()
        scatter_prev_tile_rows()

    @pl.when(f == n_f - 1)
    def _():
        @pl.when(i >= 2)
        def _():
            wait_scatter(slot)
        y_buf[slot, :tm] = acc_ref[...]

    @pl.when((f == n_f - 1) & (i == n_tiles - 1))
    def _():
        def body(r, carry):
            scatter_copy(dest_ref, r, slot).start()
            return carry
        lax.fori_loop(0, n_rows, body, 0)
        wait_gather(1 - slot)
        wait_scatter(1 - slot)
        wait_scatter(slot)


def _moe_tile_rows(tm, nf):
    return -(-tm // (8 * nf)) * 8 * nf


def moe_ffn(h, row_token, row_dest, n_y_rows, g, wg, wu, wd, tile_expert, n_live, *, tm=MOE_TM):
    d = h.shape[1]
    nf, _, _, tf = wg.shape
    n_tiles, _, n_rows = row_token.shape
    assert wd.shape[1] == nf * tf and n_rows == _moe_tile_rows(tm, nf) and n_tiles >= 2

    def f_idx(i, f, nv):
        return jnp.where(i < nv[0], f, nf - 1)

    grid_spec = pltpu.PrefetchScalarGridSpec(
        num_scalar_prefetch=2,
        grid=(n_tiles, nf),
        in_specs=[
            pl.BlockSpec((1, 1, n_rows), lambda i, f, te, nv: (i, 0, 0), memory_space=pltpu.SMEM),
            pl.BlockSpec((1, 1, n_rows), lambda i, f, te, nv: (jnp.minimum(i + 1, n_tiles - 1), 0, 0),
                         memory_space=pltpu.SMEM),
            pl.BlockSpec((1, 1, n_rows), lambda i, f, te, nv: (i, 0, 0), memory_space=pltpu.SMEM),
            pl.BlockSpec((1, 1, n_rows), lambda i, f, te, nv: (jnp.maximum(i - 1, 0), 0, 0),
                         memory_space=pltpu.SMEM),
            pl.BlockSpec(memory_space=pl.ANY),
            pl.BlockSpec((1, d), lambda i, f, te, nv: (0, 0)),
            pl.BlockSpec((None, None, d, tf), lambda i, f, te, nv: (f_idx(i, f, nv), te[i], 0, 0)),
            pl.BlockSpec((None, None, d, tf), lambda i, f, te, nv: (f_idx(i, f, nv), te[i], 0, 0)),
            pl.BlockSpec((None, tf, d), lambda i, f, te, nv: (te[i], f_idx(i, f, nv), 0)),
        ],
        out_specs=pl.BlockSpec(memory_space=pl.ANY),
        scratch_shapes=[pltpu.VMEM((2, n_rows, d), F32), pltpu.VMEM((2, n_rows, d), F32),
                        pltpu.VMEM((tm, d), BF16), pltpu.VMEM((tm, d), F32),
                        pltpu.SemaphoreType.DMA((2,)), pltpu.SemaphoreType.DMA((2,))],
    )
    vmem = 4 * n_rows * d * 4 + tm * d * (2 + 4) + 3 * 2 * d * tf * 2 + 4 * tm * tf * 4 + (6 << 20)
    return pl.pallas_call(
        _moe_ffn_kernel,
        grid_spec=grid_spec,
        out_shape=jax.ShapeDtypeStruct((n_y_rows, d), F32),
        compiler_params=_cparams(("arbitrary", "arbitrary"), vmem),
        name="moe_ffn",
    )(tile_expert, n_live, row_token, row_token, row_dest, row_dest, h, g.reshape(1, d), wg, wu, wd)


def _combine_kernel(y0_ref, y1_ref, h_ref, gate_ref, fg_ref, o_ref):
    gate = gate_ref[...]
    out = h_ref[...] + gate[:, 0:1] * y0_ref[...] + gate[:, 1:2] * y1_ref[...]
    o_ref[...] = _rms(out, fg_ref[...], 1e-6)


def moe_combine(y, h, gates, final_gain, *, tc=COMBINE_ROWS):
    seq, d = h.shape
    n_tiles = seq // tc
    row = lambda w, off: pl.BlockSpec((tc, w), lambda i: (i + off, 0))
    return pl.pallas_call(
        _combine_kernel,
        grid=(n_tiles,),
        in_specs=[row(d, 0), row(d, n_tiles), row(d, 0), row(V7X_LANES, 0),
                  pl.BlockSpec((1, d), lambda i: (0, 0))],
        out_specs=row(d, 0),
        out_shape=jax.ShapeDtypeStruct((seq, d), F32),
        compiler_params=_cparams(("parallel",), 10 * tc * d * 4 + (4 << 20)),
        name="moe_combine",
    )(y, y, h, gates, final_gain.reshape(1, d))


def _swap_halves(w):
    half = w.shape[-1] // 2
    return jnp.concatenate([w[..., half:], w[..., :half]], axis=-1)


def _mla_weights(w_uq, w_ukv):
    lora = w_uq.shape[0]
    uq = w_uq.reshape(lora, B_HEADS, B_NOPE + B_ROPE)
    nope, pe = uq[..., :B_NOPE], uq[..., B_NOPE:]
    zpad = jnp.zeros((lora, B_HEADS, B_QK_PAD - B_NOPE - B_ROPE), w_uq.dtype)
    wq1 = jnp.concatenate([nope, pe, zpad], axis=-1).reshape(lora, B_HEADS * B_QK_PAD)
    wq2 = jnp.concatenate([_swap_halves(pe), zpad], axis=-1).reshape(lora, B_HEADS * V7X_LANES)
    ukv = w_ukv.reshape(lora, B_HEADS, B_NOPE + B_V)
    wkn = ukv[..., :B_NOPE].reshape(lora, B_HEADS * B_NOPE)
    wv = ukv[..., B_NOPE:].reshape(lora, B_HEADS * B_V)
    return wq1.astype(BF16), wq2.astype(BF16), wkn.astype(BF16), wv.astype(BF16)


def _rope_tables(seq):
    inv = ROPE_THETA ** (-jnp.arange(0, B_ROPE, 2, dtype=F32) / B_ROPE)
    ang = jnp.arange(seq, dtype=F32)[:, None] * inv[None, :]
    cos, sin = jnp.cos(ang), jnp.sin(ang)
    zeros = jnp.zeros((seq, 128 - B_ROPE), F32)
    return (jnp.concatenate([cos, cos, zeros], axis=1), jnp.concatenate([-sin, sin, zeros], axis=1))


def _dispatch_plan(idx, rank, counts, tm, n_rows):
    seq = idx.shape[0]
    n_tiles = seq * TOP_K // tm + N_EXPERTS
    cnt = counts[0, :N_EXPERTS].astype(jnp.int32)
    tiles_per = (cnt + tm - 1) // tm
    tile_end = jnp.cumsum(tiles_per)
    row_start = (tile_end - tiles_per) * tm
    pos = (row_start[idx[:, :TOP_K]] + rank[:, :TOP_K]).reshape(-1)
    entry = (pos // tm) * n_rows + pos % tm
    tile_id = jnp.arange(n_tiles, dtype=jnp.int32)
    spare = (TOP_K * seq + (tile_id[:, None] % 2) * n_rows
             + jnp.arange(n_rows, dtype=jnp.int32)[None, :]).reshape(-1)
    pair = jnp.full((n_tiles * n_rows,), -1, jnp.int32).at[entry].set(
        jnp.arange(seq * TOP_K, dtype=jnp.int32))
    tok, slot = pair // TOP_K, pair % TOP_K
    row_token = jnp.where(pair >= 0, tok, 0)
    row_dest = jnp.where(pair >= 0, slot * seq + tok, spare)
    n_live = tile_end[-1]
    tile_expert = jnp.sum((tile_id[:, None] >= tile_end[None, :]).astype(jnp.int32), axis=1)
    last_expert = jnp.sum((n_live - 1 >= tile_end).astype(jnp.int32))
    tile_expert = jnp.where(tile_id < n_live, tile_expert, last_expert).astype(jnp.int32)
    shape3 = (n_tiles, 1, n_rows)
    return (row_token.reshape(shape3), row_dest.reshape(shape3), tile_expert,
            n_live.reshape(1).astype(jnp.int32), TOP_K * seq + 2 * n_rows)


def kernel(x, ev_attn_norm, ev_w_in, ev_q_norm, ev_w_uq, ev_kv_norm, ev_w_ukv, ev_lambda_q1, ev_lambda_k1, ev_lambda_q2, ev_lambda_k2, ev_subln, ev_w_out, ev_ffn_norm, ev_ffn_w_gate, ev_ffn_w_up, ev_ffn_w_down, od_attn_norm, od_w_in, od_forget_bias, od_w_out, od_ffn_norm, od_router, od_moe_w_gate, od_moe_w_up, od_moe_w_down, final_norm):
    batch, seq, d = x.shape
    assert batch == 1
    h = x.reshape(seq, d)

    lambda_init = 0.8 - 0.6 * math.exp(-0.3 * 0)
    w_in = ev_w_in[0]
    n_a = 2 * A_HEADS * 2 * A_QK + A_HEADS * A_V
    w_a = w_in[:, :n_a].astype(BF16)
    scale_a = jnp.concatenate([jnp.full((A_HEADS * 2 * A_QK,), A_QK ** -0.5 * LOG2E, F32),
                               jnp.ones((n_a - A_HEADS * 2 * A_QK,), F32)])
    w_b = jnp.concatenate([w_in[:, n_a:], _swap_halves(w_in[:, -B_ROPE:])], axis=1).astype(BF16)
    qkv_a = rms_matmul(h, ev_attn_norm[0], w_a, scale_a, BF16)
    c_b = rms_matmul(h, ev_attn_norm[0], w_b, jnp.ones((w_b.shape[1],), F32), F32, tn=w_b.shape[1])
    oa, (ffn_wg, ffn_wu, ffn_wd) = diff_attention(
        qkv_a, ev_lambda_q1[0], ev_lambda_k1[0], ev_lambda_q2[0], ev_lambda_k2[0], ev_subln[0],
        lambda_init, [(ev_ffn_w_gate[0], FFN_TF), (ev_ffn_w_up[0], FFN_TF), (ev_ffn_w_down[0], None)])
    wq1, wq2, wkn, wv = _mla_weights(ev_w_uq[0], ev_w_ukv[0])
    cos_t, sin_t = _rope_tables(seq)
    q_b, k_b, v_b = mla_prep(c_b, ev_q_norm[0], ev_kv_norm[0], wq1, wq2, wkn, wv, cos_t, sin_t)
    ob = mla_attention(q_b, k_b, v_b)
    w_out = ev_w_out[0].astype(BF16)
    h = proj_residual([oa, ob], [w_out[:A_HEADS * A_V], w_out[A_HEADS * A_V:]], h)
    h = dense_ffn(h, ev_ffn_norm[0], ffn_wg, ffn_wu, ffn_wd)

    width = C_HEADS * C_DIM
    w_qkv = od_w_in[0][:, :3 * width].astype(BF16)
    scale_c = jnp.concatenate([jnp.full((width,), C_DIM ** -0.5 * LOG2E, F32), jnp.ones((2 * width,), F32)])
    qkv_c = rms_matmul(h, od_attn_norm[0], w_qkv, scale_c, BF16)
    wf_t = od_w_in[0][:, 3 * width:].T.astype(BF16)
    cum = forget_gate_cumsum(h, od_attn_norm[0], wf_t, od_forget_bias[0])
    n_e, _, ff = od_moe_w_gate[0].shape
    oc, (moe_wg, moe_wu, moe_wd) = fox_attention(qkv_c, cum, [
        (od_moe_w_gate[0].reshape(n_e * d, ff), FFN_TF), (od_moe_w_up[0].reshape(n_e * d, ff), FFN_TF),
        (od_moe_w_down[0].reshape(n_e * ff, d), None)])
    h = proj_residual([oc], [od_w_out[0].astype(BF16)], h)

    w_router = jnp.zeros((d, V7X_LANES), F32).at[:, :N_EXPERTS].set(od_router[0])
    idx, gates, rank, counts = moe_router(h, od_ffn_norm[0], w_router)
    n_rows = _moe_tile_rows(MOE_TM, od_moe_w_gate.shape[-1] // FFN_TF)
    row_token, row_dest, tile_expert, n_live, n_y_rows = _dispatch_plan(idx, rank, counts, MOE_TM, n_rows)
    blocked = (ff // FFN_TF, n_e, d, FFN_TF)
    y = moe_ffn(h, row_token, row_dest, n_y_rows, od_ffn_norm[0], moe_wg.reshape(blocked),
                moe_wu.reshape(blocked), moe_wd.reshape(n_e, ff, d), tile_expert, n_live)
    out = moe_combine(y, h, gates, final_norm)
    return out.reshape(batch, seq, d)
```

```python
import functools
import math

import jax
import jax.numpy as jnp
from jax import lax
from jax.experimental import pallas as pl
from jax.experimental.pallas import tpu as pltpu

F32 = jnp.float32
BF16 = jnp.bfloat16

V7X_LANES = 128
V7X_VMEM_REQUEST_CAP = 60000 * 1024

A_HEADS, A_QK, A_V = 8, 64, 128
B_HEADS, B_LORA, B_NOPE, B_ROPE, B_V = 8, 512, 128, 64, 128
B_QK_PAD = 256
C_HEADS, C_DIM = 16, 128
N_EXPERTS, TOP_K = 8, 2
ROPE_THETA = 10000.0
MASK_VALUE = -1e30
LOG2E = math.log2(math.e)

ATTN_BLOCK = 1024
DIFF_ATTN_BLOCK = 1024
LONG_TRIP_PAIRS = 4
SOFTMAX_ROWS = 128
PROJ_TM, PROJ_TN = 1024, 1024
FFN_TM, FFN_TF = 512, 512
MOE_TM = 512
COMBINE_ROWS = 256
PREP_TM = 512
GATE_TM = 1024
ROUTER_TM = 512


def _cparams(semantics, vmem_bytes):
    return pltpu.CompilerParams(
        dimension_semantics=semantics,
        vmem_limit_bytes=int(min(V7X_VMEM_REQUEST_CAP, vmem_bytes)))


def _rms(x, g, eps):
    return x * lax.rsqrt(jnp.mean(x * x, axis=-1, keepdims=True) + eps) * g


def _rms_matmul_kernel(x_ref, g_ref, w_ref, cs_ref, o_ref, xn_ref, *, eps):
    @pl.when(pl.program_id(1) == 0)
    def _():
        xn_ref[...] = _rms(x_ref[...], g_ref[...], eps).astype(BF16)

    acc = jnp.dot(xn_ref[...], w_ref[...], preferred_element_type=F32)
    o_ref[...] = (acc * cs_ref[...]).astype(o_ref.dtype)


def rms_matmul(x, g, w, col_scale, out_dtype, *, eps=1e-6, tm=PROJ_TM, tn=PROJ_TN):
    m, k = x.shape
    n = w.shape[1]
    assert m % tm == 0 and n % tn == 0, (m, n, tm, tn)
    vmem = 2 * tm * k * 4 + tm * k * 2 + 2 * k * tn * 2 + 4 * tm * tn * 4 + (4 << 20)
    return pl.pallas_call(
        functools.partial(_rms_matmul_kernel, eps=eps),
        grid=(m // tm, n // tn),
        in_specs=[
            pl.BlockSpec((tm, k), lambda i, j: (i, 0)),
            pl.BlockSpec((1, k), lambda i, j: (0, 0)),
            pl.BlockSpec((k, tn), lambda i, j: (0, j)),
            pl.BlockSpec((1, tn), lambda i, j: (0, j)),
        ],
        out_specs=pl.BlockSpec((tm, tn), lambda i, j: (i, j)),
        out_shape=jax.ShapeDtypeStruct((m, n), out_dtype),
        scratch_shapes=[pltpu.VMEM((tm, k), BF16)],
        compiler_params=_cparams(("parallel", "arbitrary"), vmem),
        name="rms_matmul",
    )(x, g.reshape(1, k), w, col_scale.reshape(1, n))


def _proj_residual_kernel(*refs, n_in):
    a_refs, w_refs = refs[:n_in], refs[n_in:2 * n_in]
    h_ref, o_ref = refs[2 * n_in], refs[2 * n_in + 1]
    acc = h_ref[...]
    for a_ref, w_ref in zip(a_refs, w_refs):
        acc = acc + jnp.dot(a_ref[...], w_ref[...], preferred_element_type=F32)
    o_ref[...] = acc


def proj_residual(a_list, w_list, h, *, tm=PROJ_TM, tn=PROJ_TN):
    m, n = h.shape
    n_in = len(a_list)
    assert m % tm == 0 and n % tn == 0
    ks = [a.shape[1] for a in a_list]
    vmem = sum(2 * tm * k * 2 + 2 * k * tn * 2 for k in ks) + 6 * tm * tn * 4 + (4 << 20)
    in_specs = [pl.BlockSpec((tm, k), lambda i, j: (i, 0)) for k in ks]
    in_specs += [pl.BlockSpec((k, tn), lambda i, j: (0, j)) for k in ks]
    in_specs += [pl.BlockSpec((tm, tn), lambda i, j: (i, j))]
    return pl.pallas_call(
        functools.partial(_proj_residual_kernel, n_in=n_in),
        grid=(m // tm, n // tn),
        in_specs=in_specs,
        out_specs=pl.BlockSpec((tm, tn), lambda i, j: (i, j)),
        out_shape=jax.ShapeDtypeStruct((m, n), F32),
        compiler_params=_cparams(("parallel", "arbitrary"), vmem),
        name="proj_residual",
    )(*a_list, *w_list, h)


def _attn_stream(q, k_ref, v_ref, bias_fn, scratch, *, bq):
    s_bufs, p_bufs, a_bufs = scratch[0:2], scratch[2:4], scratch[4:6]
    m_scr, l_scr, acc_scr = scratch[6:9]
    rows, bk = s_bufs[0].shape
    assert bq == 2 * bk
    n_lane_chunks = bk // V7X_LANES
    m_scr[...] = jnp.full(m_scr.shape, MASK_VALUE, F32)
    l_scr[...] = jnp.zeros(l_scr.shape, F32)
    acc_scr[...] = jnp.zeros(acc_scr.shape, F32)
    a_bufs[1][...] = jnp.ones(a_bufs[1].shape, F32)
    p_bufs[1][...] = jnp.zeros(p_bufs[1].shape, BF16)

    def scores(j, slot):
        start = pl.multiple_of(j * bk, bk)
        s = lax.dot_general(q, k_ref[pl.ds(start, bk), :], (((1,), (1,)), ((), ())),
                            preferred_element_type=F32)
        s_bufs[slot][...] = s if bias_fn is None else s + bias_fn(start)

    def softmax(slot, diag_offset=None):
        s_buf, p_buf, a_buf = s_bufs[slot], p_bufs[slot], a_bufs[slot]
        for r0 in range(0, rows, SOFTMAX_ROWS):
            sl = slice(r0, r0 + SOFTMAX_ROWS)
            first_row = r0 % bq
            if diag_offset is not None and first_row + SOFTMAX_ROWS - 1 < diag_offset:
                continue
            s = s_buf[sl, :]
            if diag_offset is not None and first_row < diag_offset + bk - 1:
                row = lax.broadcasted_iota(jnp.int32, s.shape, 0) + first_row
                col = lax.broadcasted_iota(jnp.int32, s.shape, 1) + diag_offset
                s = jnp.where(col <= row, s, MASK_VALUE)
            chunks = [s[:, i * V7X_LANES:(i + 1) * V7X_LANES] for i in range(n_lane_chunks)]
            m_lane = chunks[0]
            for x in chunks[1:]:
                m_lane = jnp.maximum(m_lane, x)
            m_prev = m_scr[sl, :]
            m_next = jnp.maximum(m_prev, jnp.max(m_lane, axis=1, keepdims=True))
            alpha = jnp.exp2(m_prev - m_next)
            l_new = alpha * l_scr[sl, :]
            for i, x in enumerate(chunks):
                p = jnp.exp2(x - m_next)
                l_new = l_new + p
                p_buf[sl, i * V7X_LANES:(i + 1) * V7X_LANES] = p.astype(BF16)
            l_scr[sl, :] = l_new
            m_scr[sl, :] = m_next
            a_buf[sl, :] = alpha

    def pv(j, slot, first_live_row=0):
        start = pl.multiple_of(j * bk, bk)
        v = v_ref[pl.ds(start, bk), :]
        live = ([slice(0, rows)] if first_live_row == 0 else
                [slice(r0 + first_live_row, r0 + bq) for r0 in range(0, rows, bq)])
        for sl in live:
            acc_scr[sl, :] = acc_scr[sl, :] * a_bufs[slot][sl, :] + jnp.dot(
                p_bufs[slot][sl, :], v, preferred_element_type=F32)

    def result():
        return acc_scr[...] / jnp.sum(l_scr[...], axis=1, keepdims=True)

    return scores, softmax, pv, result


def _flash_attend(heads, qi, scratch, *, bq, pairs_per_trip=2):
    bk = bq // 2
    per_head = len(scratch) // len(heads)
    streams = [_attn_stream(*head, scratch[t * per_head:(t + 1) * per_head], bq=bq)
               for t, head in enumerate(heads)]

    def each(stage, *args, **kwargs):
        for stream in streams:
            stream[stage](*args, **kwargs)

    SCORES, SOFTMAX, PV = 0, 1, 2

    def two_steps(j, diag):
        each(PV, jnp.maximum(j - 1, 0), 1)
        each(SOFTMAX, 0, 0 if diag else None)
        each(SCORES, j + 1, 1)
        each(PV, j, 0)
        each(SOFTMAX, 1, bk if diag else None)
        if not diag:
            each(SCORES, j + 2, 0)

    each(SCORES, 0, 0)

    done = 0
    pairs = pairs_per_trip
    while pairs >= 1:
        def body(i, carry, pairs=pairs, done=done):
            for u in range(pairs):
                two_steps(2 * (done + pairs * i + u), False)
            return carry

        n_trips = (qi - done) // pairs
        lax.fori_loop(0, n_trips, body, 0)
        done = done + n_trips * pairs
        pairs //= 2
    two_steps(2 * qi, True)
    each(PV, 2 * qi + 1, 1, first_live_row=bk)
    return [stream[3]() for stream in streams]


def _attn_scratch(rows, bq):
    bk = bq // 2
    stat = lambda: pltpu.VMEM((rows, V7X_LANES), F32)
    s_buf = lambda: pltpu.VMEM((rows, bk), F32)
    p_buf = lambda: pltpu.VMEM((rows, bk), BF16)
    return [s_buf(), s_buf(), p_buf(), p_buf()] + [stat() for _ in range(5)]


def _attn_vmem(seq, rows, bq, kv_width):
    resident = 2 * seq * kv_width * 2
    buffers = 2 * rows * (bq // 2) * (4 + 2) + 5 * rows * V7X_LANES * 4
    temps = 2 * rows * (bq // 2) * 4
    return resident + buffers + temps + (6 << 20)


def _side_cast_specs(weights, n_steps, step_index):
    in_specs, out_specs, out_shapes, vmem = [], [], [], 0
    for w, col_block in weights:
        rows, cols = w.shape
        block = next(b for b in range(16, rows + 1, 16) if rows % b == 0 and b * n_steps >= rows)
        last = rows // block - 1

        def row_block(*grid_idx, last=last):
            return jnp.minimum(step_index(*grid_idx), last)

        in_specs.append(pl.BlockSpec((block, cols), lambda *g, rb=row_block: (rb(*g), 0)))
        if col_block is None:
            out_specs.append(pl.BlockSpec((block, cols), lambda *g, rb=row_block: (rb(*g), 0)))
            out_shapes.append(jax.ShapeDtypeStruct((rows, cols), BF16))
        else:
            out_specs.append(pl.BlockSpec((cols // col_block, block, col_block),
                                          lambda *g, rb=row_block: (0, rb(*g), 0)))
            out_shapes.append(jax.ShapeDtypeStruct((cols // col_block, rows, col_block), BF16))
        vmem += 2 * block * cols * (4 + 2)
    return in_specs, out_specs, out_shapes, vmem


def _side_cast(in_refs, out_refs):
    for src, dst in zip(in_refs, out_refs):
        if len(dst.shape) == 2:
            dst[...] = src[...].astype(dst.dtype)
        else:
            width = dst.shape[2]
            for c in range(dst.shape[0]):
                dst[c] = src[:, c * width:(c + 1) * width].astype(dst.dtype)


def _diff_attn_kernel(slope_ref, q_ref, k_ref, v_ref, lq1_ref, lk1_ref, lq2_ref, lk2_ref,
                      subln_ref, *rest, bq, lambda_init, n_side):
    o_ref, scratch = rest[n_side], rest[2 * n_side + 1:]
    _side_cast(rest[:n_side], rest[n_side + 1:2 * n_side + 1])
    h, qi = pl.program_id(0), pl.program_id(1)
    q = q_ref[...]
    lane = lax.broadcasted_iota(jnp.int32, q.shape, 1)
    zero = jnp.zeros_like(q)
    qs = jnp.concatenate([jnp.where(lane < A_QK, q, zero), jnp.where(lane >= A_QK, q, zero)], axis=0)
    slope = slope_ref[h] * LOG2E
    t0 = qi * bq

    def bias_fn(start):
        kpos = lax.broadcasted_iota(jnp.int32, (1, bq // 2), 1) + (start - t0)
        return slope * kpos.astype(F32)

    o, = _flash_attend([(qs, k_ref, v_ref, bias_fn)], qi, scratch, bq=bq, pairs_per_trip=LONG_TRIP_PAIRS)
    lam = (jnp.exp(jnp.sum(lq1_ref[...] * lk1_ref[...], axis=1, keepdims=True))
           - jnp.exp(jnp.sum(lq2_ref[...] * lk2_ref[...], axis=1, keepdims=True)) + lambda_init)
    d = o[:bq] - lam * o[bq:]
    o_ref[...] = (_rms(d, subln_ref[...], 1e-5) * (1.0 - lambda_init)).astype(o_ref.dtype)


def diff_attention(qkv, lq1, lk1, lq2, lk2, subln, lambda_init, side_weights, *, bq=DIFF_ATTN_BLOCK):
    seq = qkv.shape[0]
    nq = seq // bq
    slopes = jnp.exp2(-8.0 * jnp.arange(1, A_HEADS + 1, dtype=F32) / A_HEADS)
    vec = lambda: pl.BlockSpec((1, A_QK), lambda h, i, s: (0, 0))
    side_in, side_out, side_shapes, side_vmem = _side_cast_specs(
        side_weights, A_HEADS * nq, lambda h, i, s: h * nq + i)
    grid_spec = pltpu.PrefetchScalarGridSpec(
        num_scalar_prefetch=1,
        grid=(A_HEADS, nq),
        in_specs=[
            pl.BlockSpec((bq, 2 * A_QK), lambda h, i, s: (i, h)),
            pl.BlockSpec((seq, 2 * A_QK), lambda h, i, s: (0, A_HEADS + h)),
            pl.BlockSpec((seq, A_V), lambda h, i, s: (0, 2 * A_HEADS + h)),
            vec(), vec(), vec(), vec(),
            pl.BlockSpec((1, A_V), lambda h, i, s: (0, 0)),
        ] + side_in,
        out_specs=[pl.BlockSpec((bq, A_V), lambda h, i, s: (i, h))] + side_out,
        scratch_shapes=_attn_scratch(2 * bq, bq),
    )
    out = pl.pallas_call(
        functools.partial(_diff_attn_kernel, bq=bq, lambda_init=lambda_init, n_side=len(side_weights)),
        grid_spec=grid_spec,
        out_shape=[jax.ShapeDtypeStruct((seq, A_HEADS * A_V), BF16)] + side_shapes,
        compiler_params=_cparams(("arbitrary", "arbitrary"), _attn_vmem(seq, 2 * bq, bq, 2 * A_QK + A_V) + side_vmem),
        name="diff_attention",
    )(slopes, qkv, qkv, qkv, lq1.reshape(1, -1), lk1.reshape(1, -1), lq2.reshape(1, -1),
      lk2.reshape(1, -1), subln.reshape(1, -1), *[w for w, _ in side_weights])
    return out[0], out[1:]


def _mla_prep_kernel(c_ref, qn_ref, kvn_ref, wq1_ref, wq2_ref, wkn_ref, wv_ref, cos_ref, sin_ref,
                     q_ref, k_ref, v_ref, *, scale):
    c = c_ref[...]
    cqn = _rms(c[:, :B_LORA], qn_ref[...], 1e-6).astype(BF16)
    ckvn = _rms(c[:, B_LORA:2 * B_LORA], kvn_ref[...], 1e-6).astype(BF16)
    kp = c[:, 2 * B_LORA:]
    cos, sin = cos_ref[...], sin_ref[...]
    qa = jnp.dot(cqn, wq1_ref[...], preferred_element_type=F32)
    qb = jnp.dot(cqn, wq2_ref[...], preferred_element_type=F32)
    kn = jnp.dot(ckvn, wkn_ref[...], preferred_element_type=F32)
    v_ref[...] = jnp.dot(ckvn, wv_ref[...], preferred_element_type=F32).astype(v_ref.dtype)
    kr = (kp * cos + pltpu.roll(kp, B_ROPE, axis=1) * sin).astype(k_ref.dtype)
    for h in range(B_HEADS):
        lo = B_QK_PAD * h
        hi, end = lo + B_NOPE, lo + B_QK_PAD
        group = slice(V7X_LANES * h, V7X_LANES * (h + 1))
        q_ref[:, lo:hi] = (qa[:, lo:hi] * scale).astype(q_ref.dtype)
        pe = qa[:, hi:end] * cos + qb[:, group] * sin
        q_ref[:, hi:end] = (pe * scale).astype(q_ref.dtype)
        k_ref[:, lo:hi] = kn[:, group].astype(k_ref.dtype)
        k_ref[:, hi:end] = kr


def mla_prep(c, q_norm, kv_norm, wq1, wq2, wkn, wv, cos_t, sin_t, *, tm=PREP_TM):
    seq = c.shape[0]
    scale = (B_NOPE + B_ROPE) ** -0.5 * LOG2E
    full = lambda a: pl.BlockSpec(a.shape, lambda i: (0, 0))
    row = lambda w: pl.BlockSpec((tm, w), lambda i: (i, 0))
    qn, kvn = q_norm.reshape(1, -1), kv_norm.reshape(1, -1)
    qk_w, v_w = B_HEADS * B_QK_PAD, B_HEADS * B_V
    out_w = 2 * qk_w + v_w
    weights = 2 * 2 * (wq1.size + wq2.size + wkn.size + wv.size)
    vmem = (weights + 2 * tm * (c.shape[1] * 4 + 2 * V7X_LANES * 4 + out_w * 2)
            + 2 * tm * out_w * 4 + (6 << 20))
    return pl.pallas_call(
        functools.partial(_mla_prep_kernel, scale=scale),
        grid=(seq // tm,),
        in_specs=[row(c.shape[1]), full(qn), full(kvn), full(wq1), full(wq2), full(wkn), full(wv),
                  row(V7X_LANES), row(V7X_LANES)],
        out_specs=[row(qk_w), row(qk_w), row(v_w)],
        out_shape=[jax.ShapeDtypeStruct((seq, qk_w), BF16), jax.ShapeDtypeStruct((seq, qk_w), BF16),
                   jax.ShapeDtypeStruct((seq, v_w), BF16)],
        compiler_params=_cparams(("parallel",), vmem),
        name="mla_prep",
    )(c, qn, kvn, wq1, wq2, wkn, wv, cos_t, sin_t)


def _mla_attn_kernel(q_ref, k_ref, v_ref, o_ref, *scratch, bq):
    o, = _flash_attend([(q_ref[...], k_ref, v_ref, None)], pl.program_id(1), scratch, bq=bq,
                       pairs_per_trip=LONG_TRIP_PAIRS)
    o_ref[...] = o.astype(o_ref.dtype)


def mla_attention(q, k, v, *, bq=ATTN_BLOCK):
    seq = q.shape[0]
    return pl.pallas_call(
        functools.partial(_mla_attn_kernel, bq=bq),
        grid=(B_HEADS, seq // bq),
        in_specs=[
            pl.BlockSpec((bq, B_QK_PAD), lambda h, i: (i, h)),
            pl.BlockSpec((seq, B_QK_PAD), lambda h, i: (0, h)),
            pl.BlockSpec((seq, B_V), lambda h, i: (0, h)),
        ],
        out_specs=pl.BlockSpec((bq, B_V), lambda h, i: (i, h)),
        out_shape=jax.ShapeDtypeStruct((seq, B_HEADS * B_V), BF16),
        scratch_shapes=_attn_scratch(bq, bq),
        compiler_params=_cparams(("parallel", "arbitrary"), _attn_vmem(seq, bq, bq, B_QK_PAD + B_V)),
        name="mla_attention",
    )(q, k, v)


def _forget_gate_kernel(h_ref, g_ref, wf_ref, b_ref, cum_ref, carry_scr):
    @pl.when(pl.program_id(0) == 0)
    def _():
        carry_scr[...] = jnp.zeros(carry_scr.shape, F32)

    xn = _rms(h_ref[...], g_ref[...], 1e-6).astype(BF16)
    f = lax.dot_general(wf_ref[...], xn, (((1,), (1,)), ((), ())), preferred_element_type=F32)
    z = f + b_ref[...]
    log_f = jnp.minimum(z, 0.0) - jnp.log(1.0 + jnp.exp(-jnp.abs(z)))
    r = lax.broadcasted_iota(jnp.int32, (V7X_LANES, V7X_LANES), 0)
    c = lax.broadcasted_iota(jnp.int32, (V7X_LANES, V7X_LANES), 1)
    tri = (r <= c).astype(F32)
    carry = carry_scr[...]
    for j in range(log_f.shape[1] // V7X_LANES):
        sl = slice(j * V7X_LANES, (j + 1) * V7X_LANES)
        loc = jnp.dot(log_f[:, sl], tri, preferred_element_type=F32,
                      precision=lax.Precision.HIGHEST) + carry
        cum_ref[:, sl] = loc
        carry = jnp.broadcast_to(loc[:, V7X_LANES - 1:], carry.shape)
    carry_scr[...] = carry


def forget_gate_cumsum(h, g, wf_t, bias, *, tm=GATE_TM):
    seq, d = h.shape
    nh = wf_t.shape[0]
    return pl.pallas_call(
        _forget_gate_kernel,
        grid=(seq // tm,),
        in_specs=[
            pl.BlockSpec((tm, d), lambda i: (i, 0)),
            pl.BlockSpec((1, d), lambda i: (0, 0)),
            pl.BlockSpec((nh, d), lambda i: (0, 0)),
            pl.BlockSpec((nh, 1), lambda i: (0, 0)),
        ],
        out_specs=pl.BlockSpec((nh, tm), lambda i: (0, i)),
        out_shape=jax.ShapeDtypeStruct((nh, seq), F32),
        scratch_shapes=[pltpu.VMEM((nh, V7X_LANES), F32)],
        compiler_params=_cparams(("arbitrary",), 2 * tm * d * 4 + 3 * tm * d * 4 + (4 << 20)),
        name="forget_gate_cumsum",
    )(h, g.reshape(1, d), wf_t, bias.reshape(nh, 1))


def _fox_attn_kernel(q_ref, k_ref, v_ref, cum_ref, *rest, bq, n_side):
    o_ref, scratch = rest[n_side], rest[2 * n_side + 1:]
    _side_cast(rest[:n_side], rest[n_side + 1:2 * n_side + 1])
    qi = pl.program_id(1)
    t0 = pl.multiple_of(qi * bq, bq)
    c0 = cum_ref[:, pl.ds(t0, V7X_LANES)][:, :1]

    def bias_fn(start):
        return (c0 - cum_ref[:, pl.ds(start, bq // 2)]) * LOG2E

    o, = _flash_attend([(q_ref[...], k_ref, v_ref, bias_fn)], qi, scratch, bq=bq,
                       pairs_per_trip=LONG_TRIP_PAIRS)
    o_ref[...] = o.astype(o_ref.dtype)


def fox_attention(qkv, cum, side_weights, *, bq=ATTN_BLOCK):
    seq = qkv.shape[0]
    nq = seq // bq
    cum3 = cum.reshape(C_HEADS, 1, seq)
    side_in, side_out, side_shapes, side_vmem = _side_cast_specs(
        side_weights, C_HEADS * nq, lambda h, i: h * nq + i)
    out = pl.pallas_call(
        functools.partial(_fox_attn_kernel, bq=bq, n_side=len(side_weights)),
        grid=(C_HEADS, nq),
        in_specs=[
            pl.BlockSpec((bq, C_DIM), lambda h, i: (i, h)),
            pl.BlockSpec((seq, C_DIM), lambda h, i: (0, C_HEADS + h)),
            pl.BlockSpec((seq, C_DIM), lambda h, i: (0, 2 * C_HEADS + h)),
            pl.BlockSpec((None, 1, seq), lambda h, i: (h, 0, 0)),
        ] + side_in,
        out_specs=[pl.BlockSpec((bq, C_DIM), lambda h, i: (i, h))] + side_out,
        out_shape=[jax.ShapeDtypeStruct((seq, C_HEADS * C_DIM), BF16)] + side_shapes,
        scratch_shapes=_attn_scratch(bq, bq),
        compiler_params=_cparams(("arbitrary", "arbitrary"),
                                 _attn_vmem(seq, bq, bq, 2 * C_DIM) + side_vmem),
        name="fox_attention",
    )(qkv, qkv, qkv, cum3, *[w for w, _ in side_weights])
    return out[0], out[1:]


def _swiglu_accumulate(xn, wg, wu, wd, acc_ref):
    gate = jnp.dot(xn, wg, preferred_element_type=F32)
    up = jnp.dot(xn, wu, preferred_element_type=F32)
    act = (gate * jax.nn.sigmoid(gate) * up).astype(BF16)
    acc_ref[...] += jnp.dot(act, wd, preferred_element_type=F32)


def _ffn_kernel(h_ref, g_ref, wg_ref, wu_ref, wd_ref, o_ref, xn_ref, acc_ref):
    f = pl.program_id(1)

    @pl.when(f == 0)
    def _():
        xn_ref[...] = _rms(h_ref[...], g_ref[...], 1e-6).astype(BF16)
        acc_ref[...] = jnp.zeros(acc_ref.shape, F32)

    _swiglu_accumulate(xn_ref[...], wg_ref[...], wu_ref[...], wd_ref[...], acc_ref)

    @pl.when(f == pl.num_programs(1) - 1)
    def _():
        o_ref[...] = h_ref[...] + acc_ref[...]


def _ffn_vmem(tm, d, tf):
    return 4 * tm * d * 4 + tm * d * 2 + tm * d * 4 + 3 * 2 * d * tf * 2 + 4 * tm * tf * 4 + (6 << 20)


def dense_ffn(h, g, wg, wu, wd, *, tm=FFN_TM):
    m, d = h.shape
    n_f, _, tf = wg.shape
    assert m % tm == 0 and wd.shape[0] == n_f * tf
    return pl.pallas_call(
        _ffn_kernel,
        grid=(m // tm, n_f),
        in_specs=[
            pl.BlockSpec((tm, d), lambda i, f: (i, 0)),
            pl.BlockSpec((1, d), lambda i, f: (0, 0)),
            pl.BlockSpec((None, d, tf), lambda i, f: (f, 0, 0)),
            pl.BlockSpec((None, d, tf), lambda i, f: (f, 0, 0)),
            pl.BlockSpec((tf, d), lambda i, f: (f, 0)),
        ],
        out_specs=pl.BlockSpec((tm, d), lambda i, f: (i, 0)),
        out_shape=jax.ShapeDtypeStruct((m, d), F32),
        scratch_shapes=[pltpu.VMEM((tm, d), BF16), pltpu.VMEM((tm, d), F32)],
        compiler_params=_cparams(("parallel", "arbitrary"), _ffn_vmem(tm, d, tf)),
        name="dense_ffn",
    )(h, g.reshape(1, d), wg, wu, wd)


def _router_kernel(h_ref, g_ref, wr_ref, idx_ref, gate_ref, rank_ref, cnt_ref, cnt_scr):
    @pl.when(pl.program_id(0) == 0)
    def _():
        cnt_scr[...] = jnp.zeros(cnt_scr.shape, F32)

    tm = h_ref.shape[0]
    xn = _rms(h_ref[...], g_ref[...], 1e-6)
    logits = jnp.dot(xn, wr_ref[...], preferred_element_type=F32, precision=lax.Precision.HIGHEST)
    lane = lax.broadcasted_iota(jnp.int32, logits.shape, 1)
    lane_f = lane.astype(F32)
    valid = lane < N_EXPERTS
    logits = jnp.where(valid, logits, MASK_VALUE)
    e = jnp.exp(logits - jnp.max(logits, axis=1, keepdims=True))
    probs = jnp.where(valid, e / jnp.sum(e, axis=1, keepdims=True), -1.0)
    p1 = jnp.max(probs, axis=1, keepdims=True)
    i1 = jnp.min(jnp.where(probs == p1, lane_f, float(V7X_LANES)), axis=1, keepdims=True)
    rest = jnp.where(lane_f == i1, -1.0, probs)
    p2 = jnp.max(rest, axis=1, keepdims=True)
    i2 = jnp.min(jnp.where(rest == p2, lane_f, float(V7X_LANES)), axis=1, keepdims=True)
    hot1 = (lane_f == i1).astype(F32)
    hot2 = (lane_f == i2).astype(F32)
    r = lax.broadcasted_iota(jnp.int32, (tm, tm), 0)
    c = lax.broadcasted_iota(jnp.int32, (tm, tm), 1)
    strict_lower = (c < r).astype(BF16)
    before = jnp.dot(strict_lower, (hot1 + hot2).astype(BF16), preferred_element_type=F32)
    before = before + cnt_scr[...]
    rank1 = jnp.sum(before * hot1, axis=1, keepdims=True)
    rank2 = jnp.sum(before * hot2, axis=1, keepdims=True)
    cnt_scr[...] += jnp.sum(hot1 + hot2, axis=0, keepdims=True)
    cnt_ref[...] = cnt_scr[...]
    denom = p1 + p2
    idx_ref[...] = jnp.where(lane == 0, i1, jnp.where(lane == 1, i2, 0.0)).astype(jnp.int32)
    gate_ref[...] = jnp.where(lane == 0, p1 / denom, jnp.where(lane == 1, p2 / denom, 0.0))
    rank_ref[...] = jnp.where(lane == 0, rank1, jnp.where(lane == 1, rank2, 0.0)).astype(jnp.int32)


def moe_router(h, g, w_router_padded, *, tm=ROUTER_TM):
    seq, d = h.shape
    out = lambda dt: jax.ShapeDtypeStruct((seq, V7X_LANES), dt)
    row = pl.BlockSpec((tm, V7X_LANES), lambda i: (i, 0))
    return pl.pallas_call(
        _router_kernel,
        grid=(seq // tm,),
        in_specs=[
            pl.BlockSpec((tm, d), lambda i: (i, 0)),
            pl.BlockSpec((1, d), lambda i: (0, 0)),
            pl.BlockSpec((d, V7X_LANES), lambda i: (0, 0)),
        ],
        out_specs=[row, row, row, pl.BlockSpec((1, V7X_LANES), lambda i: (0, 0))],
        out_shape=[out(jnp.int32), out(F32), out(jnp.int32),
                   jax.ShapeDtypeStruct((1, V7X_LANES), F32)],
        scratch_shapes=[pltpu.VMEM((1, V7X_LANES), F32)],
        compiler_params=_cparams(("arbitrary",), 5 * tm * d * 4 + 4 * tm * tm * 4 + (6 << 20)),
        name="moe_router",
    )(h, g.reshape(1, d), w_router_padded)


def _moe_ffn_kernel(te_ref, nv_ref, rows_ref, next_rows_ref, dest_ref, prev_dest_ref, h_hbm, g_ref,
                    wg_ref, wu_ref, wd_ref, y_hbm, x_buf, y_buf, xn_ref, acc_ref, in_sem, out_sem):
    i, f = pl.program_id(0), pl.program_id(1)
    n_tiles, n_f = pl.num_programs(0), pl.num_programs(1)
    live = i < nv_ref[0]
    tm = xn_ref.shape[0]
    n_rows = x_buf.shape[1]
    rows_per_step = n_rows // n_f
    slot = i % 2

    def gather_copy(idx_ref, r, dst_slot):
        return pltpu.make_async_copy(h_hbm.at[pl.ds(idx_ref[0, 0, r], 1)],
                                     x_buf.at[dst_slot, pl.ds(r, 1)], in_sem.at[dst_slot])

    def scatter_copy(idx_ref, r, src_slot):
        src_row = jnp.minimum(r, tm - 1)
        return pltpu.make_async_copy(y_buf.at[src_slot, pl.ds(src_row, 1)],
                                     y_hbm.at[pl.ds(idx_ref[0, 0, r], 1)], out_sem.at[src_slot])

    def wait_gather(dst_slot):
        pltpu.make_async_copy(h_hbm.at[pl.ds(0, n_rows)], x_buf.at[dst_slot], in_sem.at[dst_slot]).wait()

    def wait_scatter(src_slot):
        pltpu.make_async_copy(y_buf.at[src_slot], y_hbm.at[pl.ds(0, n_rows)], out_sem.at[src_slot]).wait()

    def step_rows():
        return [f * rows_per_step + k for k in range(rows_per_step)]

    def gather_next_tile_rows():
        for r in step_rows():
            gather_copy(next_rows_ref, r, 1 - slot).start()

    def scatter_prev_tile_rows():
        for r in step_rows():
            scatter_copy(prev_dest_ref, r, 1 - slot).start()

    @pl.when((f == 0) & (i == 0))
    def _():
        def body(r, carry):
            gather_copy(rows_ref, r, slot).start()
            return carry
        lax.fori_loop(0, n_rows, body, 0)

    @pl.when(f == 0)
    def _():
        wait_gather(slot)
        xn_ref[...] = _rms(x_buf[slot, :tm], g_ref[...], 1e-6).astype(BF16)
        acc_ref[...] = jnp.zeros(acc_ref.shape, F32)

    @pl.when(live & (i > 0))
    def _():
        gather_next_tile_rows()
        scatter_prev_tile_rows()
        _swiglu_accumulate(xn_ref[...], wg_ref[...], wu_ref[...], wd_ref[...], acc_ref)

    @pl.when(live & (i == 0))
    def _():
        _swiglu_accumulate(xn_ref[...], wg_ref[...], wu_ref[...], wd_ref[...], acc_ref)
        gather_next_tile_rows()

    @pl.when(jnp.logical_not(live))
    def _():
        gather_next_tile_rows()
        scatter_prev_tile_rows()

    @pl.when(f == n_f - 1)
    def _():
        @pl.when(i >= 2)
        def _():
            wait_scatter(slot)
        y_buf[slot, :tm] = acc_ref[...]

    @pl.when((f == n_f - 1) & (i == n_tiles - 1))
    def _():
        def body(r, carry):
            scatter_copy(dest_ref, r, slot).start()
            return carry
        lax.fori_loop(0, n_rows, body, 0)
        wait_gather(1 - slot)
        wait_scatter(1 - slot)
        wait_scatter(slot)


def _moe_tile_rows(tm, nf):
    return -(-tm // (8 * nf)) * 8 * nf


def moe_ffn(h, row_token, row_dest, n_y_rows, g, wg, wu, wd, tile_expert, n_live, *, tm=MOE_TM):
    d = h.shape[1]
    nf, _, _, tf = wg.shape
    n_tiles, _, n_rows = row_token.shape
    assert wd.shape[1] == nf * tf and n_rows == _moe_tile_rows(tm, nf) and n_tiles >= 2

    def f_idx(i, f, nv):
        return jnp.where(i < nv[0], f, nf - 1)

    grid_spec = pltpu.PrefetchScalarGridSpec(
        num_scalar_prefetch=2,
        grid=(n_tiles, nf),
        in_specs=[
            pl.BlockSpec((1, 1, n_rows), lambda i, f, te, nv: (i, 0, 0), memory_space=pltpu.SMEM),
            pl.BlockSpec((1, 1, n_rows), lambda i, f, te, nv: (jnp.minimum(i + 1, n_tiles - 1), 0, 0),
                         memory_space=pltpu.SMEM),
            pl.BlockSpec((1, 1, n_rows), lambda i, f, te, nv: (i, 0, 0), memory_space=pltpu.SMEM),
            pl.BlockSpec((1, 1, n_rows), lambda i, f, te, nv: (jnp.maximum(i - 1, 0), 0, 0),
                         memory_space=pltpu.SMEM),
            pl.BlockSpec(memory_space=pl.ANY),
            pl.BlockSpec((1, d), lambda i, f, te, nv: (0, 0)),
            pl.BlockSpec((None, None, d, tf), lambda i, f, te, nv: (f_idx(i, f, nv), te[i], 0, 0)),
            pl.BlockSpec((None, None, d, tf), lambda i, f, te, nv: (f_idx(i, f, nv), te[i], 0, 0)),
            pl.BlockSpec((None, tf, d), lambda i, f, te, nv: (te[i], f_idx(i, f, nv), 0)),
        ],
        out_specs=pl.BlockSpec(memory_space=pl.ANY),
        scratch_shapes=[pltpu.VMEM((2, n_rows, d), F32), pltpu.VMEM((2, n_rows, d), F32),
                        pltpu.VMEM((tm, d), BF16), pltpu.VMEM((tm, d), F32),
                        pltpu.SemaphoreType.DMA((2,)), pltpu.SemaphoreType.DMA((2,))],
    )
    vmem = 4 * n_rows * d * 4 + tm * d * (2 + 4) + 3 * 2 * d * tf * 2 + 4 * tm * tf * 4 + (6 << 20)
    return pl.pallas_call(
        _moe_ffn_kernel,
        grid_spec=grid_spec,
        out_shape=jax.ShapeDtypeStruct((n_y_rows, d), F32),
        compiler_params=_cparams(("arbitrary", "arbitrary"), vmem),
        name="moe_ffn",
    )(tile_expert, n_live, row_token, row_token, row_dest, row_dest, h, g.reshape(1, d), wg, wu, wd)


def _combine_kernel(y0_ref, y1_ref, h_ref, gate_ref, fg_ref, o_ref):
    gate = gate_ref[...]
    out = h_ref[...] + gate[:, 0:1] * y0_ref[...] + gate[:, 1:2] * y1_ref[...]
    o_ref[...] = _rms(out, fg_ref[...], 1e-6)


def moe_combine(y, h, gates, final_gain, *, tc=COMBINE_ROWS):
    seq, d = h.shape
    n_tiles = seq // tc
    row = lambda w, off: pl.BlockSpec((tc, w), lambda i: (i + off, 0))
    return pl.pallas_call(
        _combine_kernel,
        grid=(n_tiles,),
        in_specs=[row(d, 0), row(d, n_tiles), row(d, 0), row(V7X_LANES, 0),
                  pl.BlockSpec((1, d), lambda i: (0, 0))],
        out_specs=row(d, 0),
        out_shape=jax.ShapeDtypeStruct((seq, d), F32),
        compiler_params=_cparams(("parallel",), 10 * tc * d * 4 + (4 << 20)),
        name="moe_combine",
    )(y, y, h, gates, final_gain.reshape(1, d))


def _swap_halves(w):
    half = w.shape[-1] // 2
    return jnp.concatenate([w[..., half:], w[..., :half]], axis=-1)


def _mla_weights(w_uq, w_ukv):
    lora = w_uq.shape[0]
    uq = w_uq.reshape(lora, B_HEADS, B_NOPE + B_ROPE)
    nope, pe = uq[..., :B_NOPE], uq[..., B_NOPE:]
    zpad = jnp.zeros((lora, B_HEADS, B_QK_PAD - B_NOPE - B_ROPE), w_uq.dtype)
    wq1 = jnp.concatenate([nope, pe, zpad], axis=-1).reshape(lora, B_HEADS * B_QK_PAD)
    wq2 = jnp.concatenate([_swap_halves(pe), zpad], axis=-1).reshape(lora, B_HEADS * V7X_LANES)
    ukv = w_ukv.reshape(lora, B_HEADS, B_NOPE + B_V)
    wkn = ukv[..., :B_NOPE].reshape(lora, B_HEADS * B_NOPE)
    wv = ukv[..., B_NOPE:].reshape(lora, B_HEADS * B_V)
    return wq1.astype(BF16), wq2.astype(BF16), wkn.astype(BF16), wv.astype(BF16)


def _rope_tables(seq):
    inv = ROPE_THETA ** (-jnp.arange(0, B_ROPE, 2, dtype=F32) / B_ROPE)
    ang = jnp.arange(seq, dtype=F32)[:, None] * inv[None, :]
    cos, sin = jnp.cos(ang), jnp.sin(ang)
    zeros = jnp.zeros((seq, 128 - B_ROPE), F32)
    return (jnp.concatenate([cos, cos, zeros], axis=1), jnp.concatenate([-sin, sin, zeros], axis=1))


def _dispatch_plan(idx, rank, counts, tm, n_rows):
    seq = idx.shape[0]
    n_tiles = seq * TOP_K // tm + N_EXPERTS
    cnt = counts[0, :N_EXPERTS].astype(jnp.int32)
    tiles_per = (cnt + tm - 1) // tm
    tile_end = jnp.cumsum(tiles_per)
    row_start = (tile_end - tiles_per) * tm
    pos = (row_start[idx[:, :TOP_K]] + rank[:, :TOP_K]).reshape(-1)
    entry = (pos // tm) * n_rows + pos % tm
    tile_id = jnp.arange(n_tiles, dtype=jnp.int32)
    spare = (TOP_K * seq + (tile_id[:, None] % 2) * n_rows
             + jnp.arange(n_rows, dtype=jnp.int32)[None, :]).reshape(-1)
    pair = jnp.full((n_tiles * n_rows,), -1, jnp.int32).at[entry].set(
        jnp.arange(seq * TOP_K, dtype=jnp.int32))
    tok, slot = pair // TOP_K, pair % TOP_K
    row_token = jnp.where(pair >= 0, tok, 0)
    row_dest = jnp.where(pair >= 0, slot * seq + tok, spare)
    n_live = tile_end[-1]
    tile_expert = jnp.sum((tile_id[:, None] >= tile_end[None, :]).astype(jnp.int32), axis=1)
    last_expert = jnp.sum((n_live - 1 >= tile_end).astype(jnp.int32))
    tile_expert = jnp.where(tile_id < n_live, tile_expert, last_expert).astype(jnp.int32)
    shape3 = (n_tiles, 1, n_rows)
    return (row_token.reshape(shape3), row_dest.reshape(shape3), tile_expert,
            n_live.reshape(1).astype(jnp.int32), TOP_K * seq + 2 * n_rows)


def kernel(x, ev_attn_norm, ev_w_in, ev_q_norm, ev_w_uq, ev_kv_norm, ev_w_ukv, ev_lambda_q1, ev_lambda_k1, ev_lambda_q2, ev_lambda_k2, ev_subln, ev_w_out, ev_ffn_norm, ev_ffn_w_gate, ev_ffn_w_up, ev_ffn_w_down, od_attn_norm, od_w_in, od_forget_bias, od_w_out, od_ffn_norm, od_router, od_moe_w_gate, od_moe_w_up, od_moe_w_down, final_norm):
    batch, seq, d = x.shape
    assert batch == 1
    h = x.reshape(seq, d)

    lambda_init = 0.8 - 0.6 * math.exp(-0.3 * 0)
    w_in = ev_w_in[0]
    n_a = 2 * A_HEADS * 2 * A_QK + A_HEADS * A_V
    w_a = w_in[:, :n_a].astype(BF16)
    scale_a = jnp.concatenate([jnp.full((A_HEADS * 2 * A_QK,), A_QK ** -0.5 * LOG2E, F32),
                               jnp.ones((n_a - A_HEADS * 2 * A_QK,), F32)])
    w_b = jnp.concatenate([w_in[:, n_a:], _swap_halves(w_in[:, -B_ROPE:])], axis=1).astype(BF16)
    qkv_a = rms_matmul(h, ev_attn_norm[0], w_a, scale_a, BF16)
    c_b = rms_matmul(h, ev_attn_norm[0], w_b, jnp.ones((w_b.shape[1],), F32), F32, tn=w_b.shape[1])
    oa, (ffn_wg, ffn_wu, ffn_wd) = diff_attention(
        qkv_a, ev_lambda_q1[0], ev_lambda_k1[0], ev_lambda_q2[0], ev_lambda_k2[0], ev_subln[0],
        lambda_init, [(ev_ffn_w_gate[0], FFN_TF), (ev_ffn_w_up[0], FFN_TF), (ev_ffn_w_down[0], None)])
    wq1, wq2, wkn, wv = _mla_weights(ev_w_uq[0], ev_w_ukv[0])
    cos_t, sin_t = _rope_tables(seq)
    q_b, k_b, v_b = mla_prep(c_b, ev_q_norm[0], ev_kv_norm[0], wq1, wq2, wkn, wv, cos_t, sin_t)
    ob = mla_attention(q_b, k_b, v_b)
    w_out = ev_w_out[0].astype(BF16)
    h = proj_residual([oa, ob], [w_out[:A_HEADS * A_V], w_out[A_HEADS * A_V:]], h)
    h = dense_ffn(h, ev_ffn_norm[0], ffn_wg, ffn_wu, ffn_wd)

    width = C_HEADS * C_DIM
    w_qkv = od_w_in[0][:, :3 * width].astype(BF16)
    scale_c = jnp.concatenate([jnp.full((width,), C_DIM ** -0.5 * LOG2E, F32), jnp.ones((2 * width,), F32)])
    qkv_c = rms_matmul(h, od_attn_norm[0], w_qkv, scale_c, BF16)
    wf_t = od_w_in[0][:, 3 * width:].T.astype(BF16)
    cum = forget_gate_cumsum(h, od_attn_norm[0], wf_t, od_forget_bias[0])
    n_e, _, ff = od_moe_w_gate[0].shape
    oc, (moe_wg, moe_wu, moe_wd) = fox_attention(qkv_c, cum, [
        (od_moe_w_gate[0].reshape(n_e * d, ff), FFN_TF), (od_moe_w_up[0].reshape(n_e * d, ff), FFN_TF),
        (od_moe_w_down[0].reshape(n_e * ff, d), None)])
    h = proj_residual([oc], [od_w_out[0].astype(BF16)], h)

    w_router = jnp.zeros((d, V7X_LANES), F32).at[:, :N_EXPERTS].set(od_router[0])
    idx, gates, rank, counts = moe_router(h, od_ffn_norm[0], w_router)
    n_rows = _moe_tile_rows(MOE_TM, od_moe_w_gate.shape[-1] // FFN_TF)
    row_token, row_dest, tile_expert, n_live, n_y_rows = _dispatch_plan(idx, rank, counts, MOE_TM, n_rows)
    blocked = (ff // FFN_TF, n_e, d, FFN_TF)
    y = moe_ffn(h, row_token, row_dest, n_y_rows, od_ffn_norm[0], moe_wg.reshape(blocked),
                moe_wu.reshape(blocked), moe_wd.reshape(n_e, ff, d), tile_expert, n_live)
    out = moe_combine(y, h, gates, final_norm)
    return out.reshape(batch, seq, d)
```
